```python
import math
import jax, jax.numpy as jnp
from jax import lax
import numpy as np

D_MODEL = 2048
BATCH = 4
SEQ = 4096
DEPTH = 2

GRID_W = 64
CTX_LEN = 256
HEAD_DIM = 128
BRANCH_W = 512
N_BRANCH = 4
HY_W = 512
HY_ORDER = 2
HY_BANDS = 16
HY_EMB = 1 + 2 * HY_BANDS
HY_FFN = 64
HY_FAST_DECAY = 0.3
HY_SLOW_DECAY = 1.5
HY_TARGET = 1e-2
GQA_HEADS = 4
GQA_KV = 2
NA_HEADS = 4
NA_WIN_H = 8
NA_WIN_W = 16
DF_HEADS = 4
DF_QK_DIM = 64
DF_V_DIM = 128
N_EXPERTS = 16
EXPERT_FF = 2048
CAPACITY_FACTOR = 2
ROPE_THETA = 10000.0
Q_BLOCK = 128
NORM_EPS = 1e-6
ATT_SCALE = HEAD_DIM ** -0.5
DF_SCALE = DF_QK_DIM ** -0.5
IN_WIDTH = 3 * HY_W + (GQA_HEADS + 2 * GQA_KV) * HEAD_DIM + 3 * NA_HEADS * HEAD_DIM + DF_HEADS * (4 * DF_QK_DIM + DF_V_DIM) + N_BRANCH * D_MODEL

kernel_name = 'hybrid_diffusion_hyena_gqa_natten_diff_ecmoe'


def rms_norm(x, g):
    xf = x.astype(jnp.float32)
    y = xf * lax.rsqrt(jnp.mean(xf * xf, axis=-1, keepdims=True) + NORM_EPS)
    return (y * g.astype(jnp.float32)).astype(x.dtype)


def modulation(cvec, w_mod, b_mod):
    m = jax.nn.silu(cvec) @ w_mod + b_mod
    return jnp.split(m, 6, axis=-1)


def rope_2d(x, rows, cols):
    d = x.shape[-1]
    half = d // 2
    inv = ROPE_THETA ** (-jnp.arange(0, half, 2, dtype=jnp.float32) / half)

    def rot(xa, p):
        ang = p.astype(jnp.float32)[:, None] * inv[None, :]
        cos = jnp.cos(ang)[None, :, None, :]
        sin = jnp.sin(ang)[None, :, None, :]
        a, b = jnp.split(xa.astype(jnp.float32), 2, axis=-1)
        return jnp.concatenate([a * cos - b * sin, b * cos + a * sin], axis=-1)

    return jnp.concatenate([rot(x[..., :half], rows), rot(x[..., half:], cols)], axis=-1).astype(x.dtype)


def short_conv(u, w):
    L = u.shape[1]
    up = jnp.pad(u, ((0, 0), (1, 1), (0, 0)))
    return up[:, :L] * w[0] + up[:, 1:L + 1] * w[1] + up[:, 2:] * w[2]


def hyena_filters(L, f1, b1, freq, f2, b2, f3):
    f32 = jnp.float32
    t = jnp.linspace(0.0, 1.0, L, dtype=f32)[:, None]
    w = (2.0 * math.pi / L) * jnp.arange(L, dtype=f32)[:, None]
    bands = jnp.linspace(1e-4, HY_BANDS - 1, HY_BANDS, dtype=f32)[None, :]
    z = jnp.concatenate([t, jnp.cos(bands * w), -jnp.sin(bands * w)], axis=-1)
    fr = freq.astype(f32)
    hid = jnp.sin(fr * (z @ f1.astype(f32) + b1.astype(f32)))
    hid = jnp.sin(fr * (hid @ f2.astype(f32) + b2.astype(f32)))
    h = (hid @ f3.astype(f32)).reshape(L, 2, HY_ORDER, HY_W)
    deltas = jnp.abs(jnp.linspace(math.log(HY_TARGET) / HY_SLOW_DECAY, math.log(HY_TARGET) / HY_FAST_DECAY, HY_W, dtype=f32))
    h = h * jnp.exp(-t[:, :, None, None] * deltas)
    fwd, bwd = h[:, 0], h[:, 1]
    kern = jnp.concatenate([fwd, jnp.zeros_like(fwd[:1]), bwd[:0:-1]], axis=0)
    kern = kern / jnp.sum(jnp.abs(kern), axis=0, keepdims=True)
    return jnp.fft.rfft(kern, axis=0)


def hyena_mixer(u, hy_short, kern_f, hy_bias):
    L = u.shape[1]
    uc = short_conv(u, hy_short)
    v, x1, x2 = jnp.split(uc.astype(jnp.float32), 3, axis=-1)
    bias = hy_bias.astype(jnp.float32)
    z = v
    for o, g in enumerate((x1, x2)):
        zf = jnp.fft.rfft(z, n=2 * L, axis=1)
        conv = jnp.fft.irfft(zf * kern_f[None, :, o], n=2 * L, axis=1)[:, :L]
        z = g * (conv + bias[o] * z)
    return z.astype(u.dtype)


def block_attention(q, k, v, scale):
    B, T, H, d = q.shape
    G = k.shape[2]
    R = H // G
    nb = T // Q_BLOCK
    qb = q.reshape(B, nb, Q_BLOCK, G, R, d).swapaxes(0, 1)

    def one(qi):
        s = jnp.einsum('bqgrd,bkgd->bgrqk', qi, k, preferred_element_type=jnp.float32) * scale
        p = jax.nn.softmax(s, axis=-1).astype(v.dtype)
        return jnp.einsum('bgrqk,bkge->bqgre', p, v)

    o = lax.map(one, qb)
    return o.swapaxes(0, 1).reshape(B, T, H, v.shape[-1])


def diff_attention(q, k, v, lam, scale):
    B, T, H = q.shape[:3]
    nb = T // Q_BLOCK
    qb = q.reshape(B, nb, Q_BLOCK, H, 2, q.shape[-1]).swapaxes(0, 1)

    def one(qi):
        s = jnp.einsum('bqhjd,bkhjd->bjhqk', qi, k, preferred_element_type=jnp.float32) * scale
        p = jax.nn.softmax(s, axis=-1)
        wgt = (p[:, 0] - lam * p[:, 1]).astype(v.dtype)
        return jnp.einsum('bhqk,bkhe->bqhe', wgt, v)

    o = lax.map(one, qb)
    return o.swapaxes(0, 1).reshape(B, T, H, v.shape[-1])


def neighbourhood_attention(q, k, v, kc, vc, rpb, scale):
    B, T, H, d = q.shape
    rows = T // GRID_W
    wh, ww = min(NA_WIN_H, rows), NA_WIN_W
    qg = q.reshape(B, rows, GRID_W, H, d)
    kg = k.reshape(B, rows, GRID_W, H, d)
    vg = v.reshape(B, rows, GRID_W, H, d)
    col = np.arange(GRID_W)
    col_idx = np.clip(col - ww // 2, 0, GRID_W - ww)[:, None] + np.arange(ww)[None, :]
    coff = col_idx - col[:, None] + NA_WIN_W - 1
    nk = wh * ww

    def one_row(r):
        rs = jnp.clip(r - wh // 2, 0, rows - wh)
        roff = rs + jnp.arange(wh) - r + NA_WIN_H - 1
        bias = rpb[:, roff[None, :, None], coff[:, None, :]].reshape(H, GRID_W, nk).astype(jnp.float32)
        kr = lax.dynamic_slice_in_dim(kg, rs, wh, axis=1)
        vr = lax.dynamic_slice_in_dim(vg, rs, wh, axis=1)
        kn = kr[:, :, col_idx].transpose(0, 2, 1, 3, 4, 5).reshape(B, GRID_W, nk, H, d)
        vn = vr[:, :, col_idx].transpose(0, 2, 1, 3, 4, 5).reshape(B, GRID_W, nk, H, d)
        qr = lax.dynamic_index_in_dim(qg, r, axis=1, keepdims=False)
        s_n = jnp.einsum('bqhd,bqkhd->bhqk', qr, kn, preferred_element_type=jnp.float32) * scale + bias[None]
        s_c = jnp.einsum('bqhd,bkhd->bhqk', qr, kc, preferred_element_type=jnp.float32) * scale
        p = jax.nn.softmax(jnp.concatenate([s_n, s_c], axis=-1), axis=-1).astype(v.dtype)
        return jnp.einsum('bhqk,bqkhd->bqhd', p[..., :nk], vn) + jnp.einsum('bhqk,bkhd->bqhd', p[..., nk:], vc)

    out = lax.map(one_row, jnp.arange(rows))
    return out.transpose(1, 0, 2, 3, 4).reshape(B, T, H, d)


def project_stream(h, w_in, qn_b, kn_b, qn_c, kn_c, qn_d, kn_d, rows, cols):
    B, T, _ = h.shape
    sizes = [3 * HY_W, GQA_HEADS * HEAD_DIM, GQA_KV * HEAD_DIM, GQA_KV * HEAD_DIM,
             NA_HEADS * HEAD_DIM, NA_HEADS * HEAD_DIM, NA_HEADS * HEAD_DIM,
             DF_HEADS * 2 * DF_QK_DIM, DF_HEADS * 2 * DF_QK_DIM, DF_HEADS * DF_V_DIM, N_BRANCH * D_MODEL]
    parts = jnp.split(h @ w_in, np.cumsum(sizes)[:-1].tolist(), axis=-1)
    hy, q_b, k_b, v_b, q_c, k_c, v_c, q_d, k_d, v_d, gates = parts
    q_b = rms_norm(q_b.reshape(B, T, GQA_HEADS, HEAD_DIM), qn_b)
    k_b = rms_norm(k_b.reshape(B, T, GQA_KV, HEAD_DIM), kn_b)
    q_c = rms_norm(q_c.reshape(B, T, NA_HEADS, HEAD_DIM), qn_c)
    k_c = rms_norm(k_c.reshape(B, T, NA_HEADS, HEAD_DIM), kn_c)
    q_d = rms_norm(q_d.reshape(B, T, DF_HEADS * 2, DF_QK_DIM), qn_d)
    k_d = rms_norm(k_d.reshape(B, T, DF_HEADS * 2, DF_QK_DIM), kn_d)
    if rows is not None:
        q_b, k_b = rope_2d(q_b, rows, cols), rope_2d(k_b, rows, cols)
        q_d, k_d = rope_2d(q_d, rows, cols), rope_2d(k_d, rows, cols)
    return dict(hy=hy, q_b=q_b, k_b=k_b, v_b=v_b.reshape(B, T, GQA_KV, HEAD_DIM),
                q_c=q_c, k_c=k_c, v_c=v_c.reshape(B, T, NA_HEADS, HEAD_DIM),
                q_d=q_d.reshape(B, T, DF_HEADS, 2, DF_QK_DIM), k_d=k_d.reshape(B, T, DF_HEADS, 2, DF_QK_DIM),
                v_d=v_d.reshape(B, T, DF_HEADS, DF_V_DIM), gates=gates)


def token_mixing(p, pc, kern_f, hy_short, hy_bias, rpb, lam, lam_init, subln_d, w_branch, w_out):
    B, T = p['gates'].shape[:2]
    y_a = hyena_mixer(p['hy'], hy_short, kern_f, hy_bias)
    if pc is None:
        k_b, v_b, k_d, v_d = p['k_b'], p['v_b'], p['k_d'], p['v_d']
        y_c = block_attention(p['q_c'], p['k_c'], p['v_c'], ATT_SCALE)
    else:
        k_b = jnp.concatenate([p['k_b'], pc['k_b']], axis=1)
        v_b = jnp.concatenate([p['v_b'], pc['v_b']], axis=1)
        k_d = jnp.concatenate([p['k_d'], pc['k_d']], axis=1)
        v_d = jnp.concatenate([p['v_d'], pc['v_d']], axis=1)
        y_c = neighbourhood_attention(p['q_c'], p['k_c'], p['v_c'], pc['k_c'], pc['v_c'], rpb, ATT_SCALE)
    y_b = block_attention(p['q_b'], k_b, v_b, ATT_SCALE)
    y_d = rms_norm(diff_attention(p['q_d'], k_d, v_d, lam, DF_SCALE), subln_d) * (1.0 - lam_init)
    ys = jnp.stack([y_a, y_b.reshape(B, T, BRANCH_W), y_c.reshape(B, T, BRANCH_W), y_d.reshape(B, T, BRANCH_W)], axis=2)
    g = jax.nn.sigmoid(p['gates'].reshape(B, T, N_BRANCH, D_MODEL).astype(jnp.float32)).astype(ys.dtype)
    merged = jnp.sum(jnp.einsum('btnc,ncd->btnd', ys, w_branch) * g, axis=2)
    return merged @ w_out


def expert_choice_moe(h, w_router, w1, w3, w2):
    B, T, D = h.shape
    cap = CAPACITY_FACTOR * T // N_EXPERTS
    aff = jax.nn.softmax(jnp.einsum('btd,de->bte', h, w_router, preferred_element_type=jnp.float32), axis=-1)
    gate, idx = lax.top_k(aff.transpose(0, 2, 1), cap)
    xg = jax.vmap(lambda hb, ib: hb[ib])(h, idx)
    a = jnp.einsum('becd,edf->becf', xg, w1)
    b = jnp.einsum('becd,edf->becf', xg, w3)
    y = jnp.einsum('becf,efd->becd', jax.nn.silu(a) * b, w2) * gate[..., None].astype(h.dtype)
    return jax.vmap(lambda yb, ib: jnp.zeros((T, D), yb.dtype).at[ib.reshape(-1)].add(yb.reshape(-1, D)))(y, idx)


def setup_inputs(seed: int = 0) -> dict:
    key = jax.random.key(seed)
    keys = jax.random.split(key, 40)
    counter = [0]

    def nrm(shape, scale=1.0):
        k = keys[counter[0]]
        counter[0] += 1
        return jax.random.normal(k, shape, jnp.float32) * scale

    L, D = DEPTH, D_MODEL
    return {
        'x': nrm((BATCH, SEQ, D)),
        'c': nrm((BATCH, D)),
        'ctx': nrm((BATCH, CTX_LEN, D)),
        'c_ctx': nrm((D,)),
        'w_mod': nrm((L, D, 6 * D), 0.5 * D ** -0.5),
        'b_mod': nrm((L, 6 * D), 0.01),
        'norm1': 1.0 + nrm((L, D), 0.02),
        'norm2': 1.0 + nrm((L, D), 0.02),
        'w_in': nrm((L, D, IN_WIDTH), D ** -0.5),
        'hy_short': nrm((L, 3, 3 * HY_W), 3 ** -0.5),
        'hy_f1': nrm((L, HY_EMB, HY_FFN), HY_EMB ** -0.5),
        'hy_b1': nrm((L, HY_FFN), 0.1),
        'hy_freq': 1.0 + nrm((L, HY_FFN), 0.1),
        'hy_f2': nrm((L, HY_FFN, HY_FFN), HY_FFN ** -0.5),
        'hy_b2': nrm((L, HY_FFN), 0.1),
        'hy_f3': nrm((L, HY_FFN, 2 * HY_ORDER * HY_W), HY_FFN ** -0.5),
        'hy_bias': nrm((L, HY_ORDER, HY_W), 0.5),
        'qn_b': 1.0 + nrm((L, HEAD_DIM), 0.02),
        'kn_b': 1.0 + nrm((L, HEAD_DIM), 0.02),
        'qn_c': 1.0 + nrm((L, HEAD_DIM), 0.02),
        'kn_c': 1.0 + nrm((L, HEAD_DIM), 0.02),
        'rpb_c': nrm((L, NA_HEADS, 2 * NA_WIN_H - 1, 2 * NA_WIN_W - 1), 0.1),
        'qn_d': 1.0 + nrm((L, DF_QK_DIM), 0.02),
        'kn_d': 1.0 + nrm((L, DF_QK_DIM), 0.02),
        'lam_q1': nrm((L, DF_QK_DIM), 0.1),
        'lam_k1': nrm((L, DF_QK_DIM), 0.1),
        'lam_q2': nrm((L, DF_QK_DIM), 0.1),
        'lam_k2': nrm((L, DF_QK_DIM), 0.1),
        'subln_d': 1.0 + nrm((L, DF_V_DIM), 0.02),
        'w_branch': nrm((L, N_BRANCH, BRANCH_W, D), BRANCH_W ** -0.5),
        'w_out': nrm((L, D, D), D ** -0.5),
        'w_router': nrm((L, D, N_EXPERTS), D ** -0.5),
        'w_e1': nrm((L, N_EXPERTS, D, EXPERT_FF), D ** -0.5),
        'w_e3': nrm((L, N_EXPERTS, D, EXPERT_FF), D ** -0.5),
        'w_e2': nrm((L, N_EXPERTS, EXPERT_FF, D), EXPERT_FF ** -0.5),
    }


def reference(x, c, ctx, c_ctx, w_mod, b_mod, norm1, norm2, w_in, hy_short, hy_f1, hy_b1, hy_freq, hy_f2, hy_b2,
              hy_f3, hy_bias, qn_b, kn_b, qn_c, kn_c, rpb_c, qn_d, kn_d, lam_q1, lam_k1, lam_q2, lam_k2, subln_d,
              w_branch, w_out, w_router, w_e1, w_e3, w_e2):
    f32 = jnp.float32
    T = x.shape[1]
    pos = jnp.arange(T, dtype=jnp.int32)
    rows, cols = pos // GRID_W, pos % GRID_W
    for l in range(DEPTH):
        last = l == DEPTH - 1
        lam_init = 0.8 - 0.6 * math.exp(-0.3 * l)
        lam = (jnp.exp(jnp.sum(lam_q1[l].astype(f32) * lam_k1[l].astype(f32)))
               - jnp.exp(jnp.sum(lam_q2[l].astype(f32) * lam_k2[l].astype(f32))) + lam_init)
        sh1, sc1, g1, sh2, sc2, g2 = modulation(c[:, None, :], w_mod[l], b_mod[l])
        csh1, csc1, cg1, csh2, csc2, cg2 = modulation(c_ctx, w_mod[l], b_mod[l])
        hc = rms_norm(ctx, norm1[l]) * (1.0 + csc1) + csh1
        pc = project_stream(hc, w_in[l], qn_b[l], kn_b[l], qn_c[l], kn_c[l], qn_d[l], kn_d[l], None, None)
        hx = rms_norm(x, norm1[l]) * (1.0 + sc1) + sh1
        px = project_stream(hx, w_in[l], qn_b[l], kn_b[l], qn_c[l], kn_c[l], qn_d[l], kn_d[l], rows, cols)
        kern_x = hyena_filters(T, hy_f1[l], hy_b1[l], hy_freq[l], hy_f2[l], hy_b2[l], hy_f3[l])
        x = x + g1 * token_mixing(px, pc, kern_x, hy_short[l], hy_bias[l], rpb_c[l], lam, lam_init,
                                  subln_d[l], w_branch[l], w_out[l])
        hx2 = rms_norm(x, norm2[l]) * (1.0 + sc2) + sh2
        x = x + g2 * expert_choice_moe(hx2, w_router[l], w_e1[l], w_e3[l], w_e2[l])
        if not last:
            kern_c = hyena_filters(ctx.shape[1], hy_f1[l], hy_b1[l], hy_freq[l], hy_f2[l], hy_b2[l], hy_f3[l])
            ctx = ctx + cg1 * token_mixing(pc, None, kern_c, hy_short[l], hy_bias[l], rpb_c[l], lam, lam_init,
                                           subln_d[l], w_branch[l], w_out[l])
            hc2 = rms_norm(ctx, norm2[l]) * (1.0 + csc2) + csh2
            ctx = ctx + cg2 * expert_choice_moe(hc2, w_router[l], w_e1[l], w_e3[l], w_e2[l])
    return x
```

```python
import functools
import math

import numpy as np
import jax
import jax.numpy as jnp
from jax import lax
from jax.experimental import pallas as pl
from jax.experimental.pallas import tpu as pltpu

F32 = jnp.float32
BF16 = jnp.bfloat16

GRID_W = 64
HEAD_DIM = 128
BRANCH_W = 512
N_BRANCH = 4
HY_W = 512
HY_ORDER = 2
HY_BANDS = 16
HY_EMB = 1 + 2 * HY_BANDS
HY_FAST_DECAY = 0.3
HY_SLOW_DECAY = 1.5
HY_TARGET = 1e-2
GQA_HEADS = 4
GQA_KV = 2
NA_HEADS = 4
NA_WIN_H = 8
NA_WIN_W = 16
DF_HEADS = 4
DF_QK_DIM = 64
N_EXPERTS = 16
CAPACITY_FACTOR = 2
ROPE_THETA = 10000.0
NORM_EPS = 1e-6
ATT_SCALE = HEAD_DIM ** -0.5
DF_SCALE = DF_QK_DIM ** -0.5
NEG_INF = -1e30

LANE = 128
ROW_TILE = 256
NA_GROUP_ROWS = 4
NA_WIN_ROWS = NA_GROUP_ROWS + NA_WIN_H
DFT_HALF = 256
VMEM_LIMIT = 56 * 1024 * 1024

C_HY = 0
C_QB = 3 * HY_W
C_KB = C_QB + GQA_HEADS * HEAD_DIM
C_VB = C_KB + GQA_KV * HEAD_DIM
C_QC = C_VB + GQA_KV * HEAD_DIM
C_KC = C_QC + NA_HEADS * HEAD_DIM
C_VC = C_KC + NA_HEADS * HEAD_DIM
C_QD = C_VC + NA_HEADS * HEAD_DIM
C_KD = C_QD + DF_HEADS * 2 * DF_QK_DIM
C_VD = C_KD + DF_HEADS * 2 * DF_QK_DIM
C_GATES = C_VD + DF_HEADS * HEAD_DIM


def _cp(*sem):
    return pltpu.CompilerParams(dimension_semantics=sem, vmem_limit_bytes=VMEM_LIMIT)


def _dot(a, b):
    return jnp.dot(a, b, preferred_element_type=F32)


def _dot_nt(a, b):
    return lax.dot_general(a, b, (((1,), (1,)), ((), ())), preferred_element_type=F32)


def _dot_hi(a, b):
    return jnp.dot(a, b, preferred_element_type=F32, precision=lax.Precision.HIGHEST)


def _mod_kernel(c_ref, w_ref, b_ref, o_ref):
    c = c_ref[...]
    a = (c * jax.nn.sigmoid(c)).astype(BF16)
    o_ref[...] = _dot(a, w_ref[...].astype(BF16)) + b_ref[...]


def _modulation(cc, w, b):
    D, N = w.shape
    tn = 1024
    return pl.pallas_call(
        _mod_kernel,
        grid=(N // tn,),
        in_specs=[pl.BlockSpec((8, D), lambda j: (0, 0)),
                  pl.BlockSpec((D, tn), lambda j: (0, j)),
                  pl.BlockSpec((1, tn), lambda j: (0, j))],
        out_specs=pl.BlockSpec((8, tn), lambda j: (0, j)),
        out_shape=jax.ShapeDtypeStruct((8, N), F32),
        compiler_params=_cp("arbitrary"),
        name="modulation",
    )(cc, w, b.reshape(1, N))


def _prenorm_body(x, g_ref, sc_ref, sh_ref):
    y = x * lax.rsqrt(jnp.mean(x * x, axis=-1, keepdims=True) + NORM_EPS)
    return y * g_ref[...] * (1.0 + sc_ref[...]) + sh_ref[...]


def _prenorm_kernel(x_ref, c_ref, g_ref, sc_ref, sh_ref, o_ref, *, n_lat):
    i = pl.program_id(1)

    @pl.when(i < n_lat)
    def _():
        o_ref[...] = _prenorm_body(x_ref[...], g_ref, sc_ref, sh_ref).astype(BF16)

    @pl.when(i >= n_lat)
    def _():
        o_ref[...] = _prenorm_body(c_ref[...], g_ref, sc_ref, sh_ref).astype(BF16)


def _prenorm_router_kernel(x_ref, c_ref, g_ref, sc_ref, sh_ref, wr_ref, o_ref, a_ref, *, n_lat, n_exp):
    i = pl.program_id(1)

    def run(x):
        h = _prenorm_body(x, g_ref, sc_ref, sh_ref)
        o_ref[...] = h.astype(BF16)
        logits = _dot_hi(h, wr_ref[...])
        lane = lax.broadcasted_iota(jnp.int32, logits.shape, 1)
        logits = jnp.where(lane < n_exp, logits, NEG_INF)
        e = jnp.exp(logits - jnp.max(logits, axis=-1, keepdims=True))
        a_ref[...] = e / jnp.sum(e, axis=-1, keepdims=True)

    @pl.when(i < n_lat)
    def _():
        run(x_ref[...])

    @pl.when(i >= n_lat)
    def _():
        run(c_ref[...])


def _prenorm(x, ctx, gain, modt, q_scale, q_shift, n_ctx_tiles, w_router=None):
    B, T, D = x.shape
    n_lat = T // ROW_TILE
    nt = n_lat + n_ctx_tiles
    S = nt * ROW_TILE

    def mod_map(q):
        return lambda b, i: (jnp.where(i < n_lat, b, B), q, 0, 0)

    in_specs = [
        pl.BlockSpec((None, ROW_TILE, D), lambda b, i: (b, jnp.minimum(i, n_lat - 1), 0)),
        pl.BlockSpec((None, ROW_TILE, D), lambda b, i: (b, jnp.maximum(i - n_lat, 0), 0)),
        pl.BlockSpec((1, D), lambda b, i: (0, 0)),
        pl.BlockSpec((None, None, 1, D), mod_map(q_scale)),
        pl.BlockSpec((None, None, 1, D), mod_map(q_shift)),
    ]
    out_h = pl.BlockSpec((None, ROW_TILE, D), lambda b, i: (b, i, 0))
    shape_h = jax.ShapeDtypeStruct((B, S, D), BF16)
    args = [x, ctx, gain.reshape(1, D), modt, modt]
    if w_router is None:
        return pl.pallas_call(
            functools.partial(_prenorm_kernel, n_lat=n_lat),
            grid=(B, nt), in_specs=in_specs, out_specs=out_h, out_shape=shape_h,
            compiler_params=_cp("parallel", "arbitrary"), name="prenorm",
        )(*args)
    n_exp = w_router.shape[1]
    wr = jnp.pad(w_router, ((0, 0), (0, LANE - n_exp)))
    return pl.pallas_call(
        functools.partial(_prenorm_router_kernel, n_lat=n_lat, n_exp=n_exp),
        grid=(B, nt),
        in_specs=in_specs + [pl.BlockSpec((D, LANE), lambda b, i: (0, 0))],
        out_specs=[out_h, pl.BlockSpec((None, ROW_TILE, LANE), lambda b, i: (b, i, 0))],
        out_shape=[shape_h, jax.ShapeDtypeStruct((B, S, LANE), F32)],
        compiler_params=_cp("parallel", "arbitrary"), name="prenorm_router",
    )(*args, wr)


def _mm_kernel(a_ref, b_ref, o_ref, *, sigmoid):
    acc = _dot(a_ref[...], b_ref[...])
    if sigmoid:
        acc = jax.nn.sigmoid(acc)
    o_ref[...] = acc.astype(o_ref.dtype)


def _matmul(a, b, tm, tn, out_dtype, sigmoid=False, name="matmul"):
    M, K = a.shape
    N = b.shape[1]
    return pl.pallas_call(
        functools.partial(_mm_kernel, sigmoid=sigmoid),
        grid=(M // tm, N // tn),
        in_specs=[pl.BlockSpec((tm, K), lambda i, j: (i, 0)),
                  pl.BlockSpec((K, tn), lambda i, j: (0, j))],
        out_specs=pl.BlockSpec((tm, tn), lambda i, j: (i, j)),
        out_shape=jax.ShapeDtypeStruct((M, N), out_dtype),
        compiler_params=_cp("parallel", "arbitrary"), name=name,
    )(a, b)


def _qkpost_kernel(*refs, nh, seg, rope):
    if rope:
        x_ref, g_ref, cos_ref, sin_ref, o_ref = refs
    else:
        x_ref, g_ref, o_ref = refs
    lane = lax.broadcasted_iota(jnp.int32, (1, LANE), 1)
    for h in range(nh):
        x = x_ref[:, h * LANE:(h + 1) * LANE].astype(F32)
        sq = x * x
        if seg == LANE:
            ms = jnp.mean(sq, axis=-1, keepdims=True)
        else:
            lo = jnp.sum(jnp.where(lane < seg, sq, 0.0), axis=-1, keepdims=True)
            hi = jnp.sum(jnp.where(lane >= seg, sq, 0.0), axis=-1, keepdims=True)
            ms = jnp.where(lane < seg, lo, hi) * (1.0 / seg)
        y = x * lax.rsqrt(ms + NORM_EPS) * g_ref[...]
        if rope:
            q = seg // 4
            partner = jnp.where((lane % (seg // 2)) < q, pltpu.roll(y, LANE - q, 1), pltpu.roll(y, q, 1))
            y = y * cos_ref[...] + partner * sin_ref[...]
        o_ref[:, h * LANE:(h + 1) * LANE] = y.astype(o_ref.dtype)


def _qkpost(proj, col0, nh, gain, seg, tables):
    B, S, _ = proj.shape
    W = nh * LANE
    ts = S // 2
    g = jnp.tile(gain.astype(F32), LANE // seg).reshape(1, LANE)
    in_specs = [pl.BlockSpec((None, ts, W), lambda b, i: (b, i, col0 // W)),
                pl.BlockSpec((1, LANE), lambda b, i: (0, 0))]
    args = [proj, g]
    if tables is not None:
        in_specs += [pl.BlockSpec((ts, LANE), lambda b, i: (i, 0))] * 2
        args += list(tables)
    return pl.pallas_call(
        functools.partial(_qkpost_kernel, nh=nh, seg=seg, rope=tables is not None),
        grid=(B, 2), in_specs=in_specs,
        out_specs=pl.BlockSpec((None, ts, W), lambda b, i: (b, i, 0)),
        out_shape=jax.ShapeDtypeStruct((B, S, W), BF16),
        compiler_params=_cp("parallel", "arbitrary"), name="qkpost",
    )(*args)


def _rope_tables(T, S, seg):
    half = seg // 2
    nfreq = half // 2
    inv = ROPE_THETA ** (-jnp.arange(0, half, 2, dtype=F32) / half)
    pos = jnp.arange(T, dtype=jnp.int32)
    rows, cols = (pos // GRID_W).astype(F32), (pos % GRID_W).astype(F32)
    l = np.arange(LANE) % seg
    use_col = l >= half
    fidx = (l % half) % nfreq
    is_b = (l % half) >= nfreq
    ang = jnp.where(use_col[None, :], cols[:, None], rows[:, None]) * inv[fidx][None, :]
    cos, sin = jnp.cos(ang), jnp.sin(ang)
    sin = jnp.where(is_b[None, :], sin, -sin)
    cos = jnp.concatenate([cos, jnp.ones((S - T, LANE), F32)], axis=0)
    sin = jnp.concatenate([sin, jnp.zeros((S - T, LANE), F32)], axis=0)
    return cos, sin


def _softmax_pv(s, v):
    m = jnp.max(s, axis=-1, keepdims=True)
    p = jnp.exp(s - m)
    l = jnp.sum(p, axis=-1, keepdims=True)
    return _dot(p.astype(BF16), v) / l


def _gqa_kernel(q_ref, k_ref, v_ref, o_ref, *, T, R, scale):
    i = pl.program_id(2)

    def attend(k, v):
        for r in range(R):
            s = _dot_nt(q_ref[:, r * LANE:(r + 1) * LANE], k) * scale
            o_ref[:, r * LANE:(r + 1) * LANE] = _softmax_pv(s, v).astype(o_ref.dtype)

    @pl.when(i < T // ROW_TILE)
    def _():
        attend(k_ref[...], v_ref[...])

    @pl.when(i >= T // ROW_TILE)
    def _():
        attend(k_ref[T:, :], v_ref[T:, :])


def _gqa(q, k, v, v_col0, T, Tq, R, scale):
    B, S = k.shape[0], k.shape[1]
    G = k.shape[2] // LANE
    vb = v_col0 // LANE
    return pl.pallas_call(
        functools.partial(_gqa_kernel, T=T, R=R, scale=scale),
        grid=(B, G, Tq // ROW_TILE),
        in_specs=[pl.BlockSpec((None, ROW_TILE, R * LANE), lambda b, g, i: (b, i, g)),
                  pl.BlockSpec((None, S, LANE), lambda b, g, i: (b, 0, g)),
                  pl.BlockSpec((None, S, LANE), lambda b, g, i: (b, 0, vb + g))],
        out_specs=pl.BlockSpec((None, ROW_TILE, R * LANE), lambda b, g, i: (b, i, g)),
        out_shape=jax.ShapeDtypeStruct((B, Tq, G * R * LANE), BF16),
        compiler_params=_cp("parallel", "parallel", "arbitrary"), name="gqa",
    )(q, k, v)


def _diff_kernel(lam_ref, q_ref, k_ref, v_ref, g_ref, o_ref, *, T, scale, out_scale):
    i = pl.program_id(2)
    lam = lam_ref[0]

    def attend(k, v):
        q = q_ref[...]
        lane = lax.broadcasted_iota(jnp.int32, q.shape, 1)
        zero = jnp.zeros_like(q)
        o1 = _softmax_pv(_dot_nt(jnp.where(lane < DF_QK_DIM, q, zero), k) * scale, v)
        o2 = _softmax_pv(_dot_nt(jnp.where(lane >= DF_QK_DIM, q, zero), k) * scale, v)
        o = o1 - lam * o2
        y = o * lax.rsqrt(jnp.mean(o * o, axis=-1, keepdims=True) + NORM_EPS) * g_ref[...]
        o_ref[...] = (y * out_scale).astype(o_ref.dtype)

    @pl.when(i < T // ROW_TILE)
    def _():
        attend(k_ref[...], v_ref[...])

    @pl.when(i >= T // ROW_TILE)
    def _():
        attend(k_ref[T:, :], v_ref[T:, :])


def _diff_attn(lam, q, k, v, v_col0, subln, T, Tq, out_scale):
    B, S = k.shape[0], k.shape[1]
    H = k.shape[2] // LANE
    vb = v_col0 // LANE
    return pl.pallas_call(
        functools.partial(_diff_kernel, T=T, scale=DF_SCALE, out_scale=out_scale),
        grid=(B, H, Tq // ROW_TILE),
        in_specs=[pl.BlockSpec(memory_space=pltpu.SMEM),
                  pl.BlockSpec((None, ROW_TILE, LANE), lambda b, h, i: (b, i, h)),
                  pl.BlockSpec((None, S, LANE), lambda b, h, i: (b, 0, h)),
                  pl.BlockSpec((None, S, LANE), lambda b, h, i: (b, 0, vb + h)),
                  pl.BlockSpec((1, LANE), lambda b, h, i: (0, 0))],
        out_specs=pl.BlockSpec((None, ROW_TILE, LANE), lambda b, h, i: (b, i, h)),
        out_shape=jax.ShapeDtypeStruct((B, Tq, H * LANE), BF16),
        compiler_params=_cp("parallel", "parallel", "arbitrary"), name="diff_attn",
    )(lam.reshape(1).astype(F32), q, k, v, subln.astype(F32).reshape(1, LANE))


def _na_kernel(q_ref, k_ref, v_ref, bias_ref, o_ref, *, T, scale):
    i = pl.program_id(2)
    n_groups = T // ROW_TILE
    grid_rows = T // GRID_W
    q = q_ref[...]
    kc, vc = k_ref[T:, :], v_ref[T:, :]
    s_c = _dot_nt(q, kc) * scale

    @pl.when(i < n_groups)
    def _():
        row0 = jnp.clip(i * NA_GROUP_ROWS - NA_WIN_H // 2, 0, grid_rows - NA_WIN_ROWS)
        start = pl.multiple_of(row0 * GRID_W, ROW_TILE)
        kw = k_ref[pl.ds(start, NA_WIN_ROWS * GRID_W), :]
        vw = v_ref[pl.ds(start, NA_WIN_ROWS * GRID_W), :]
        s_n = _dot_nt(q, kw) * scale + bias_ref[...]
        m = jnp.maximum(jnp.max(s_n, axis=-1, keepdims=True), jnp.max(s_c, axis=-1, keepdims=True))
        p_n, p_c = jnp.exp(s_n - m), jnp.exp(s_c - m)
        l = jnp.sum(p_n, axis=-1, keepdims=True) + jnp.sum(p_c, axis=-1, keepdims=True)
        o = (_dot(p_n.astype(BF16), vw) + _dot(p_c.astype(BF16), vc)) / l
        o_ref[...] = o.astype(o_ref.dtype)

    @pl.when(i >= n_groups)
    def _():
        o_ref[...] = _softmax_pv(s_c, vc).astype(o_ref.dtype)


def _na_bias(rpb, T):
    rows = T // GRID_W
    n_groups = rows // NA_GROUP_ROWS
    out = []
    for grp in (0, 1, n_groups - 1):
        i = np.arange(NA_GROUP_ROWS)[:, None, None, None]
        c = np.arange(GRID_W)[None, :, None, None]
        kk = np.arange(NA_WIN_ROWS)[None, None, :, None]
        kc = np.arange(GRID_W)[None, None, None, :]
        r = grp * NA_GROUP_ROWS + i
        rs = np.clip(r - NA_WIN_H // 2, 0, rows - NA_WIN_H)
        row0 = np.clip(grp * NA_GROUP_ROWS - NA_WIN_H // 2, 0, rows - NA_WIN_ROWS)
        key_row = row0 + kk
        cs = np.clip(c - NA_WIN_W // 2, 0, GRID_W - NA_WIN_W)
        valid = (key_row >= rs) & (key_row < rs + NA_WIN_H) & (kc >= cs) & (kc < cs + NA_WIN_W)
        roff = np.clip(key_row - r + NA_WIN_H - 1, 0, 2 * NA_WIN_H - 2)
        coff = np.clip(kc - c + NA_WIN_W - 1, 0, 2 * NA_WIN_W - 2)
        shape = (NA_GROUP_ROWS, GRID_W, NA_WIN_ROWS, GRID_W)
        roff, coff, valid = (np.broadcast_to(a, shape).reshape(ROW_TILE, -1) for a in (roff, coff, valid))
        out.append(jnp.where(valid[None], rpb.astype(F32)[:, roff, coff], NEG_INF))
    return jnp.stack(out)


def _na(q, k, v, v_col0, bias, T, Tq):
    B, S = k.shape[0], k.shape[1]
    H = k.shape[2] // LANE
    vb = v_col0 // LANE
    n_groups = T // ROW_TILE
    nk = NA_WIN_ROWS * GRID_W

    def bias_map(b, h, i):
        return (jnp.where(i == 0, 0, jnp.where(i >= n_groups - 1, 2, 1)), h, 0, 0)

    return pl.pallas_call(
        functools.partial(_na_kernel, T=T, scale=ATT_SCALE),
        grid=(B, H, Tq // ROW_TILE),
        in_specs=[pl.BlockSpec((None, ROW_TILE, LANE), lambda b, h, i: (b, i, h)),
                  pl.BlockSpec((None, S, LANE), lambda b, h, i: (b, 0, h)),
                  pl.BlockSpec((None, S, LANE), lambda b, h, i: (b, 0, vb + h)),
                  pl.BlockSpec((None, None, ROW_TILE, nk), bias_map)],
        out_specs=pl.BlockSpec((None, ROW_TILE, LANE), lambda b, h, i: (b, i, h)),
        out_shape=jax.ShapeDtypeStruct((B, Tq, H * LANE), BF16),
        compiler_params=_cp("parallel", "parallel", "arbitrary"), name="na_attn",
    )(q, k, v, bias)


def _shortconv_kernel(*refs, T, with_ctx):
    u_refs, w_refs, o_refs = refs[0:3], refs[3:6], refs[6:]
    S = u_refs[0].shape[0]
    row = lax.broadcasted_iota(jnp.int32, (S, 1), 0)
    first = (row == 0) | (row == T)
    last = (row == T - 1) | (row == S - 1)
    for n in range(3):
        u = u_refs[n][...].astype(F32)
        w = w_refs[n][...]
        prev = jnp.where(first, 0.0, pltpu.roll(u, 1, 0))
        nxt = jnp.where(last, 0.0, pltpu.roll(u, S - 1, 0))
        y = (prev * w[0:1] + u * w[1:2] + nxt * w[2:3]).astype(BF16)
        o_refs[n][...] = y[:T]
        if with_ctx:
            o_refs[3 + n][...] = y[T:]


def _shortconv(proj, w, T, with_ctx):
    B, S, _ = proj.shape
    Lc = S - T
    nct = HY_W // LANE
    in_specs = [pl.BlockSpec((None, S, LANE), functools.partial(lambda b, c, n: (b, 0, n * nct + c), n=n))
                for n in range(3)]
    in_specs += [pl.BlockSpec((3, LANE), functools.partial(lambda b, c, n: (0, n * nct + c), n=n))
                 for n in range(3)]
    out_specs = [pl.BlockSpec((T, LANE), lambda b, c: (0, b * nct + c))] * 3
    out_shape = [jax.ShapeDtypeStruct((T, B * HY_W), BF16)] * 3
    if with_ctx:
        out_specs += [pl.BlockSpec((Lc, LANE), lambda b, c: (0, b * nct + c))] * 3
        out_shape += [jax.ShapeDtypeStruct((Lc, B * HY_W), BF16)] * 3
    return pl.pallas_call(
        functools.partial(_shortconv_kernel, T=T, with_ctx=with_ctx),
        grid=(B, nct), in_specs=in_specs, out_specs=out_specs, out_shape=out_shape,
        compiler_params=_cp("parallel", "arbitrary"), name="hy_shortconv",
    )(proj, proj, proj, w, w, w)


def _filter_kernel(z_ref, f1_ref, b1_ref, fr_ref, f2_ref, b2_ref, f3f_ref, f3b_ref, dl_ref, hs_ref, hd_ref):
    z = z_ref[...]
    fr = fr_ref[...]
    hid = jnp.sin(fr * (_dot_hi(z, f1_ref[...]) + b1_ref[...]))
    hid = jnp.sin(fr * (_dot_hi(hid, f2_ref[...]) + b2_ref[...]))
    decay = jnp.exp(-z[:, 0:1] * dl_ref[...])
    fw = _dot_hi(hid, f3f_ref[...]) * decay
    bw = _dot_hi(hid, f3b_ref[...]) * decay
    row = lax.broadcasted_iota(jnp.int32, bw.shape, 0)
    bw = jnp.where(row == 0, 0.0, bw)
    inv = 1.0 / (jnp.sum(jnp.abs(fw), axis=0, keepdims=True) + jnp.sum(jnp.abs(bw), axis=0, keepdims=True))
    hs_ref[...] = ((fw + bw) * inv).astype(BF16)
    hd_ref[...] = ((fw - bw) * inv).astype(BF16)


def _hyena_filter_sums(L, f1, b1, freq, f2, b2, f3):
    t = jnp.linspace(0.0, 1.0, L, dtype=F32)[:, None]
    w = (2.0 * math.pi / L) * jnp.arange(L, dtype=F32)[:, None]
    bands = jnp.linspace(1e-4, HY_BANDS - 1, HY_BANDS, dtype=F32)[None, :]
    z = jnp.concatenate([t, jnp.cos(bands * w), -jnp.sin(bands * w)], axis=-1)
    z = jnp.pad(z, ((0, 0), (0, LANE - HY_EMB)))
    f1p = jnp.pad(f1.astype(F32), ((0, LANE - HY_EMB), (0, 0)))
    ffn = f1.shape[1]
    deltas = jnp.abs(jnp.linspace(math.log(HY_TARGET) / HY_SLOW_DECAY, math.log(HY_TARGET) / HY_FAST_DECAY,
                                  HY_W, dtype=F32))
    NC = HY_ORDER * HY_W
    dl = jnp.tile(deltas, HY_ORDER).reshape(1, NC)
    tn = 256
    small = lambda shape: pl.BlockSpec(shape, lambda j: (0, 0))
    return pl.pallas_call(
        _filter_kernel,
        grid=(NC // tn,),
        in_specs=[small((L, LANE)), small((LANE, ffn)), small((1, ffn)), small((1, ffn)),
                  small((ffn, ffn)), small((1, ffn)),
                  pl.BlockSpec((ffn, tn), lambda j: (0, j)),
                  pl.BlockSpec((ffn, tn), lambda j: (0, NC // tn + j)),
                  pl.BlockSpec((1, tn), lambda j: (0, j))],
        out_specs=[pl.BlockSpec((L, tn), lambda j: (0, j))] * 2,
        out_shape=[jax.ShapeDtypeStruct((L, NC), BF16)] * 2,
        compiler_params=_cp("arbitrary"), name="hy_filter",
    )(z, f1p, b1.astype(F32).reshape(1, ffn), freq.astype(F32).reshape(1, ffn), f2.astype(F32),
      b2.astype(F32).reshape(1, ffn), f3.astype(F32), f3.astype(F32), dl)


def _dft_tables(L):
    N = 2 * L
    r = np.arange(N)
    k = (r // (2 * DFT_HALF)) * DFT_HALF + r % DFT_HALF
    is_im = (r // DFT_HALF) % 2 == 1
    nyq = is_im & (k == 0)
    n = jnp.arange(L, dtype=jnp.int32)
    kj = jnp.asarray(k, jnp.int32)
    sign = (1 - 2 * (n % 2)).astype(F32)

    def table(kn, im, ny, sg):
        ang = (kn % N).astype(F32) * (2.0 * math.pi / N)
        return jnp.where(ny, sg, jnp.where(im, -jnp.sin(ang), jnp.cos(ang))).astype(BF16)

    a = table(kj[:, None] * n[None, :], is_im[:, None], nyq[:, None], sign[None, :])
    at = table(n[:, None] * kj[None, :], is_im[None, :], nyq[None, :], sign[:, None])
    return a, at


def _kf_kernel(a_ref, hs_ref, hd_ref, o_ref, *, n_fft):
    i = pl.program_id(0)
    H = DFT_HALF
    re = _dot(a_ref[:H, :], hs_ref[...])
    im = _dot(a_ref[H:, :], hd_ref[...])
    o_ref[:H, :] = re * (2.0 / n_fft)
    o_ref[H:, :] = im * (2.0 / n_fft)

    @pl.when(i == 0)
    def _():
        ny = _dot(a_ref[H:H + 16, :], hs_ref[...])
        o_ref[0:1, :] = re[0:1] * (1.0 / n_fft)
        o_ref[H:H + 1, :] = ny[0:1] * (1.0 / n_fft)


def _filter_spectrum(a, hs, hd):
    N, L = a.shape
    NC = hs.shape[1]
    tm, tn = 2 * DFT_HALF, 512
    return pl.pallas_call(
        functools.partial(_kf_kernel, n_fft=N),
        grid=(N // tm, NC // tn),
        in_specs=[pl.BlockSpec((tm, L), lambda i, j: (i, 0)),
                  pl.BlockSpec((L, tn), lambda i, j: (0, j)),
                  pl.BlockSpec((L, tn), lambda i, j: (0, j))],
        out_specs=pl.BlockSpec((tm, tn), lambda i, j: (i, j)),
        out_shape=jax.ShapeDtypeStruct((N, NC), F32),
        compiler_params=_cp("parallel", "arbitrary"), name="hy_filter_spectrum",
    )(a, hs, hd)


def _fwd_kernel(a_ref, z_ref, kf_ref, p_ref):
    i = pl.program_id(0)
    H = DFT_HALF
    acc = _dot(a_ref[...], z_ref[...])
    zr, zi = acc[:H], acc[H:]
    kr, ki = kf_ref[:H, :], kf_ref[H:, :]
    row = lax.broadcasted_iota(jnp.int32, zr.shape, 0)
    real_pair = (row == 0) & (i == 0)
    p_ref[:H, :] = jnp.where(real_pair, zr * kr, zr * kr - zi * ki).astype(BF16)
    p_ref[H:, :] = jnp.where(real_pair, zi * ki, zr * ki + zi * kr).astype(BF16)


def _dft_multiply(a, z, kf, order):
    N, L = a.shape
    NB = z.shape[1] // HY_W
    tm = 2 * DFT_HALF
    return pl.pallas_call(
        _fwd_kernel,
        grid=(N // tm, NB),
        in_specs=[pl.BlockSpec((tm, L), lambda i, j: (i, 0)),
                  pl.BlockSpec((L, HY_W), lambda i, j: (0, j)),
                  pl.BlockSpec((tm, HY_W), lambda i, j: (i, order))],
        out_specs=pl.BlockSpec((tm, HY_W), lambda i, j: (i, j)),
        out_shape=jax.ShapeDtypeStruct((N, NB * HY_W), BF16),
        compiler_params=_cp("parallel", "arbitrary"), name="hy_dft_multiply",
    )(a, z, kf)


def _inv_kernel(at_ref, p_ref, x_ref, z_ref, b_ref, o_ref):
    y = _dot(at_ref[...], p_ref[...])
    z = z_ref[...].astype(F32)
    o_ref[...] = (x_ref[...].astype(F32) * (y + b_ref[...] * z)).astype(o_ref.dtype)


def _idft_gate(at, p, xo, z, bias, batch_major):
    L, N = at.shape
    NB = z.shape[1] // HY_W
    tm = min(L, 512)
    if batch_major:
        out_spec = pl.BlockSpec((None, tm, HY_W), lambda i, j: (j, i, 0))
        out_shape = jax.ShapeDtypeStruct((NB, L, HY_W), BF16)
    else:
        out_spec = pl.BlockSpec((tm, HY_W), lambda i, j: (i, j))
        out_shape = jax.ShapeDtypeStruct((L, NB * HY_W), BF16)
    return pl.pallas_call(
        _inv_kernel,
        grid=(L // tm, NB),
        in_specs=[pl.BlockSpec((tm, N), lambda i, j: (i, 0)),
                  pl.BlockSpec((N, HY_W), lambda i, j: (0, j)),
                  pl.BlockSpec((tm, HY_W), lambda i, j: (i, j)),
                  pl.BlockSpec((tm, HY_W), lambda i, j: (i, j)),
                  pl.BlockSpec((1, HY_W), lambda i, j: (0, 0))],
        out_specs=out_spec, out_shape=out_shape,
        compiler_params=_cp("parallel", "arbitrary"), name="hy_idft_gate",
    )(at, p, xo, z, bias.astype(F32).reshape(1, HY_W))


def _hyena(vxx, tables, filt, hy_bias):
    a, at = tables
    kf = _filter_spectrum(a, *filt)
    z = vxx[0]
    for o in range(HY_ORDER):
        p = _dft_multiply(a, z, kf, o)
        z = _idft_gate(at, p, vxx[1 + o], z, hy_bias[o], batch_major=(o == HY_ORDER - 1))
    return z


def _merge_kernel(ya_ref, yb_ref, yc_ref, yd_ref, g0_ref, g1_ref, g2_ref, g3_ref, wb_ref, wo_ref, x_ref, gate_ref,
                  o_ref):
    ys = (ya_ref, yb_ref, yc_ref, yd_ref)
    gs = (g0_ref, g1_ref, g2_ref, g3_ref)
    acc = _dot(ys[0][...], wb_ref[0]) * gs[0][...].astype(F32)
    for n in range(1, N_BRANCH):
        acc += _dot(ys[n][...], wb_ref[n]) * gs[n][...].astype(F32)
    o_ref[...] = x_ref[...] + gate_ref[...] * _dot(acc.astype(BF16), wo_ref[...])


def _merge(ys, gates, wb, wo, x, modt, q_gate, tm, row_tile0, mod_row):
    B, R, D = x.shape
    row = lambda b, i: (b, row_tile0 + i, 0)
    in_specs = [pl.BlockSpec((None, tm, BRANCH_W), row)] * N_BRANCH
    in_specs += [pl.BlockSpec((None, tm, D), functools.partial(lambda b, i, n: (b, row_tile0 + i, n), n=n))
                 for n in range(N_BRANCH)]
    in_specs += [pl.BlockSpec((N_BRANCH, BRANCH_W, D), lambda b, i: (0, 0, 0), pipeline_mode=pl.Buffered(1)),
                 pl.BlockSpec((D, D), lambda b, i: (0, 0), pipeline_mode=pl.Buffered(1)),
                 pl.BlockSpec((None, tm, D), lambda b, i: (b, i, 0)),
                 pl.BlockSpec((None, None, 1, D), lambda b, i: (mod_row(b), q_gate, 0, 0))]
    return pl.pallas_call(
        _merge_kernel,
        grid=(B, R // tm), in_specs=in_specs,
        out_specs=pl.BlockSpec((None, tm, D), lambda b, i: (b, i, 0)),
        out_shape=jax.ShapeDtypeStruct((B, R, D), F32),
        compiler_params=_cp("parallel", "arbitrary"), name="merge",
    )(*ys, gates, gates, gates, gates, wb, wo, x, modt)


def _expert_up_kernel(x_ref, w1_ref, w3_ref, o_ref):
    x = x_ref[...]
    a = _dot(x, w1_ref[...].astype(BF16))
    b = _dot(x, w3_ref[...].astype(BF16))
    o_ref[...] = (a * jax.nn.sigmoid(a) * b).astype(o_ref.dtype)


def _expert_down_kernel(h_ref, g_ref, w2_ref, o_ref):
    o_ref[...] = _dot(h_ref[...], w2_ref[...].astype(BF16)) * g_ref[...]


def _experts(xg, gate, w1, w3, w2):
    E, M, D = xg.shape
    F = w1.shape[2]
    tf = 256
    hmid = pl.pallas_call(
        _expert_up_kernel,
        grid=(E, F // tf),
        in_specs=[pl.BlockSpec((None, M, D), lambda e, f: (e, 0, 0)),
                  pl.BlockSpec((None, D, tf), lambda e, f: (e, 0, f)),
                  pl.BlockSpec((None, D, tf), lambda e, f: (e, 0, f))],
        out_specs=pl.BlockSpec((None, M, tf), lambda e, f: (e, 0, f)),
        out_shape=jax.ShapeDtypeStruct((E, M, F), BF16),
        compiler_params=_cp("parallel", "arbitrary"), name="expert_up",
    )(xg, w1, w3)
    return pl.pallas_call(
        _expert_down_kernel,
        grid=(E, D // tf),
        in_specs=[pl.BlockSpec((None, M, F), lambda e, n: (e, 0, 0)),
                  pl.BlockSpec((None, M, 1), lambda e, n: (e, 0, 0)),
                  pl.BlockSpec((None, F, tf), lambda e, n: (e, 0, n))],
        out_specs=pl.BlockSpec((None, M, tf), lambda e, n: (e, 0, n)),
        out_shape=jax.ShapeDtypeStruct((E, M, D), F32),
        compiler_params=_cp("parallel", "arbitrary"), name="expert_down",
    )(hmid, gate, w2)


def _moe(h2, aff, T, Lc, w1, w3, w2):
    B, S, D = h2.shape
    E = w1.shape[0]
    base = (jnp.arange(B, dtype=jnp.int32) * S)[:, None, None]

    def route(a, off):
        n = a.shape[1]
        gate, idx = lax.top_k(a.transpose(0, 2, 1), CAPACITY_FACTOR * n // E)
        rows = (idx + base + off).transpose(1, 0, 2).reshape(E, -1)
        return gate.transpose(1, 0, 2).reshape(E, -1), rows

    gate, rows = route(aff[:, :T, :E], 0)
    if Lc:
        gate_c, rows_c = route(aff[:, T:, :E], T)
        gate, rows = jnp.concatenate([gate, gate_c], axis=1), jnp.concatenate([rows, rows_c], axis=1)
    xg = h2.reshape(B * S, D)[rows]
    y = _experts(xg, gate[..., None], w1, w3, w2)
    return jnp.zeros((B * S, D), F32).at[rows.reshape(-1)].add(y.reshape(-1, D))


def kernel(x, c, ctx, c_ctx, w_mod, b_mod, norm1, norm2, w_in, hy_short, hy_f1, hy_b1, hy_freq, hy_f2, hy_b2, hy_f3, hy_bias, qn_b, kn_b, qn_c, kn_c, rpb_c, qn_d, kn_d, lam_q1, lam_k1, lam_q2, lam_k2, subln_d, w_branch, w_out, w_router, w_e1, w_e3, w_e2):
    B, T, D = x.shape
    Lc = ctx.shape[1]
    S = T + Lc
    depth = w_mod.shape[0]
    assert T % ROW_TILE == 0 and Lc == ROW_TILE and T // GRID_W >= NA_WIN_ROWS

    cc = jnp.concatenate([c, c_ctx[None, :], jnp.zeros((8 - B - 1, D), F32)], axis=0)
    rope_h = _rope_tables(T, S, HEAD_DIM)
    rope_d = _rope_tables(T, S, DF_QK_DIM)
    dft_lat = _dft_tables(T)
    dft_ctx = _dft_tables(Lc)

    for l in range(depth):
        last = l == depth - 1
        n_ctx = 0 if last else 1
        Tq = T + n_ctx * ROW_TILE
        lam_init = 0.8 - 0.6 * math.exp(-0.3 * l)
        lam = (jnp.exp(jnp.sum(lam_q1[l].astype(F32) * lam_k1[l].astype(F32)))
               - jnp.exp(jnp.sum(lam_q2[l].astype(F32) * lam_k2[l].astype(F32))) + lam_init)
        modt = _modulation(cc, w_mod[l], b_mod[l]).reshape(8, 6, 1, D)

        h = _prenorm(x, ctx, norm1[l], modt, 1, 0, 1)
        w_in_l = w_in[l].astype(BF16)
        h2d = h.reshape(B * S, D)
        proj = _matmul(h2d, w_in_l[:, :C_GATES], 1024, 512, BF16, name="in_proj").reshape(B, S, C_GATES)
        gates = _matmul(h2d, w_in_l[:, C_GATES:], 1024, 512, BF16, sigmoid=True,
                        name="in_proj_gates").reshape(B, S, N_BRANCH * D)

        q_b = _qkpost(proj, C_QB, GQA_HEADS, qn_b[l], HEAD_DIM, rope_h)
        k_b = _qkpost(proj, C_KB, GQA_KV, kn_b[l], HEAD_DIM, rope_h)
        q_c = _qkpost(proj, C_QC, NA_HEADS, qn_c[l], HEAD_DIM, None)
        k_c = _qkpost(proj, C_KC, NA_HEADS, kn_c[l], HEAD_DIM, None)
        q_d = _qkpost(proj, C_QD, DF_HEADS, qn_d[l], DF_QK_DIM, rope_d)
        k_d = _qkpost(proj, C_KD, DF_HEADS, kn_d[l], DF_QK_DIM, rope_d)

        y_b = _gqa(q_b, k_b, proj, C_VB, T, Tq, GQA_HEADS // GQA_KV, ATT_SCALE)
        y_c = _na(q_c, k_c, proj, C_VC, _na_bias(rpb_c[l], T), T, Tq)
        y_d = _diff_attn(lam, q_d, k_d, proj, C_VD, subln_d[l], T, Tq, 1.0 - lam_init)

        conv = _shortconv(proj, hy_short[l].astype(F32), T, with_ctx=not last)
        filt = _hyena_filter_sums(T, hy_f1[l], hy_b1[l], hy_freq[l], hy_f2[l], hy_b2[l], hy_f3[l])
        y_a = _hyena(conv[:3], dft_lat, filt, hy_bias[l])
        if not last:
            filt_c = _hyena_filter_sums(Lc, hy_f1[l], hy_b1[l], hy_freq[l], hy_f2[l], hy_b2[l], hy_f3[l])
            y_a = jnp.concatenate([y_a, _hyena(conv[3:], dft_ctx, filt_c, hy_bias[l])], axis=1)

        ys = (y_a, y_b, y_c, y_d)
        wb, wo = w_branch[l].astype(BF16), w_out[l].astype(BF16)
        x = _merge(ys, gates, wb, wo, x, modt, 2, ROW_TILE, 0, lambda b: b)
        if not last:
            ctx = _merge(ys, gates, wb, wo, ctx, modt, 2, ROW_TILE, T // ROW_TILE, lambda b: B)

        h2, aff = _prenorm(x, ctx, norm2[l], modt, 4, 3, n_ctx, w_router=w_router[l].astype(F32))
        moe = _moe(h2, aff, T, n_ctx * Lc, w_e1[l], w_e3[l], w_e2[l]).reshape(B, Tq, D)
        g2 = modt[:, 5, 0, :]
        x = x + g2[:B, None, :] * moe[:, :T]
        if not last:
            ctx = ctx + g2[B][None, None, :] * moe[:, T:]
    return x
```

```python
import functools
import math

import numpy as np
import jax
import jax.numpy as jnp
from jax import lax
from jax.experimental import pallas as pl
from jax.experimental.pallas import tpu as pltpu

F32 = jnp.float32
BF16 = jnp.bfloat16

GRID_W = 64
HEAD_DIM = 128
BRANCH_W = 512
N_BRANCH = 4
HY_W = 512
HY_ORDER = 2
HY_BANDS = 16
HY_EMB = 1 + 2 * HY_BANDS
HY_FAST_DECAY = 0.3
HY_SLOW_DECAY = 1.5
HY_TARGET = 1e-2
GQA_HEADS = 4
GQA_KV = 2
NA_HEADS = 4
NA_WIN_H = 8
NA_WIN_W = 16
DF_HEADS = 4
DF_QK_DIM = 64
N_EXPERTS = 16
CAPACITY_FACTOR = 2
ROPE_THETA = 10000.0
NORM_EPS = 1e-6
ATT_SCALE = HEAD_DIM ** -0.5
DF_SCALE = DF_QK_DIM ** -0.5
NEG_INF = -1e30
LOG2E = math.log2(math.e)

LANE = 128
ROW_TILE = 256
NA_GROUP_ROWS = 4
NA_WIN_ROWS = NA_GROUP_ROWS + NA_WIN_H
DFT_HALF = 256
VMEM_LIMIT = 56 * 1024 * 1024

C_HY = 0
C_QB = 3 * HY_W
C_KB = C_QB + GQA_HEADS * HEAD_DIM
C_VB = C_KB + GQA_KV * HEAD_DIM
C_QC = C_VB + GQA_KV * HEAD_DIM
C_KC = C_QC + NA_HEADS * HEAD_DIM
C_VC = C_KC + NA_HEADS * HEAD_DIM
C_QD = C_VC + NA_HEADS * HEAD_DIM
C_KD = C_QD + DF_HEADS * 2 * DF_QK_DIM
C_VD = C_KD + DF_HEADS * 2 * DF_QK_DIM
C_GATES = C_VD + DF_HEADS * HEAD_DIM


def _cp(*sem):
    return pltpu.CompilerParams(dimension_semantics=sem, vmem_limit_bytes=VMEM_LIMIT)


def _dot(a, b):
    return jnp.dot(a, b, preferred_element_type=F32)


def _dot_nt(a, b):
    return lax.dot_general(a, b, (((1,), (1,)), ((), ())), preferred_element_type=F32)


def _dot_hi(a, b):
    return jnp.dot(a, b, preferred_element_type=F32, precision=lax.Precision.HIGHEST)


def _mod_kernel(c_ref, w_ref, b_ref, o_ref):
    c = c_ref[...]
    a = (c * jax.nn.sigmoid(c)).astype(BF16)
    o_ref[...] = _dot(a, w_ref[...].astype(BF16)) + b_ref[...]


def _modulation(cc, w, b, l):
    depth, D, N = w.shape
    tn = math.gcd(N, 1024)
    return pl.pallas_call(
        _mod_kernel,
        grid=(N // tn,),
        in_specs=[pl.BlockSpec((8, D), lambda j: (0, 0)),
                  pl.BlockSpec((None, D, tn), lambda j: (l, 0, j)),
                  pl.BlockSpec((None, 1, tn), lambda j: (l, 0, j))],
        out_specs=pl.BlockSpec((8, tn), lambda j: (0, j)),
        out_shape=jax.ShapeDtypeStruct((8, N), F32),
        compiler_params=_cp("arbitrary"),
        name="modulation",
    )(cc, w, b.reshape(depth, 1, N))


def _prenorm_body(x, g_ref, sc_ref, sh_ref):
    y = x * lax.rsqrt(jnp.mean(x * x, axis=-1, keepdims=True) + NORM_EPS)
    return y * g_ref[...] * (1.0 + sc_ref[...]) + sh_ref[...]


def _prenorm_kernel(x_ref, c_ref, g_ref, sc_ref, sh_ref, o_ref, *, n_lat):
    i = pl.program_id(1)

    @pl.when(i < n_lat)
    def _():
        o_ref[...] = _prenorm_body(x_ref[...], g_ref, sc_ref, sh_ref).astype(BF16)

    @pl.when(i >= n_lat)
    def _():
        o_ref[...] = _prenorm_body(c_ref[...], g_ref, sc_ref, sh_ref).astype(BF16)


def _prenorm_router_kernel(x_ref, c_ref, g_ref, sc_ref, sh_ref, wr_ref, o_ref, a_ref, *, n_lat, n_exp):
    i = pl.program_id(1)

    def run(x):
        h = _prenorm_body(x, g_ref, sc_ref, sh_ref)
        o_ref[...] = h.astype(BF16)
        logits = _dot_hi(h, wr_ref[...])
        lane = lax.broadcasted_iota(jnp.int32, logits.shape, 1)
        logits = jnp.where(lane < n_exp, logits, NEG_INF)
        e = jnp.exp(logits - jnp.max(logits, axis=-1, keepdims=True))
        a_ref[...] = e / jnp.sum(e, axis=-1, keepdims=True)

    @pl.when(i < n_lat)
    def _():
        run(x_ref[...])

    @pl.when(i >= n_lat)
    def _():
        run(c_ref[...])


def _prenorm(x, ctx, gain, modt, q_scale, q_shift, n_ctx_tiles, w_router=None):
    B, T, D = x.shape
    n_lat = T // ROW_TILE
    nt = n_lat + n_ctx_tiles
    S = nt * ROW_TILE

    def mod_map(q):
        return lambda b, i: (jnp.where(i < n_lat, b, B), q, 0, 0)

    in_specs = [
        pl.BlockSpec((None, ROW_TILE, D), lambda b, i: (b, jnp.minimum(i, n_lat - 1), 0)),
        pl.BlockSpec((None, ROW_TILE, D), lambda b, i: (b, jnp.maximum(i - n_lat, 0), 0)),
        pl.BlockSpec((1, D), lambda b, i: (0, 0)),
        pl.BlockSpec((None, None, 1, D), mod_map(q_scale)),
        pl.BlockSpec((None, None, 1, D), mod_map(q_shift)),
    ]
    out_h = pl.BlockSpec((None, ROW_TILE, D), lambda b, i: (b, i, 0))
    shape_h = jax.ShapeDtypeStruct((B, S, D), BF16)
    args = [x, ctx, gain.reshape(1, D), modt, modt]
    if w_router is None:
        return pl.pallas_call(
            functools.partial(_prenorm_kernel, n_lat=n_lat),
            grid=(B, nt), in_specs=in_specs, out_specs=out_h, out_shape=shape_h,
            compiler_params=_cp("parallel", "arbitrary"), name="prenorm",
        )(*args)
    n_exp = w_router.shape[1]
    wr = jnp.pad(w_router, ((0, 0), (0, LANE - n_exp)))
    return pl.pallas_call(
        functools.partial(_prenorm_router_kernel, n_lat=n_lat, n_exp=n_exp),
        grid=(B, nt),
        in_specs=in_specs + [pl.BlockSpec((D, LANE), lambda b, i: (0, 0))],
        out_specs=[out_h, pl.BlockSpec((None, ROW_TILE, LANE), lambda b, i: (b, i, 0))],
        out_shape=[shape_h, jax.ShapeDtypeStruct((B, S, LANE), F32)],
        compiler_params=_cp("parallel", "arbitrary"), name="prenorm_router",
    )(*args, wr)


def _mm_kernel(a_ref, b_ref, o_ref, *, sigmoid):
    acc = _dot(a_ref[...], b_ref[...])
    if sigmoid:
        acc = jax.nn.sigmoid(acc)
    o_ref[...] = acc.astype(o_ref.dtype)


def _matmul(a, b, l, col0, N, out_dtype, sigmoid=False, name="matmul"):
    M, K = a.shape
    tm, tn = math.gcd(M, 1024), 512
    cb = col0 // tn
    return pl.pallas_call(
        functools.partial(_mm_kernel, sigmoid=sigmoid),
        grid=(M // tm, N // tn),
        in_specs=[pl.BlockSpec((tm, K), lambda i, j: (i, 0)),
                  pl.BlockSpec((None, K, tn), lambda i, j: (l, 0, cb + j))],
        out_specs=pl.BlockSpec((tm, tn), lambda i, j: (i, j)),
        out_shape=jax.ShapeDtypeStruct((M, N), out_dtype),
        compiler_params=_cp("parallel", "arbitrary"), name=name,
    )(a, b)


def _qkpost_kernel(*refs, nh, seg, rope):
    if rope:
        x_ref, g_ref, cos_ref, sin_ref, o_ref = refs
    else:
        x_ref, g_ref, o_ref = refs
    lane = lax.broadcasted_iota(jnp.int32, (1, LANE), 1)
    for h in range(nh):
        x = x_ref[:, h * LANE:(h + 1) * LANE].astype(F32)
        sq = x * x
        if seg == LANE:
            ms = jnp.mean(sq, axis=-1, keepdims=True)
        else:
            lo = jnp.sum(jnp.where(lane < seg, sq, 0.0), axis=-1, keepdims=True)
            hi = jnp.sum(jnp.where(lane >= seg, sq, 0.0), axis=-1, keepdims=True)
            ms = jnp.where(lane < seg, lo, hi) * (1.0 / seg)
        y = x * lax.rsqrt(ms + NORM_EPS) * g_ref[...]
        if rope:
            q = seg // 4
            partner = jnp.where((lane % (seg // 2)) < q, pltpu.roll(y, LANE - q, 1), pltpu.roll(y, q, 1))
            y = y * cos_ref[...] + partner * sin_ref[...]
        o_ref[:, h * LANE:(h + 1) * LANE] = y.astype(o_ref.dtype)


def _qkpost(proj, col0, nh, gain, seg, tables):
    B, S, _ = proj.shape
    W = nh * LANE
    ts = S // 2
    g = jnp.tile(gain.astype(F32), LANE // seg).reshape(1, LANE)
    in_specs = [pl.BlockSpec((None, ts, W), lambda b, i: (b, i, col0 // W)),
                pl.BlockSpec((1, LANE), lambda b, i: (0, 0))]
    args = [proj, g]
    if tables is not None:
        in_specs += [pl.BlockSpec((ts, LANE), lambda b, i: (i, 0))] * 2
        args += list(tables)
    return pl.pallas_call(
        functools.partial(_qkpost_kernel, nh=nh, seg=seg, rope=tables is not None),
        grid=(B, 2), in_specs=in_specs,
        out_specs=pl.BlockSpec((None, ts, W), lambda b, i: (b, i, 0)),
        out_shape=jax.ShapeDtypeStruct((B, S, W), BF16),
        compiler_params=_cp("parallel", "arbitrary"), name="qkpost",
    )(*args)


def _rope_tables(T, S, seg):
    half = seg // 2
    nfreq = half // 2
    inv = ROPE_THETA ** (-jnp.arange(0, half, 2, dtype=F32) / half)
    pos = jnp.arange(T, dtype=jnp.int32)
    rows, cols = (pos // GRID_W).astype(F32), (pos % GRID_W).astype(F32)
    l = np.arange(LANE) % seg
    use_col = l >= half
    fidx = (l % half) % nfreq
    is_b = (l % half) >= nfreq
    ang = jnp.where(use_col[None, :], cols[:, None], rows[:, None]) * inv[fidx][None, :]
    cos, sin = jnp.cos(ang), jnp.sin(ang)
    sin = jnp.where(is_b[None, :], sin, -sin)
    cos = jnp.concatenate([cos, jnp.ones((S - T, LANE), F32)], axis=0)
    sin = jnp.concatenate([sin, jnp.zeros((S - T, LANE), F32)], axis=0)
    return cos, sin


def _row_sum(p):
    return _dot(p, jnp.ones((p.shape[1], LANE), p.dtype))[:, :1]


def _softmax_pv(s, v, scale):
    c = scale * LOG2E
    m = jnp.max(s, axis=-1, keepdims=True)
    p = jnp.exp2(s * c - m * c).astype(BF16)
    return _dot(p, v) / _row_sum(p)


def _gqa_kernel(q_ref, k_ref, v_ref, o_ref, *, T, R, scale):
    i = pl.program_id(2)

    def attend(k, v):
        for r in range(R):
            s = _dot_nt(q_ref[:, r * LANE:(r + 1) * LANE], k)
            o_ref[:, r * LANE:(r + 1) * LANE] = _softmax_pv(s, v, scale).astype(o_ref.dtype)

    @pl.when(i < T // ROW_TILE)
    def _():
        attend(k_ref[...], v_ref[...])

    @pl.when(i >= T // ROW_TILE)
    def _():
        attend(k_ref[T:, :], v_ref[T:, :])


def _gqa(q, k, v, v_col0, T, Tq, R, scale):
    B, S = k.shape[0], k.shape[1]
    G = k.shape[2] // LANE
    vb = v_col0 // LANE
    return pl.pallas_call(
        functools.partial(_gqa_kernel, T=T, R=R, scale=scale),
        grid=(B, G, Tq // ROW_TILE),
        in_specs=[pl.BlockSpec((None, ROW_TILE, R * LANE), lambda b, g, i: (b, i, g)),
                  pl.BlockSpec((None, S, LANE), lambda b, g, i: (b, 0, g)),
                  pl.BlockSpec((None, S, LANE), lambda b, g, i: (b, 0, vb + g))],
        out_specs=pl.BlockSpec((None, ROW_TILE, R * LANE), lambda b, g, i: (b, i, g)),
        out_shape=jax.ShapeDtypeStruct((B, Tq, G * R * LANE), BF16),
        compiler_params=_cp("parallel", "parallel", "arbitrary"), name="gqa",
    )(q, k, v)


def _diff_kernel(lam_ref, q_ref, k_ref, v_ref, g_ref, o_ref, *, T, scale, out_scale):
    i = pl.program_id(2)
    lam = lam_ref[0]

    def attend(k, v):
        q = q_ref[...]
        lane = lax.broadcasted_iota(jnp.int32, q.shape, 1)
        zero = jnp.zeros_like(q)
        o1 = _softmax_pv(_dot_nt(jnp.where(lane < DF_QK_DIM, q, zero), k), v, scale)
        o2 = _softmax_pv(_dot_nt(jnp.where(lane >= DF_QK_DIM, q, zero), k), v, scale)
        o = o1 - lam * o2
        y = o * lax.rsqrt(jnp.mean(o * o, axis=-1, keepdims=True) + NORM_EPS) * g_ref[...]
        o_ref[...] = (y * out_scale).astype(o_ref.dtype)

    @pl.when(i < T // ROW_TILE)
    def _():
        attend(k_ref[...], v_ref[...])

    @pl.when(i >= T // ROW_TILE)
    def _():
        attend(k_ref[T:, :], v_ref[T:, :])


def _diff_attn(lam, q, k, v, v_col0, subln, T, Tq, out_scale):
    B, S = k.shape[0], k.shape[1]
    H = k.shape[2] // LANE
    vb = v_col0 // LANE
    return pl.pallas_call(
        functools.partial(_diff_kernel, T=T, scale=DF_SCALE, out_scale=out_scale),
        grid=(B, H, Tq // ROW_TILE),
        in_specs=[pl.BlockSpec(memory_space=pltpu.SMEM),
                  pl.BlockSpec((None, ROW_TILE, LANE), lambda b, h, i: (b, i, h)),
                  pl.BlockSpec((None, S, LANE), lambda b, h, i: (b, 0, h)),
                  pl.BlockSpec((None, S, LANE), lambda b, h, i: (b, 0, vb + h)),
                  pl.BlockSpec((1, LANE), lambda b, h, i: (0, 0))],
        out_specs=pl.BlockSpec((None, ROW_TILE, LANE), lambda b, h, i: (b, i, h)),
        out_shape=jax.ShapeDtypeStruct((B, Tq, H * LANE), BF16),
        compiler_params=_cp("parallel", "parallel", "arbitrary"), name="diff_attn",
    )(lam.reshape(1).astype(F32), q, k, v, subln.astype(F32).reshape(1, LANE))


def _na_kernel(q_ref, k_ref, v_ref, bias_ref, o_ref, *, T, scale):
    i = pl.program_id(2)
    n_groups = T // ROW_TILE
    grid_rows = T // GRID_W
    q = q_ref[...]
    kc, vc = k_ref[T:, :], v_ref[T:, :]
    s_c = _dot_nt(q, kc)

    @pl.when(i < n_groups)
    def _():
        row0 = jnp.clip(i * NA_GROUP_ROWS - NA_WIN_H // 2, 0, grid_rows - NA_WIN_ROWS)
        start = pl.multiple_of(row0 * GRID_W, ROW_TILE)
        kw = k_ref[pl.ds(start, NA_WIN_ROWS * GRID_W), :]
        vw = v_ref[pl.ds(start, NA_WIN_ROWS * GRID_W), :]
        c = scale * LOG2E
        u_n = _dot_nt(q, kw) * c + bias_ref[...]
        u_c = s_c * c
        m = jnp.maximum(jnp.max(u_n, axis=-1, keepdims=True), jnp.max(u_c, axis=-1, keepdims=True))
        p_n, p_c = jnp.exp2(u_n - m).astype(BF16), jnp.exp2(u_c - m).astype(BF16)
        o = (_dot(p_n, vw) + _dot(p_c, vc)) / (_row_sum(p_n) + _row_sum(p_c))
        o_ref[...] = o.astype(o_ref.dtype)

    @pl.when(i >= n_groups)
    def _():
        o_ref[...] = _softmax_pv(s_c, vc, scale).astype(o_ref.dtype)


def _na_bias(rpb, T):
    rows = T // GRID_W
    n_groups = rows // NA_GROUP_ROWS
    n_roff, n_coff = 2 * NA_WIN_H - 1, 2 * NA_WIN_W - 1
    c = np.arange(GRID_W)[:, None]
    kc = np.arange(GRID_W)[None, :]
    cs = np.clip(c - NA_WIN_W // 2, 0, GRID_W - NA_WIN_W)
    col_valid = (kc >= cs) & (kc < cs + NA_WIN_W)
    coff = np.clip(kc - c + NA_WIN_W - 1, 0, n_coff - 1)
    col_sel = (coff[..., None] == np.arange(n_coff)).astype(np.float32)
    rpb = rpb.astype(F32) * LOG2E
    out = []
    for grp in (0, 1, n_groups - 1):
        r = grp * NA_GROUP_ROWS + np.arange(NA_GROUP_ROWS)[:, None]
        rs = np.clip(r - NA_WIN_H // 2, 0, rows - NA_WIN_H)
        row0 = np.clip(grp * NA_GROUP_ROWS - NA_WIN_H // 2, 0, rows - NA_WIN_ROWS)
        key_row = row0 + np.arange(NA_WIN_ROWS)[None, :]
        row_valid = (key_row >= rs) & (key_row < rs + NA_WIN_H)
        roff = np.clip(key_row - r + NA_WIN_H - 1, 0, n_roff - 1)
        row_sel = (roff[..., None] == np.arange(n_roff)).astype(np.float32)
        bias = jnp.einsum('ika,hab,cqb->hickq', row_sel, rpb, col_sel, precision=lax.Precision.HIGHEST)
        valid = row_valid[:, None, :, None] & col_valid[None, :, None, :]
        out.append(jnp.where(valid[None], bias, NEG_INF).reshape(rpb.shape[0], ROW_TILE, -1))
    return jnp.stack(out)


def _na(q, k, v, v_col0, bias, T, Tq):
    B, S = k.shape[0], k.shape[1]
    H = k.shape[2] // LANE
    vb = v_col0 // LANE
    n_groups = T // ROW_TILE
    nk = NA_WIN_ROWS * GRID_W

    def bias_map(b, h, i):
        return (jnp.where(i == 0, 0, jnp.where(i >= n_groups - 1, 2, 1)), h, 0, 0)

    return pl.pallas_call(
        functools.partial(_na_kernel, T=T, scale=ATT_SCALE),
        grid=(B, H, Tq // ROW_TILE),
        in_specs=[pl.BlockSpec((None, ROW_TILE, LANE), lambda b, h, i: (b, i, h)),
                  pl.BlockSpec((None, S, LANE), lambda b, h, i: (b, 0, h)),
                  pl.BlockSpec((None, S, LANE), lambda b, h, i: (b, 0, vb + h)),
                  pl.BlockSpec((None, None, ROW_TILE, nk), bias_map)],
        out_specs=pl.BlockSpec((None, ROW_TILE, LANE), lambda b, h, i: (b, i, h)),
        out_shape=jax.ShapeDtypeStruct((B, Tq, H * LANE), BF16),
        compiler_params=_cp("parallel", "parallel", "arbitrary"), name="na_attn",
    )(q, k, v, bias)


def _shortconv_kernel(*refs, T, with_ctx):
    u_refs, w_refs, o_refs = refs[0:3], refs[3:6], refs[6:]
    S = u_refs[0].shape[0]
    row = lax.broadcasted_iota(jnp.int32, (S, 1), 0)
    first = (row == 0) | (row == T)
    last = (row == T - 1) | (row == S - 1)
    for n in range(3):
        u = u_refs[n][...].astype(F32)
        w = w_refs[n][...]
        prev = jnp.where(first, 0.0, pltpu.roll(u, 1, 0))
        nxt = jnp.where(last, 0.0, pltpu.roll(u, S - 1, 0))
        y = (prev * w[0:1] + u * w[1:2] + nxt * w[2:3]).astype(BF16)
        o_refs[n][...] = y[:T]
        if with_ctx:
            o_refs[3 + n][...] = y[T:]


def _shortconv(proj, w, T, with_ctx):
    B, S, _ = proj.shape
    Lc = S - T
    nct = HY_W // LANE
    in_specs = [pl.BlockSpec((None, S, LANE), functools.partial(lambda b, c, n: (b, 0, n * nct + c), n=n))
                for n in range(3)]
    in_specs += [pl.BlockSpec((3, LANE), functools.partial(lambda b, c, n: (0, n * nct + c), n=n))
                 for n in range(3)]
    out_specs = [pl.BlockSpec((T, LANE), lambda b, c: (0, b * nct + c))] * 3
    out_shape = [jax.ShapeDtypeStruct((T, B * HY_W), BF16)] * 3
    if with_ctx:
        out_specs += [pl.BlockSpec((Lc, LANE), lambda b, c: (0, b * nct + c))] * 3
        out_shape += [jax.ShapeDtypeStruct((Lc, B * HY_W), BF16)] * 3
    return pl.pallas_call(
        functools.partial(_shortconv_kernel, T=T, with_ctx=with_ctx),
        grid=(B, nct), in_specs=in_specs, out_specs=out_specs, out_shape=out_shape,
        compiler_params=_cp("parallel", "arbitrary"), name="hy_shortconv",
    )(proj, proj, proj, w, w, w)


def _filter_kernel(z_ref, f1_ref, b1_ref, fr_ref, f2_ref, b2_ref, f3f_ref, f3b_ref, dl_ref, hs_ref, hd_ref):
    z = z_ref[...]
    fr = fr_ref[...]
    hid = jnp.sin(fr * (_dot_hi(z, f1_ref[...]) + b1_ref[...]))
    hid = jnp.sin(fr * (_dot_hi(hid, f2_ref[...]) + b2_ref[...]))
    decay = jnp.exp(-z[:, 0:1] * dl_ref[...])
    fw = _dot_hi(hid, f3f_ref[...]) * decay
    bw = _dot_hi(hid, f3b_ref[...]) * decay
    row = lax.broadcasted_iota(jnp.int32, bw.shape, 0)
    bw = jnp.where(row == 0, 0.0, bw)
    inv = 1.0 / (jnp.sum(jnp.abs(fw), axis=0, keepdims=True) + jnp.sum(jnp.abs(bw), axis=0, keepdims=True))
    hs_ref[...] = ((fw + bw) * inv).astype(BF16)
    hd_ref[...] = ((fw - bw) * inv).astype(BF16)


def _hyena_filter_sums(L, f1, b1, freq, f2, b2, f3):
    t = jnp.linspace(0.0, 1.0, L, dtype=F32)[:, None]
    w = (2.0 * math.pi / L) * jnp.arange(L, dtype=F32)[:, None]
    bands = jnp.linspace(1e-4, HY_BANDS - 1, HY_BANDS, dtype=F32)[None, :]
    z = jnp.concatenate([t, jnp.cos(bands * w), -jnp.sin(bands * w)], axis=-1)
    z = jnp.pad(z, ((0, 0), (0, LANE - HY_EMB)))
    f1p = jnp.pad(f1.astype(F32), ((0, LANE - HY_EMB), (0, 0)))
    ffn = f1.shape[1]
    deltas = jnp.abs(jnp.linspace(math.log(HY_TARGET) / HY_SLOW_DECAY, math.log(HY_TARGET) / HY_FAST_DECAY,
                                  HY_W, dtype=F32))
    NC = HY_ORDER * HY_W
    dl = jnp.tile(deltas, HY_ORDER).reshape(1, NC)
    tn = 256
    small = lambda shape: pl.BlockSpec(shape, lambda j: (0, 0))
    return pl.pallas_call(
        _filter_kernel,
        grid=(NC // tn,),
        in_specs=[small((L, LANE)), small((LANE, ffn)), small((1, ffn)), small((1, ffn)),
                  small((ffn, ffn)), small((1, ffn)),
                  pl.BlockSpec((ffn, tn), lambda j: (0, j)),
                  pl.BlockSpec((ffn, tn), lambda j: (0, NC // tn + j)),
                  pl.BlockSpec((1, tn), lambda j: (0, j))],
        out_specs=[pl.BlockSpec((L, tn), lambda j: (0, j))] * 2,
        out_shape=[jax.ShapeDtypeStruct((L, NC), BF16)] * 2,
        compiler_params=_cp("arbitrary"), name="hy_filter",
    )(z, f1p, b1.astype(F32).reshape(1, ffn), freq.astype(F32).reshape(1, ffn), f2.astype(F32),
      b2.astype(F32).reshape(1, ffn), f3.astype(F32), f3.astype(F32), dl)


def _dft_tables(L):
    N = 2 * L
    r = np.arange(N)
    k = (r // (2 * DFT_HALF)) * DFT_HALF + r % DFT_HALF
    is_im = (r // DFT_HALF) % 2 == 1
    nyq = is_im & (k == 0)
    kj = jnp.asarray(k, jnp.int32)[:, None]
    im, nyq = is_im[:, None], nyq[:, None]
    n_lo = 64

    def cos_sin(step, count):
        m = (kj * (step * jnp.arange(count, dtype=jnp.int32))[None, :]) % N
        ang = m.astype(F32) * (2.0 * math.pi / N)
        return jnp.cos(ang), jnp.sin(ang)

    c1, s1 = cos_sin(n_lo, L // n_lo)
    c0, s0 = cos_sin(1, n_lo)
    p1 = jnp.where(nyq, 1.0, jnp.where(im, -s1, c1))
    q1 = jnp.where(nyq, 0.0, jnp.where(im, -c1, -s1))
    p0 = jnp.where(nyq, (1 - 2 * (jnp.arange(n_lo) % 2)).astype(F32)[None, :], c0)
    a = (p1[:, :, None] * p0[:, None, :] + q1[:, :, None] * s0[:, None, :]).reshape(N, L).astype(BF16)
    return a, a.T


def _kf_kernel(a_ref, hs_ref, hd_ref, o_ref, *, n_fft):
    i = pl.program_id(0)
    H = DFT_HALF
    re = _dot(a_ref[:H, :], hs_ref[...])
    im = _dot(a_ref[H:, :], hd_ref[...])
    o_ref[:H, :] = re * (2.0 / n_fft)
    o_ref[H:, :] = im * (2.0 / n_fft)

    @pl.when(i == 0)
    def _():
        ny = _dot(a_ref[H:H + 16, :], hs_ref[...])
        o_ref[0:1, :] = re[0:1] * (1.0 / n_fft)
        o_ref[H:H + 1, :] = ny[0:1] * (1.0 / n_fft)


def _filter_spectrum(a, hs, hd):
    N, L = a.shape
    NC = hs.shape[1]
    tm, tn = 2 * DFT_HALF, 512
    return pl.pallas_call(
        functools.partial(_kf_kernel, n_fft=N),
        grid=(N // tm, NC // tn),
        in_specs=[pl.BlockSpec((tm, L), lambda i, j: (i, 0)),
                  pl.BlockSpec((L, tn), lambda i, j: (0, j)),
                  pl.BlockSpec((L, tn), lambda i, j: (0, j))],
        out_specs=pl.BlockSpec((tm, tn), lambda i, j: (i, j)),
        out_shape=jax.ShapeDtypeStruct((N, NC), F32),
        compiler_params=_cp("parallel", "arbitrary"), name="hy_filter_spectrum",
    )(a, hs, hd)


def _fwd_kernel(a_ref, z_ref, kf_ref, p_ref):
    i = pl.program_id(0)
    H = DFT_HALF
    acc = _dot(a_ref[...], z_ref[...])
    zr, zi = acc[:H], acc[H:]
    kr, ki = kf_ref[:H, :], kf_ref[H:, :]
    row = lax.broadcasted_iota(jnp.int32, zr.shape, 0)
    real_pair = (row == 0) & (i == 0)
    p_ref[:H, :] = jnp.where(real_pair, zr * kr, zr * kr - zi * ki).astype(BF16)
    p_ref[H:, :] = jnp.where(real_pair, zi * ki, zr * ki + zi * kr).astype(BF16)


def _dft_multiply(a, z, kf, order):
    N, L = a.shape
    NB = z.shape[1] // HY_W
    tm = 2 * DFT_HALF
    return pl.pallas_call(
        _fwd_kernel,
        grid=(N // tm, NB),
        in_specs=[pl.BlockSpec((tm, L), lambda i, j: (i, 0)),
                  pl.BlockSpec((L, HY_W), lambda i, j: (0, j)),
                  pl.BlockSpec((tm, HY_W), lambda i, j: (i, order))],
        out_specs=pl.BlockSpec((tm, HY_W), lambda i, j: (i, j)),
        out_shape=jax.ShapeDtypeStruct((N, NB * HY_W), BF16),
        compiler_params=_cp("parallel", "arbitrary"), name="hy_dft_multiply",
    )(a, z, kf)


def _inv_kernel(at_ref, p_ref, x_ref, z_ref, b_ref, o_ref):
    y = _dot(at_ref[...], p_ref[...])
    z = z_ref[...].astype(F32)
    o_ref[...] = (x_ref[...].astype(F32) * (y + b_ref[...] * z)).astype(o_ref.dtype)


def _idft_gate(at, p, xo, z, bias, batch_major):
    L, N = at.shape
    NB = z.shape[1] // HY_W
    tm = min(L, 512)
    if batch_major:
        out_spec = pl.BlockSpec((None, tm, HY_W), lambda i, j: (j, i, 0))
        out_shape = jax.ShapeDtypeStruct((NB, L, HY_W), BF16)
    else:
        out_spec = pl.BlockSpec((tm, HY_W), lambda i, j: (i, j))
        out_shape = jax.ShapeDtypeStruct((L, NB * HY_W), BF16)
    return pl.pallas_call(
        _inv_kernel,
        grid=(L // tm, NB),
        in_specs=[pl.BlockSpec((tm, N), lambda i, j: (i, 0)),
                  pl.BlockSpec((N, HY_W), lambda i, j: (0, j)),
                  pl.BlockSpec((tm, HY_W), lambda i, j: (i, j)),
                  pl.BlockSpec((tm, HY_W), lambda i, j: (i, j)),
                  pl.BlockSpec((1, HY_W), lambda i, j: (0, 0))],
        out_specs=out_spec, out_shape=out_shape,
        compiler_params=_cp("parallel", "arbitrary"), name="hy_idft_gate",
    )(at, p, xo, z, bias.astype(F32).reshape(1, HY_W))


def _hyena(vxx, tables, filt, hy_bias):
    a, at = tables
    kf = _filter_spectrum(a, *filt)
    z = vxx[0]
    for o in range(HY_ORDER):
        p = _dft_multiply(a, z, kf, o)
        z = _idft_gate(at, p, vxx[1 + o], z, hy_bias[o], batch_major=(o == HY_ORDER - 1))
    return z


def _merge_kernel(ya_ref, yb_ref, yc_ref, yd_ref, g0_ref, g1_ref, g2_ref, g3_ref, wb_ref, wo_ref, x_ref, gate_ref,
                  o_ref):
    ys = (ya_ref, yb_ref, yc_ref, yd_ref)
    gs = (g0_ref, g1_ref, g2_ref, g3_ref)
    acc = _dot(ys[0][...], wb_ref[0]) * gs[0][...].astype(F32)
    for n in range(1, N_BRANCH):
        acc += _dot(ys[n][...], wb_ref[n]) * gs[n][...].astype(F32)
    o_ref[...] = x_ref[...] + gate_ref[...] * _dot(acc.astype(BF16), wo_ref[...])


def _merge(ys, gates, wb, wo, l, x, modt, q_gate, tm, row_tile0, mod_row):
    B, R, D = x.shape
    row = lambda b, i: (b, row_tile0 + i, 0)
    in_specs = [pl.BlockSpec((None, tm, BRANCH_W), row)] * N_BRANCH
    in_specs += [pl.BlockSpec((None, tm, D), functools.partial(lambda b, i, n: (b, row_tile0 + i, n), n=n))
                 for n in range(N_BRANCH)]
    in_specs += [pl.BlockSpec((None, N_BRANCH, BRANCH_W, D), lambda b, i: (l, 0, 0, 0),
                              pipeline_mode=pl.Buffered(1)),
                 pl.BlockSpec((None, D, D), lambda b, i: (l, 0, 0), pipeline_mode=pl.Buffered(1)),
                 pl.BlockSpec((None, tm, D), lambda b, i: (b, i, 0)),
                 pl.BlockSpec((None, None, 1, D), lambda b, i: (mod_row(b), q_gate, 0, 0))]
    return pl.pallas_call(
        _merge_kernel,
        grid=(B, R // tm), in_specs=in_specs,
        out_specs=pl.BlockSpec((None, tm, D), lambda b, i: (b, i, 0)),
        out_shape=jax.ShapeDtypeStruct((B, R, D), F32),
        compiler_params=_cp("parallel", "arbitrary"), name="merge",
    )(*ys, gates, gates, gates, gates, wb, wo, x, modt)


def _expert_up_kernel(x_ref, w1_ref, w3_ref, o_ref):
    x = x_ref[...]
    a = _dot(x, w1_ref[...].astype(BF16))
    b = _dot(x, w3_ref[...].astype(BF16))
    o_ref[...] = (a * jax.nn.sigmoid(a) * b).astype(o_ref.dtype)


def _expert_down_kernel(h_ref, g_ref, w2_ref, o_ref):
    o_ref[...] = _dot(h_ref[...], w2_ref[...].astype(BF16)) * g_ref[...]


def _experts(xg, gate, w1, w3, w2, l):
    E, M, D = xg.shape
    F = w1.shape[3]
    tf = 256
    hmid = pl.pallas_call(
        _expert_up_kernel,
        grid=(E, F // tf),
        in_specs=[pl.BlockSpec((None, M, D), lambda e, f: (e, 0, 0)),
                  pl.BlockSpec((None, None, D, tf), lambda e, f: (l, e, 0, f)),
                  pl.BlockSpec((None, None, D, tf), lambda e, f: (l, e, 0, f))],
        out_specs=pl.BlockSpec((None, M, tf), lambda e, f: (e, 0, f)),
        out_shape=jax.ShapeDtypeStruct((E, M, F), BF16),
        compiler_params=_cp("parallel", "arbitrary"), name="expert_up",
    )(xg, w1, w3)
    return pl.pallas_call(
        _expert_down_kernel,
        grid=(E, D // tf),
        in_specs=[pl.BlockSpec((None, M, F), lambda e, n: (e, 0, 0)),
                  pl.BlockSpec((None, M, 1), lambda e, n: (e, 0, 0)),
                  pl.BlockSpec((None, None, F, tf), lambda e, n: (l, e, 0, n))],
        out_specs=pl.BlockSpec((None, M, tf), lambda e, n: (e, 0, n)),
        out_shape=jax.ShapeDtypeStruct((E, M, D), F32),
        compiler_params=_cp("parallel", "arbitrary"), name="expert_down",
    )(hmid, gate, w2)


def _moe(h2, aff, T, Lc, w1, w3, w2, l):
    B, S, D = h2.shape
    E = w1.shape[1]
    base = (jnp.arange(B, dtype=jnp.int32) * S)[:, None, None]

    def route(a, off):
        n = a.shape[1]
        gate, idx = lax.top_k(a.transpose(0, 2, 1), CAPACITY_FACTOR * n // E)
        rows = (idx + base + off).transpose(1, 0, 2).reshape(E, -1)
        return gate.transpose(1, 0, 2).reshape(E, -1), rows

    gate, rows = route(aff[:, :T, :E], 0)
    if Lc:
        gate_c, rows_c = route(aff[:, T:, :E], T)
        gate, rows = jnp.concatenate([gate, gate_c], axis=1), jnp.concatenate([rows, rows_c], axis=1)
    xg = h2.reshape(B * S, D)[rows]
    y = _experts(xg, gate[..., None], w1, w3, w2, l)
    return jnp.zeros((B * S, D), F32).at[rows.reshape(-1)].add(y.reshape(-1, D))


def kernel(x, c, ctx, c_ctx, w_mod, b_mod, norm1, norm2, w_in, hy_short, hy_f1, hy_b1, hy_freq, hy_f2, hy_b2, hy_f3, hy_bias, qn_b, kn_b, qn_c, kn_c, rpb_c, qn_d, kn_d, lam_q1, lam_k1, lam_q2, lam_k2, subln_d, w_branch, w_out, w_router, w_e1, w_e3, w_e2):
    B, T, D = x.shape
    Lc = ctx.shape[1]
    S = T + Lc
    depth = w_mod.shape[0]
    assert T % ROW_TILE == 0 and Lc == ROW_TILE and T // GRID_W >= NA_WIN_ROWS

    cc = jnp.concatenate([c, c_ctx[None, :], jnp.zeros((8 - B - 1, D), F32)], axis=0)
    rope_h = _rope_tables(T, S, HEAD_DIM)
    rope_d = _rope_tables(T, S, DF_QK_DIM)
    dft_lat = _dft_tables(T)
    dft_ctx = _dft_tables(Lc)
    w_in_bf, wb, wo = w_in.astype(BF16), w_branch.astype(BF16), w_out.astype(BF16)

    for l in range(depth):
        last = l == depth - 1
        n_ctx = 0 if last else 1
        Tq = T + n_ctx * ROW_TILE
        lam_init = 0.8 - 0.6 * math.exp(-0.3 * l)
        lam = (jnp.exp(jnp.sum(lam_q1[l].astype(F32) * lam_k1[l].astype(F32)))
               - jnp.exp(jnp.sum(lam_q2[l].astype(F32) * lam_k2[l].astype(F32))) + lam_init)
        modt = _modulation(cc, w_mod, b_mod, l).reshape(8, 6, 1, D)

        h = _prenorm(x, ctx, norm1[l], modt, 1, 0, 1)
        h2d = h.reshape(B * S, D)
        proj = _matmul(h2d, w_in_bf, l, 0, C_GATES, BF16, name="in_proj").reshape(B, S, C_GATES)
        gates = _matmul(h2d, w_in_bf, l, C_GATES, N_BRANCH * D, BF16, sigmoid=True,
                        name="in_proj_gates").reshape(B, S, N_BRANCH * D)

        q_b = _qkpost(proj, C_QB, GQA_HEADS, qn_b[l], HEAD_DIM, rope_h)
        k_b = _qkpost(proj, C_KB, GQA_KV, kn_b[l], HEAD_DIM, rope_h)
        q_c = _qkpost(proj, C_QC, NA_HEADS, qn_c[l], HEAD_DIM, None)
        k_c = _qkpost(proj, C_KC, NA_HEADS, kn_c[l], HEAD_DIM, None)
        q_d = _qkpost(proj, C_QD, DF_HEADS, qn_d[l], DF_QK_DIM, rope_d)
        k_d = _qkpost(proj, C_KD, DF_HEADS, kn_d[l], DF_QK_DIM, rope_d)

        y_b = _gqa(q_b, k_b, proj, C_VB, T, Tq, GQA_HEADS // GQA_KV, ATT_SCALE)
        y_c = _na(q_c, k_c, proj, C_VC, _na_bias(rpb_c[l], T), T, Tq)
        y_d = _diff_attn(lam, q_d, k_d, proj, C_VD, subln_d[l], T, Tq, 1.0 - lam_init)

        conv = _shortconv(proj, hy_short[l].astype(F32), T, with_ctx=not last)
        filt = _hyena_filter_sums(T, hy_f1[l], hy_b1[l], hy_freq[l], hy_f2[l], hy_b2[l], hy_f3[l])
        y_a = _hyena(conv[:3], dft_lat, filt, hy_bias[l])
        if not last:
            filt_c = _hyena_filter_sums(Lc, hy_f1[l], hy_b1[l], hy_freq[l], hy_f2[l], hy_b2[l], hy_f3[l])
            y_a = jnp.concatenate([y_a, _hyena(conv[3:], dft_ctx, filt_c, hy_bias[l])], axis=1)

        ys = (y_a, y_b, y_c, y_d)
        x = _merge(ys, gates, wb, wo, l, x, modt, 2, ROW_TILE, 0, lambda b: b)
        if not last:
            ctx = _merge(ys, gates, wb, wo, l, ctx, modt, 2, ROW_TILE, T // ROW_TILE, lambda b: B)

        h2, aff = _prenorm(x, ctx, norm2[l], modt, 4, 3, n_ctx, w_router=w_router[l].astype(F32))
        moe = _moe(h2, aff, T, n_ctx * Lc, w_e1, w_e3, w_e2, l).reshape(B, Tq, D)
        g2 = modt[:, 5, 0, :]
        x = x + g2[:B, None, :] * moe[:, :T]
        if not last:
            ctx = ctx + g2[B][None, None, :] * moe[:, T:]
    return x
```

```python
import functools
import math

import numpy as np
import jax
import jax.numpy as jnp
from jax import lax
from jax.experimental import pallas as pl
from jax.experimental.pallas import tpu as pltpu

F32 = jnp.float32
BF16 = jnp.bfloat16

GRID_W = 64
HEAD_DIM = 128
BRANCH_W = 512
N_BRANCH = 4
HY_W = 512
HY_ORDER = 2
HY_BANDS = 16
HY_EMB = 1 + 2 * HY_BANDS
HY_FAST_DECAY = 0.3
HY_SLOW_DECAY = 1.5
HY_TARGET = 1e-2
GQA_HEADS = 4
GQA_KV = 2
NA_HEADS = 4
NA_WIN_H = 8
NA_WIN_W = 16
DF_HEADS = 4
DF_QK_DIM = 64
N_EXPERTS = 16
CAPACITY_FACTOR = 2
ROPE_THETA = 10000.0
NORM_EPS = 1e-6
ATT_SCALE = HEAD_DIM ** -0.5
DF_SCALE = DF_QK_DIM ** -0.5
NEG_INF = -1e30
LOG2E = math.log2(math.e)

LANE = 128
ROW_TILE = 256
PROJ_TILE_N = 512
NA_GROUP_ROWS = 4
NA_WIN_ROWS = NA_GROUP_ROWS + NA_WIN_H
DFT_HALF = 256
VMEM_LIMIT = 56 * 1024 * 1024

C_HY = 0
C_QB = 3 * HY_W
C_KB = C_QB + GQA_HEADS * HEAD_DIM
C_VB = C_KB + GQA_KV * HEAD_DIM
C_QC = C_VB + GQA_KV * HEAD_DIM
C_KC = C_QC + NA_HEADS * HEAD_DIM
C_VC = C_KC + NA_HEADS * HEAD_DIM
C_QD = C_VC + NA_HEADS * HEAD_DIM
C_KD = C_QD + DF_HEADS * 2 * DF_QK_DIM
C_VD = C_KD + DF_HEADS * 2 * DF_QK_DIM
C_GATES = C_VD + DF_HEADS * HEAD_DIM


def _cp(*sem):
    return pltpu.CompilerParams(dimension_semantics=sem, vmem_limit_bytes=VMEM_LIMIT)


def _dot(a, b):
    return jnp.dot(a, b, preferred_element_type=F32)


def _dot_nt(a, b):
    return lax.dot_general(a, b, (((1,), (1,)), ((), ())), preferred_element_type=F32)


def _dot_hi(a, b):
    return jnp.dot(a, b, preferred_element_type=F32, precision=lax.Precision.HIGHEST)


def _mod_kernel(c_ref, w_ref, b_ref, o_ref):
    c = c_ref[...]
    a = (c * jax.nn.sigmoid(c)).astype(BF16)
    o_ref[...] = _dot(a, w_ref[...].astype(BF16)) + b_ref[...]


def _modulation(cc, w, b, l):
    depth, D, N = w.shape
    tn = math.gcd(N, 1024)
    return pl.pallas_call(
        _mod_kernel,
        grid=(N // tn,),
        in_specs=[pl.BlockSpec((8, D), lambda j: (0, 0)),
                  pl.BlockSpec((None, D, tn), lambda j: (l, 0, j)),
                  pl.BlockSpec((None, 1, tn), lambda j: (l, 0, j))],
        out_specs=pl.BlockSpec((8, tn), lambda j: (0, j)),
        out_shape=jax.ShapeDtypeStruct((8, N), F32),
        compiler_params=_cp("arbitrary"),
        name="modulation",
    )(cc, w, b.reshape(depth, 1, N))


def _prenorm_body(x, g_ref, sc_ref, sh_ref):
    y = x * lax.rsqrt(jnp.mean(x * x, axis=-1, keepdims=True) + NORM_EPS)
    return y * g_ref[...] * (1.0 + sc_ref[...]) + sh_ref[...]


def _prenorm_kernel(x_ref, c_ref, g_ref, sc_ref, sh_ref, o_ref, *, n_lat):
    i = pl.program_id(1)

    @pl.when(i < n_lat)
    def _():
        o_ref[...] = _prenorm_body(x_ref[...], g_ref, sc_ref, sh_ref).astype(BF16)

    @pl.when(i >= n_lat)
    def _():
        o_ref[...] = _prenorm_body(c_ref[...], g_ref, sc_ref, sh_ref).astype(BF16)


def _prenorm_router_kernel(x_ref, c_ref, g_ref, sc_ref, sh_ref, wr_ref, o_ref, a_ref, *, n_lat, n_exp):
    i = pl.program_id(1)

    def run(x):
        h = _prenorm_body(x, g_ref, sc_ref, sh_ref)
        o_ref[...] = h.astype(BF16)
        logits = _dot_hi(h, wr_ref[...])
        lane = lax.broadcasted_iota(jnp.int32, logits.shape, 1)
        logits = jnp.where(lane < n_exp, logits, NEG_INF)
        e = jnp.exp(logits - jnp.max(logits, axis=-1, keepdims=True))
        a_ref[...] = e / jnp.sum(e, axis=-1, keepdims=True)

    @pl.when(i < n_lat)
    def _():
        run(x_ref[...])

    @pl.when(i >= n_lat)
    def _():
        run(c_ref[...])


def _prenorm(x, ctx, gain, modt, q_scale, q_shift, n_ctx_tiles, w_router=None):
    B, T, D = x.shape
    n_lat = T // ROW_TILE
    nt = n_lat + n_ctx_tiles
    S = nt * ROW_TILE

    def mod_map(q):
        return lambda b, i: (jnp.where(i < n_lat, b, B), q, 0, 0)

    in_specs = [
        pl.BlockSpec((None, ROW_TILE, D), lambda b, i: (b, jnp.minimum(i, n_lat - 1), 0)),
        pl.BlockSpec((None, ROW_TILE, D), lambda b, i: (b, jnp.maximum(i - n_lat, 0), 0)),
        pl.BlockSpec((1, D), lambda b, i: (0, 0)),
        pl.BlockSpec((None, None, 1, D), mod_map(q_scale)),
        pl.BlockSpec((None, None, 1, D), mod_map(q_shift)),
    ]
    out_h = pl.BlockSpec((None, ROW_TILE, D), lambda b, i: (b, i, 0))
    shape_h = jax.ShapeDtypeStruct((B, S, D), BF16)
    args = [x, ctx, gain.reshape(1, D), modt, modt]
    if w_router is None:
        return pl.pallas_call(
            functools.partial(_prenorm_kernel, n_lat=n_lat),
            grid=(B, nt), in_specs=in_specs, out_specs=out_h, out_shape=shape_h,
            compiler_params=_cp("parallel", "arbitrary"), name="prenorm",
        )(*args)
    n_exp = w_router.shape[1]
    wr = jnp.pad(w_router, ((0, 0), (0, LANE - n_exp)))
    return pl.pallas_call(
        functools.partial(_prenorm_router_kernel, n_lat=n_lat, n_exp=n_exp),
        grid=(B, nt),
        in_specs=in_specs + [pl.BlockSpec((D, LANE), lambda b, i: (0, 0))],
        out_specs=[out_h, pl.BlockSpec((None, ROW_TILE, LANE), lambda b, i: (b, i, 0))],
        out_shape=[shape_h, jax.ShapeDtypeStruct((B, S, LANE), F32)],
        compiler_params=_cp("parallel", "arbitrary"), name="prenorm_router",
    )(*args, wr)


def _mm_kernel(a_ref, b_ref, o_ref, *, sigmoid):
    acc = _dot(a_ref[...], b_ref[...])
    if sigmoid:
        acc = jax.nn.sigmoid(acc)
    o_ref[...] = acc.astype(o_ref.dtype)


def _matmul(a, b, l, col0, N, out_dtype, sigmoid=False, name="matmul"):
    M, K = a.shape
    tm, tn = math.gcd(M, 1024), PROJ_TILE_N
    cb = col0 // tn
    return pl.pallas_call(
        functools.partial(_mm_kernel, sigmoid=sigmoid),
        grid=(M // tm, N // tn),
        in_specs=[pl.BlockSpec((tm, K), lambda i, j: (i, 0)),
                  pl.BlockSpec((None, K, tn), lambda i, j: (l, 0, cb + j))],
        out_specs=pl.BlockSpec((tm, tn), lambda i, j: (i, j)),
        out_shape=jax.ShapeDtypeStruct((M, N), out_dtype),
        compiler_params=_cp("parallel", "arbitrary"), name=name,
    )(a, b)


def _qk_post(x, g, seg, tables):
    lane = lax.broadcasted_iota(jnp.int32, (1, LANE), 1)
    sq = x * x
    if seg == LANE:
        ms = jnp.mean(sq, axis=-1, keepdims=True)
    else:
        lo = jnp.sum(jnp.where(lane < seg, sq, 0.0), axis=-1, keepdims=True)
        hi = jnp.sum(jnp.where(lane >= seg, sq, 0.0), axis=-1, keepdims=True)
        ms = jnp.where(lane < seg, lo, hi) * (1.0 / seg)
    y = x * lax.rsqrt(ms + NORM_EPS) * g
    if tables is not None:
        cos_ref, sin_ref = tables
        q = seg // 4
        partner = jnp.where((lane % (seg // 2)) < q, pltpu.roll(y, LANE - q, 1), pltpu.roll(y, q, 1))
        y = y * cos_ref[...] + partner * sin_ref[...]
    return y


def _proj_block_kinds():
    kinds = []
    for col in range(0, C_GATES, LANE):
        if C_QB <= col < C_VB:
            kinds.append((HEAD_DIM, True))
        elif C_QC <= col < C_VC:
            kinds.append((HEAD_DIM, False))
        elif C_QD <= col < C_VD:
            kinds.append((DF_QK_DIM, True))
        else:
            kinds.append(None)
    return kinds


def _inproj_kernel(a_ref, w_ref, g_ref, c128_ref, s128_ref, c64_ref, s64_ref, o_ref):
    j = pl.program_id(1)
    acc = _dot(a_ref[...], w_ref[...])
    per_tile = PROJ_TILE_N // LANE
    kinds = _proj_block_kinds()
    tiles = [kinds[t * per_tile:(t + 1) * per_tile] for t in range(len(kinds) // per_tile)]
    plain = functools.reduce(jnp.logical_or, [j == t for t, ks in enumerate(tiles) if not any(ks)])

    @pl.when(plain)
    def _():
        o_ref[...] = acc.astype(o_ref.dtype)

    for t, ks in enumerate(tiles):
        if not any(ks):
            continue

        @pl.when(j == t)
        def _(ks=ks):
            for h, kind in enumerate(ks):
                y = acc[:, h * LANE:(h + 1) * LANE]
                if kind is not None:
                    seg, rope = kind
                    tables = None if not rope else ((c128_ref, s128_ref) if seg == HEAD_DIM else (c64_ref, s64_ref))
                    y = _qk_post(y, g_ref[h:h + 1, :], seg, tables)
                o_ref[:, h * LANE:(h + 1) * LANE] = y.astype(o_ref.dtype)


def _in_proj(h2d, w_in_bf, l, gains, rope_h, rope_d):
    M, K = h2d.shape
    tm, tn = math.gcd(M, 1024), PROJ_TILE_N
    per_tile = tn // LANE
    table = pl.BlockSpec((tm, LANE), lambda i, j: (i, 0))
    return pl.pallas_call(
        _inproj_kernel,
        grid=(M // tm, C_GATES // tn),
        in_specs=[pl.BlockSpec((tm, K), lambda i, j: (i, 0)),
                  pl.BlockSpec((None, K, tn), lambda i, j: (l, 0, j)),
                  pl.BlockSpec((None, per_tile, LANE), lambda i, j: (j, 0, 0)),
                  table, table, table, table],
        out_specs=pl.BlockSpec((tm, tn), lambda i, j: (i, j)),
        out_shape=jax.ShapeDtypeStruct((M, C_GATES), BF16),
        compiler_params=_cp("parallel", "arbitrary"), name="in_proj",
    )(h2d, w_in_bf, gains.reshape(-1, per_tile, LANE), *rope_h, *rope_d)


def _proj_gains(qn_b, kn_b, qn_c, kn_c, qn_d, kn_d):
    ones = lambda n: jnp.ones((n, LANE), F32)
    rep = lambda g, n: jnp.tile(jnp.tile(g.astype(F32), LANE // g.shape[0])[None, :], (n, 1))
    return jnp.concatenate([
        ones(C_QB // LANE), rep(qn_b, GQA_HEADS), rep(kn_b, GQA_KV), ones(GQA_KV),
        rep(qn_c, NA_HEADS), rep(kn_c, NA_HEADS), ones(NA_HEADS),
        rep(qn_d, DF_HEADS), rep(kn_d, DF_HEADS), ones(DF_HEADS)], axis=0)


def _rope_tables(B, T, S, seg):
    half = seg // 2
    nfreq = half // 2
    inv = ROPE_THETA ** (-jnp.arange(0, half, 2, dtype=F32) / half)
    pos = jnp.arange(T, dtype=jnp.int32)
    rows, cols = (pos // GRID_W).astype(F32), (pos % GRID_W).astype(F32)
    l = np.arange(LANE) % seg
    use_col = l >= half
    fidx = (l % half) % nfreq
    is_b = (l % half) >= nfreq
    ang = jnp.where(use_col[None, :], cols[:, None], rows[:, None]) * inv[fidx][None, :]
    cos, sin = jnp.cos(ang), jnp.sin(ang)
    sin = jnp.where(is_b[None, :], sin, -sin)
    cos = jnp.concatenate([cos, jnp.ones((S - T, LANE), F32)], axis=0)
    sin = jnp.concatenate([sin, jnp.zeros((S - T, LANE), F32)], axis=0)
    return jnp.tile(cos, (B, 1)), jnp.tile(sin, (B, 1))


def _softmax_rows(s, scale):
    c = scale * LOG2E
    m = jnp.max(s, axis=-1, keepdims=True)
    p = jnp.exp2(s * c - m * c)
    return p.astype(BF16), jnp.sum(p, axis=-1, keepdims=True)


def _attend(qs, k, v, scale):
    scores = [_dot_nt(q, k) for q in qs]
    outs = []
    for s in scores:
        p, l = _softmax_rows(s, scale)
        outs.append(_dot(p, v) / l)
    return outs


def _gqa_kernel(q_ref, k_ref, v_ref, o_ref, *, T, R, scale):
    i = pl.program_id(2)

    def attend(k, v):
        outs = _attend([q_ref[:, r * LANE:(r + 1) * LANE] for r in range(R)], k, v, scale)
        for r in range(R):
            o_ref[:, r * LANE:(r + 1) * LANE] = outs[r].astype(o_ref.dtype)

    @pl.when(i < T // ROW_TILE)
    def _():
        attend(k_ref[...], v_ref[...])

    @pl.when(i >= T // ROW_TILE)
    def _():
        attend(k_ref[T:, :], v_ref[T:, :])


def _gqa(proj, T, Tq, scale):
    B, S, _ = proj.shape
    R = GQA_HEADS // GQA_KV
    qb, kb, vb = C_QB // (R * LANE), C_KB // LANE, C_VB // LANE
    return pl.pallas_call(
        functools.partial(_gqa_kernel, T=T, R=R, scale=scale),
        grid=(B, GQA_KV, Tq // ROW_TILE),
        in_specs=[pl.BlockSpec((None, ROW_TILE, R * LANE), lambda b, g, i: (b, i, qb + g)),
                  pl.BlockSpec((None, S, LANE), lambda b, g, i: (b, 0, kb + g)),
                  pl.BlockSpec((None, S, LANE), lambda b, g, i: (b, 0, vb + g))],
        out_specs=pl.BlockSpec((None, ROW_TILE, R * LANE), lambda b, g, i: (b, i, g)),
        out_shape=jax.ShapeDtypeStruct((B, Tq, GQA_HEADS * LANE), BF16),
        compiler_params=_cp("parallel", "parallel", "arbitrary"), name="gqa",
    )(proj, proj, proj)


def _diff_kernel(lam_ref, q_ref, k_ref, v_ref, g_ref, o_ref, *, T, scale, out_scale):
    i = pl.program_id(2)
    lam = lam_ref[0]

    def attend(k, v):
        q = q_ref[...]
        lane = lax.broadcasted_iota(jnp.int32, q.shape, 1)
        zero = jnp.zeros_like(q)
        o1, o2 = _attend([jnp.where(lane < DF_QK_DIM, q, zero), jnp.where(lane >= DF_QK_DIM, q, zero)],
                         k, v, scale)
        o = o1 - lam * o2
        y = o * lax.rsqrt(jnp.mean(o * o, axis=-1, keepdims=True) + NORM_EPS) * g_ref[...]
        o_ref[...] = (y * out_scale).astype(o_ref.dtype)

    @pl.when(i < T // ROW_TILE)
    def _():
        attend(k_ref[...], v_ref[...])

    @pl.when(i >= T // ROW_TILE)
    def _():
        attend(k_ref[T:, :], v_ref[T:, :])


def _diff_attn(lam, proj, subln, T, Tq, out_scale):
    B, S, _ = proj.shape
    qb, kb, vb = C_QD // LANE, C_KD // LANE, C_VD // LANE
    return pl.pallas_call(
        functools.partial(_diff_kernel, T=T, scale=DF_SCALE, out_scale=out_scale),
        grid=(B, DF_HEADS, Tq // ROW_TILE),
        in_specs=[pl.BlockSpec(memory_space=pltpu.SMEM),
                  pl.BlockSpec((None, ROW_TILE, LANE), lambda b, h, i: (b, i, qb + h)),
                  pl.BlockSpec((None, S, LANE), lambda b, h, i: (b, 0, kb + h)),
                  pl.BlockSpec((None, S, LANE), lambda b, h, i: (b, 0, vb + h)),
                  pl.BlockSpec((1, LANE), lambda b, h, i: (0, 0))],
        out_specs=pl.BlockSpec((None, ROW_TILE, LANE), lambda b, h, i: (b, i, h)),
        out_shape=jax.ShapeDtypeStruct((B, Tq, DF_HEADS * LANE), BF16),
        compiler_params=_cp("parallel", "parallel", "arbitrary"), name="diff_attn",
    )(lam.reshape(1).astype(F32), proj, proj, proj, subln.astype(F32).reshape(1, LANE))


def _na_kernel(q_ref, k_ref, v_ref, bias_ref, o_ref, *, T, scale):
    i = pl.program_id(2)
    n_groups = T // ROW_TILE
    grid_rows = T // GRID_W
    q = q_ref[...]
    kc, vc = k_ref[T:, :], v_ref[T:, :]
    s_c = _dot_nt(q, kc)

    @pl.when(i < n_groups)
    def _():
        row0 = jnp.clip(i * NA_GROUP_ROWS - NA_WIN_H // 2, 0, grid_rows - NA_WIN_ROWS)
        start = pl.multiple_of(row0 * GRID_W, ROW_TILE)
        kw = k_ref[pl.ds(start, NA_WIN_ROWS * GRID_W), :]
        vw = v_ref[pl.ds(start, NA_WIN_ROWS * GRID_W), :]
        c = scale * LOG2E
        u_n = _dot_nt(q, kw) * c + bias_ref[...]
        u_c = s_c * c
        m = jnp.maximum(jnp.max(u_n, axis=-1, keepdims=True), jnp.max(u_c, axis=-1, keepdims=True))
        p_n, p_c = jnp.exp2(u_n - m), jnp.exp2(u_c - m)
        l = jnp.sum(p_n, axis=-1, keepdims=True) + jnp.sum(p_c, axis=-1, keepdims=True)
        o = (_dot(p_n.astype(BF16), vw) + _dot(p_c.astype(BF16), vc)) / l
        o_ref[...] = o.astype(o_ref.dtype)

    @pl.when(i >= n_groups)
    def _():
        p, l = _softmax_rows(s_c, scale)
        o_ref[...] = (_dot(p, vc) / l).astype(o_ref.dtype)


def _na_bias(rpb, T):
    rows = T // GRID_W
    n_groups = rows // NA_GROUP_ROWS
    n_roff, n_coff = 2 * NA_WIN_H - 1, 2 * NA_WIN_W - 1
    c = np.arange(GRID_W)[:, None]
    kc = np.arange(GRID_W)[None, :]
    cs = np.clip(c - NA_WIN_W // 2, 0, GRID_W - NA_WIN_W)
    col_valid = (kc >= cs) & (kc < cs + NA_WIN_W)
    coff = np.clip(kc - c + NA_WIN_W - 1, 0, n_coff - 1)
    col_sel = (coff[..., None] == np.arange(n_coff)).astype(np.float32)
    rpb = rpb.astype(F32) * LOG2E
    out = []
    for grp in (0, 1, n_groups - 1):
        r = grp * NA_GROUP_ROWS + np.arange(NA_GROUP_ROWS)[:, None]
        rs = np.clip(r - NA_WIN_H // 2, 0, rows - NA_WIN_H)
        row0 = np.clip(grp * NA_GROUP_ROWS - NA_WIN_H // 2, 0, rows - NA_WIN_ROWS)
        key_row = row0 + np.arange(NA_WIN_ROWS)[None, :]
        row_valid = (key_row >= rs) & (key_row < rs + NA_WIN_H)
        roff = np.clip(key_row - r + NA_WIN_H - 1, 0, n_roff - 1)
        row_sel = (roff[..., None] == np.arange(n_roff)).astype(np.float32)
        bias = jnp.einsum('ika,hab,cqb->hickq', row_sel, rpb, col_sel, precision=lax.Precision.HIGHEST)
        valid = row_valid[:, None, :, None] & col_valid[None, :, None, :]
        out.append(jnp.where(valid[None], bias, NEG_INF).reshape(rpb.shape[0], ROW_TILE, -1))
    return jnp.stack(out)


def _na(proj, bias, T, Tq):
    B, S, _ = proj.shape
    qb, kb, vb = C_QC // LANE, C_KC // LANE, C_VC // LANE
    n_groups = T // ROW_TILE
    nk = NA_WIN_ROWS * GRID_W

    def bias_map(b, h, i):
        return (jnp.where(i == 0, 0, jnp.where(i >= n_groups - 1, 2, 1)), h, 0, 0)

    return pl.pallas_call(
        functools.partial(_na_kernel, T=T, scale=ATT_SCALE),
        grid=(B, NA_HEADS, Tq // ROW_TILE),
        in_specs=[pl.BlockSpec((None, ROW_TILE, LANE), lambda b, h, i: (b, i, qb + h)),
                  pl.BlockSpec((None, S, LANE), lambda b, h, i: (b, 0, kb + h)),
                  pl.BlockSpec((None, S, LANE), lambda b, h, i: (b, 0, vb + h)),
                  pl.BlockSpec((None, None, ROW_TILE, nk), bias_map)],
        out_specs=pl.BlockSpec((None, ROW_TILE, LANE), lambda b, h, i: (b, i, h)),
        out_shape=jax.ShapeDtypeStruct((B, Tq, NA_HEADS * LANE), BF16),
        compiler_params=_cp("parallel", "parallel", "arbitrary"), name="na_attn",
    )(proj, proj, proj, bias)


def _shortconv_kernel(*refs, T, with_ctx):
    u_refs, w_refs, o_refs = refs[0:3], refs[3:6], refs[6:]
    S = u_refs[0].shape[0]
    row = lax.broadcasted_iota(jnp.int32, (S, 1), 0)
    first = (row == 0) | (row == T)
    last = (row == T - 1) | (row == S - 1)
    for n in range(3):
        u = u_refs[n][...].astype(F32)
        w = w_refs[n][...]
        prev = jnp.where(first, 0.0, pltpu.roll(u, 1, 0))
        nxt = jnp.where(last, 0.0, pltpu.roll(u, S - 1, 0))
        y = (prev * w[0:1] + u * w[1:2] + nxt * w[2:3]).astype(BF16)
        o_refs[n][...] = y[:T]
        if with_ctx:
            o_refs[3 + n][...] = y[T:]


def _shortconv(proj, w, T, with_ctx):
    B, S, _ = proj.shape
    Lc = S - T
    nct = HY_W // LANE
    in_specs = [pl.BlockSpec((None, S, LANE), functools.partial(lambda b, c, n: (b, 0, n * nct + c), n=n))
                for n in range(3)]
    in_specs += [pl.BlockSpec((3, LANE), functools.partial(lambda b, c, n: (0, n * nct + c), n=n))
                 for n in range(3)]
    out_specs = [pl.BlockSpec((T, LANE), lambda b, c: (0, b * nct + c))] * 3
    out_shape = [jax.ShapeDtypeStruct((T, B * HY_W), BF16)] * 3
    if with_ctx:
        out_specs += [pl.BlockSpec((Lc, LANE), lambda b, c: (0, b * nct + c))] * 3
        out_shape += [jax.ShapeDtypeStruct((Lc, B * HY_W), BF16)] * 3
    return pl.pallas_call(
        functools.partial(_shortconv_kernel, T=T, with_ctx=with_ctx),
        grid=(B, nct), in_specs=in_specs, out_specs=out_specs, out_shape=out_shape,
        compiler_params=_cp("parallel", "arbitrary"), name="hy_shortconv",
    )(proj, proj, proj, w, w, w)


def _filter_kernel(z_ref, f1_ref, b1_ref, fr_ref, f2_ref, b2_ref, f3f_ref, f3b_ref, dl_ref, hs_ref, hd_ref):
    z = z_ref[...]
    fr = fr_ref[...]
    hid = jnp.sin(fr * (_dot_hi(z, f1_ref[...]) + b1_ref[...]))
    hid = jnp.sin(fr * (_dot_hi(hid, f2_ref[...]) + b2_ref[...]))
    decay = jnp.exp(-z[:, 0:1] * dl_ref[...])
    fw = _dot_hi(hid, f3f_ref[...]) * decay
    bw = _dot_hi(hid, f3b_ref[...]) * decay
    row = lax.broadcasted_iota(jnp.int32, bw.shape, 0)
    bw = jnp.where(row == 0, 0.0, bw)
    inv = 1.0 / (jnp.sum(jnp.abs(fw), axis=0, keepdims=True) + jnp.sum(jnp.abs(bw), axis=0, keepdims=True))
    hs_ref[...] = ((fw + bw) * inv).astype(BF16)
    hd_ref[...] = ((fw - bw) * inv).astype(BF16)


def _hyena_filter_sums(L, f1, b1, freq, f2, b2, f3):
    t = jnp.linspace(0.0, 1.0, L, dtype=F32)[:, None]
    w = (2.0 * math.pi / L) * jnp.arange(L, dtype=F32)[:, None]
    bands = jnp.linspace(1e-4, HY_BANDS - 1, HY_BANDS, dtype=F32)[None, :]
    z = jnp.concatenate([t, jnp.cos(bands * w), -jnp.sin(bands * w)], axis=-1)
    z = jnp.pad(z, ((0, 0), (0, LANE - HY_EMB)))
    f1p = jnp.pad(f1.astype(F32), ((0, LANE - HY_EMB), (0, 0)))
    ffn = f1.shape[1]
    deltas = jnp.abs(jnp.linspace(math.log(HY_TARGET) / HY_SLOW_DECAY, math.log(HY_TARGET) / HY_FAST_DECAY,
                                  HY_W, dtype=F32))
    NC = HY_ORDER * HY_W
    dl = jnp.tile(deltas, HY_ORDER).reshape(1, NC)
    tn = 256
    small = lambda shape: pl.BlockSpec(shape, lambda j: (0, 0))
    return pl.pallas_call(
        _filter_kernel,
        grid=(NC // tn,),
        in_specs=[small((L, LANE)), small((LANE, ffn)), small((1, ffn)), small((1, ffn)),
                  small((ffn, ffn)), small((1, ffn)),
                  pl.BlockSpec((ffn, tn), lambda j: (0, j)),
                  pl.BlockSpec((ffn, tn), lambda j: (0, NC // tn + j)),
                  pl.BlockSpec((1, tn), lambda j: (0, j))],
        out_specs=[pl.BlockSpec((L, tn), lambda j: (0, j))] * 2,
        out_shape=[jax.ShapeDtypeStruct((L, NC), BF16)] * 2,
        compiler_params=_cp("arbitrary"), name="hy_filter",
    )(z, f1p, b1.astype(F32).reshape(1, ffn), freq.astype(F32).reshape(1, ffn), f2.astype(F32),
      b2.astype(F32).reshape(1, ffn), f3.astype(F32), f3.astype(F32), dl)


def _dft_tables(L):
    N = 2 * L
    r = np.arange(N)
    k = (r // (2 * DFT_HALF)) * DFT_HALF + r % DFT_HALF
    is_im = (r // DFT_HALF) % 2 == 1
    nyq = is_im & (k == 0)
    kj = jnp.asarray(k, jnp.int32)[:, None]
    im, nyq = is_im[:, None], nyq[:, None]
    n_lo = 64

    def cos_sin(step, count):
        m = (kj * (step * jnp.arange(count, dtype=jnp.int32))[None, :]) % N
        ang = m.astype(F32) * (2.0 * math.pi / N)
        return jnp.cos(ang), jnp.sin(ang)

    c1, s1 = cos_sin(n_lo, L // n_lo)
    c0, s0 = cos_sin(1, n_lo)
    p1 = jnp.where(nyq, 1.0, jnp.where(im, -s1, c1))
    q1 = jnp.where(nyq, 0.0, jnp.where(im, -c1, -s1))
    p0 = jnp.where(nyq, (1 - 2 * (jnp.arange(n_lo) % 2)).astype(F32)[None, :], c0)
    a = (p1[:, :, None] * p0[:, None, :] + q1[:, :, None] * s0[:, None, :]).reshape(N, L).astype(BF16)
    return a, a.T


def _kf_kernel(a_ref, hs_ref, hd_ref, o_ref, *, n_fft):
    i = pl.program_id(0)
    H = DFT_HALF
    re = _dot(a_ref[:H, :], hs_ref[...])
    im = _dot(a_ref[H:, :], hd_ref[...])
    o_ref[:H, :] = re * (2.0 / n_fft)
    o_ref[H:, :] = im * (2.0 / n_fft)

    @pl.when(i == 0)
    def _():
        ny = _dot(a_ref[H:H + 16, :], hs_ref[...])
        o_ref[0:1, :] = re[0:1] * (1.0 / n_fft)
        o_ref[H:H + 1, :] = ny[0:1] * (1.0 / n_fft)


def _filter_spectrum(a, hs, hd):
    N, L = a.shape
    NC = hs.shape[1]
    tm, tn = 2 * DFT_HALF, 512
    return pl.pallas_call(
        functools.partial(_kf_kernel, n_fft=N),
        grid=(N // tm, NC // tn),
        in_specs=[pl.BlockSpec((tm, L), lambda i, j: (i, 0)),
                  pl.BlockSpec((L, tn), lambda i, j: (0, j)),
                  pl.BlockSpec((L, tn), lambda i, j: (0, j))],
        out_specs=pl.BlockSpec((tm, tn), lambda i, j: (i, j)),
        out_shape=jax.ShapeDtypeStruct((N, NC), F32),
        compiler_params=_cp("parallel", "arbitrary"), name="hy_filter_spectrum",
    )(a, hs, hd)


def _fwd_kernel(a_ref, z_ref, kf_ref, p_ref):
    i = pl.program_id(0)
    H = DFT_HALF
    acc = _dot(a_ref[...], z_ref[...])
    zr, zi = acc[:H], acc[H:]
    kr, ki = kf_ref[:H, :], kf_ref[H:, :]
    row = lax.broadcasted_iota(jnp.int32, zr.shape, 0)
    real_pair = (row == 0) & (i == 0)
    p_ref[:H, :] = jnp.where(real_pair, zr * kr, zr * kr - zi * ki).astype(BF16)
    p_ref[H:, :] = jnp.where(real_pair, zi * ki, zr * ki + zi * kr).astype(BF16)


def _dft_multiply(a, z, kf, order):
    N, L = a.shape
    NB = z.shape[1] // HY_W
    tm = 2 * DFT_HALF
    return pl.pallas_call(
        _fwd_kernel,
        grid=(N // tm, NB),
        in_specs=[pl.BlockSpec((tm, L), lambda i, j: (i, 0)),
                  pl.BlockSpec((L, HY_W), lambda i, j: (0, j)),
                  pl.BlockSpec((tm, HY_W), lambda i, j: (i, order))],
        out_specs=pl.BlockSpec((tm, HY_W), lambda i, j: (i, j)),
        out_shape=jax.ShapeDtypeStruct((N, NB * HY_W), BF16),
        compiler_params=_cp("parallel", "arbitrary"), name="hy_dft_multiply",
    )(a, z, kf)


def _inv_kernel(at_ref, p_ref, x_ref, z_ref, b_ref, o_ref):
    y = _dot(at_ref[...], p_ref[...])
    z = z_ref[...].astype(F32)
    o_ref[...] = (x_ref[...].astype(F32) * (y + b_ref[...] * z)).astype(o_ref.dtype)


def _idft_gate(at, p, xo, z, bias, batch_major):
    L, N = at.shape
    NB = z.shape[1] // HY_W
    tm = min(L, 512)
    if batch_major:
        out_spec = pl.BlockSpec((None, tm, HY_W), lambda i, j: (j, i, 0))
        out_shape = jax.ShapeDtypeStruct((NB, L, HY_W), BF16)
    else:
        out_spec = pl.BlockSpec((tm, HY_W), lambda i, j: (i, j))
        out_shape = jax.ShapeDtypeStruct((L, NB * HY_W), BF16)
    return pl.pallas_call(
        _inv_kernel,
        grid=(L // tm, NB),
        in_specs=[pl.BlockSpec((tm, N), lambda i, j: (i, 0)),
                  pl.BlockSpec((N, HY_W), lambda i, j: (0, j)),
                  pl.BlockSpec((tm, HY_W), lambda i, j: (i, j)),
                  pl.BlockSpec((tm, HY_W), lambda i, j: (i, j)),
                  pl.BlockSpec((1, HY_W), lambda i, j: (0, 0))],
        out_specs=out_spec, out_shape=out_shape,
        compiler_params=_cp("parallel", "arbitrary"), name="hy_idft_gate",
    )(at, p, xo, z, bias.astype(F32).reshape(1, HY_W))


def _hyena(vxx, tables, filt, hy_bias):
    a, at = tables
    kf = _filter_spectrum(a, *filt)
    z = vxx[0]
    for o in range(HY_ORDER):
        p = _dft_multiply(a, z, kf, o)
        z = _idft_gate(at, p, vxx[1 + o], z, hy_bias[o], batch_major=(o == HY_ORDER - 1))
    return z


def _merge_kernel(ya_ref, yb_ref, yc_ref, yd_ref, g0_ref, g1_ref, g2_ref, g3_ref, wb_ref, wo_ref, x_ref, gate_ref,
                  o_ref):
    ys = (ya_ref, yb_ref, yc_ref, yd_ref)
    gs = (g0_ref, g1_ref, g2_ref, g3_ref)
    acc = _dot(ys[0][...], wb_ref[0]) * gs[0][...].astype(F32)
    for n in range(1, N_BRANCH):
        acc += _dot(ys[n][...], wb_ref[n]) * gs[n][...].astype(F32)
    o_ref[...] = x_ref[...] + gate_ref[...] * _dot(acc.astype(BF16), wo_ref[...])


def _merge(ys, gates, wb, wo, l, x, modt, q_gate, tm, row_tile0, mod_row):
    B, R, D = x.shape
    row = lambda b, i: (b, row_tile0 + i, 0)
    in_specs = [pl.BlockSpec((None, tm, BRANCH_W), row)] * N_BRANCH
    in_specs += [pl.BlockSpec((None, tm, D), functools.partial(lambda b, i, n: (b, row_tile0 + i, n), n=n))
                 for n in range(N_BRANCH)]
    in_specs += [pl.BlockSpec((None, N_BRANCH, BRANCH_W, D), lambda b, i: (l, 0, 0, 0),
                              pipeline_mode=pl.Buffered(1)),
                 pl.BlockSpec((None, D, D), lambda b, i: (l, 0, 0), pipeline_mode=pl.Buffered(1)),
                 pl.BlockSpec((None, tm, D), lambda b, i: (b, i, 0)),
                 pl.BlockSpec((None, None, 1, D), lambda b, i: (mod_row(b), q_gate, 0, 0))]
    return pl.pallas_call(
        _merge_kernel,
        grid=(B, R // tm), in_specs=in_specs,
        out_specs=pl.BlockSpec((None, tm, D), lambda b, i: (b, i, 0)),
        out_shape=jax.ShapeDtypeStruct((B, R, D), F32),
        compiler_params=_cp("parallel", "arbitrary"), name="merge",
    )(*ys, gates, gates, gates, gates, wb, wo, x, modt)


def _expert_up_kernel(x_ref, w1_ref, w3_ref, o_ref):
    x = x_ref[...]
    a = _dot(x, w1_ref[...].astype(BF16))
    b = _dot(x, w3_ref[...].astype(BF16))
    o_ref[...] = (a * jax.nn.sigmoid(a) * b).astype(o_ref.dtype)


def _expert_down_kernel(h_ref, g_ref, w2_ref, o_ref):
    o_ref[...] = _dot(h_ref[...], w2_ref[...].astype(BF16)) * g_ref[...]


def _experts(xg, gate, w1, w3, w2, l):
    E, M, D = xg.shape
    F = w1.shape[3]
    tf = 256
    hmid = pl.pallas_call(
        _expert_up_kernel,
        grid=(E, F // tf),
        in_specs=[pl.BlockSpec((None, M, D), lambda e, f: (e, 0, 0)),
                  pl.BlockSpec((None, None, D, tf), lambda e, f: (l, e, 0, f)),
                  pl.BlockSpec((None, None, D, tf), lambda e, f: (l, e, 0, f))],
        out_specs=pl.BlockSpec((None, M, tf), lambda e, f: (e, 0, f)),
        out_shape=jax.ShapeDtypeStruct((E, M, F), BF16),
        compiler_params=_cp("parallel", "arbitrary"), name="expert_up",
    )(xg, w1, w3)
    return pl.pallas_call(
        _expert_down_kernel,
        grid=(E, D // tf),
        in_specs=[pl.BlockSpec((None, M, F), lambda e, n: (e, 0, 0)),
                  pl.BlockSpec((None, M, 1), lambda e, n: (e, 0, 0)),
                  pl.BlockSpec((None, None, F, tf), lambda e, n: (l, e, 0, n))],
        out_specs=pl.BlockSpec((None, M, tf), lambda e, n: (e, 0, n)),
        out_shape=jax.ShapeDtypeStruct((E, M, D), F32),
        compiler_params=_cp("parallel", "arbitrary"), name="expert_down",
    )(hmid, gate, w2)


def _moe(h2, aff, T, Lc, w1, w3, w2, l):
    B, S, D = h2.shape
    E = w1.shape[1]
    base = (jnp.arange(B, dtype=jnp.int32) * S)[:, None, None]

    def route(a, off):
        n = a.shape[1]
        gate, idx = lax.top_k(a.transpose(0, 2, 1), CAPACITY_FACTOR * n // E)
        rows = (idx + base + off).transpose(1, 0, 2).reshape(E, -1)
        return gate.transpose(1, 0, 2).reshape(E, -1), rows

    gate, rows = route(aff[:, :T, :E], 0)
    if Lc:
        gate_c, rows_c = route(aff[:, T:, :E], T)
        gate, rows = jnp.concatenate([gate, gate_c], axis=1), jnp.concatenate([rows, rows_c], axis=1)
    xg = h2.reshape(B * S, D)[rows]
    y = _experts(xg, gate[..., None], w1, w3, w2, l)
    return jnp.zeros((B * S, D), F32).at[rows.reshape(-1)].add(y.reshape(-1, D))


def kernel(x, c, ctx, c_ctx, w_mod, b_mod, norm1, norm2, w_in, hy_short, hy_f1, hy_b1, hy_freq, hy_f2, hy_b2, hy_f3, hy_bias, qn_b, kn_b, qn_c, kn_c, rpb_c, qn_d, kn_d, lam_q1, lam_k1, lam_q2, lam_k2, subln_d, w_branch, w_out, w_router, w_e1, w_e3, w_e2):
    B, T, D = x.shape
    Lc = ctx.shape[1]
    S = T + Lc
    depth = w_mod.shape[0]
    assert T % ROW_TILE == 0 and Lc == ROW_TILE and T // GRID_W >= NA_WIN_ROWS

    cc = jnp.concatenate([c, c_ctx[None, :], jnp.zeros((8 - B - 1, D), F32)], axis=0)
    rope_h = _rope_tables(B, T, S, HEAD_DIM)
    rope_d = _rope_tables(B, T, S, DF_QK_DIM)
    dft_lat = _dft_tables(T)
    dft_ctx = _dft_tables(Lc)
    w_in_bf, wb, wo = w_in.astype(BF16), w_branch.astype(BF16), w_out.astype(BF16)

    for l in range(depth):
        last = l == depth - 1
        n_ctx = 0 if last else 1
        Tq = T + n_ctx * ROW_TILE
        lam_init = 0.8 - 0.6 * math.exp(-0.3 * l)
        lam = (jnp.exp(jnp.sum(lam_q1[l].astype(F32) * lam_k1[l].astype(F32)))
               - jnp.exp(jnp.sum(lam_q2[l].astype(F32) * lam_k2[l].astype(F32))) + lam_init)
        modt = _modulation(cc, w_mod, b_mod, l).reshape(8, 6, 1, D)

        h = _prenorm(x, ctx, norm1[l], modt, 1, 0, 1)
        h2d = h.reshape(B * S, D)
        gains = _proj_gains(qn_b[l], kn_b[l], qn_c[l], kn_c[l], qn_d[l], kn_d[l])
        proj = _in_proj(h2d, w_in_bf, l, gains, rope_h, rope_d).reshape(B, S, C_GATES)
        gates = _matmul(h2d, w_in_bf, l, C_GATES, N_BRANCH * D, BF16, sigmoid=True,
                        name="in_proj_gates").reshape(B, S, N_BRANCH * D)

        y_b = _gqa(proj, T, Tq, ATT_SCALE)
        y_c = _na(proj, _na_bias(rpb_c[l], T), T, Tq)
        y_d = _diff_attn(lam, proj, subln_d[l], T, Tq, 1.0 - lam_init)

        conv = _shortconv(proj, hy_short[l].astype(F32), T, with_ctx=not last)
        filt = _hyena_filter_sums(T, hy_f1[l], hy_b1[l], hy_freq[l], hy_f2[l], hy_b2[l], hy_f3[l])
        y_a = _hyena(conv[:3], dft_lat, filt, hy_bias[l])
        if not last:
            filt_c = _hyena_filter_sums(Lc, hy_f1[l], hy_b1[l], hy_freq[l], hy_f2[l], hy_b2[l], hy_f3[l])
            y_a = jnp.concatenate([y_a, _hyena(conv[3:], dft_ctx, filt_c, hy_bias[l])], axis=1)

        ys = (y_a, y_b, y_c, y_d)
        x = _merge(ys, gates, wb, wo, l, x, modt, 2, ROW_TILE, 0, lambda b: b)
        if not last:
            ctx = _merge(ys, gates, wb, wo, l, ctx, modt, 2, ROW_TILE, T // ROW_TILE, lambda b: B)

        h2, aff = _prenorm(x, ctx, norm2[l], modt, 4, 3, n_ctx, w_router=w_router[l].astype(F32))
        moe = _moe(h2, aff, T, n_ctx * Lc, w_e1, w_e3, w_e2, l).reshape(B, Tq, D)
        g2 = modt[:, 5, 0, :]
        x = x + g2[:B, None, :] * moe[:, :T]
        if not last:
            ctx = ctx + g2[B][None, None, :] * moe[:, T:]
    return x
```

```python
import functools
import math

import numpy as np
import jax
import jax.numpy as jnp
from jax import lax
from jax.experimental import pallas as pl
from jax.experimental.pallas import tpu as pltpu

F32 = jnp.float32
BF16 = jnp.bfloat16

GRID_W = 64
HEAD_DIM = 128
BRANCH_W = 512
N_BRANCH = 4
HY_W = 512
HY_ORDER = 2
HY_BANDS = 16
HY_EMB = 1 + 2 * HY_BANDS
HY_FAST_DECAY = 0.3
HY_SLOW_DECAY = 1.5
HY_TARGET = 1e-2
GQA_HEADS = 4
GQA_KV = 2
NA_HEADS = 4
NA_WIN_H = 8
NA_WIN_W = 16
DF_HEADS = 4
DF_QK_DIM = 64
N_EXPERTS = 16
CAPACITY_FACTOR = 2
ROPE_THETA = 10000.0
NORM_EPS = 1e-6
ATT_SCALE = HEAD_DIM ** -0.5
DF_SCALE = DF_QK_DIM ** -0.5
NEG_INF = -1e30
LOG2E = math.log2(math.e)

LANE = 128
ROW_TILE = 256
PROJ_TILE_N = 512
NA_GROUP_ROWS = 4
NA_WIN_ROWS = NA_GROUP_ROWS + NA_WIN_H
DFT_HALF = 256
VMEM_LIMIT = 56 * 1024 * 1024

C_HY = 0
C_QB = 3 * HY_W
C_KB = C_QB + GQA_HEADS * HEAD_DIM
C_VB = C_KB + GQA_KV * HEAD_DIM
C_QC = C_VB + GQA_KV * HEAD_DIM
C_KC = C_QC + NA_HEADS * HEAD_DIM
C_VC = C_KC + NA_HEADS * HEAD_DIM
C_QD = C_VC + NA_HEADS * HEAD_DIM
C_KD = C_QD + DF_HEADS * 2 * DF_QK_DIM
C_VD = C_KD + DF_HEADS * 2 * DF_QK_DIM
C_GATES = C_VD + DF_HEADS * HEAD_DIM


def _cp(*sem):
    return pltpu.CompilerParams(dimension_semantics=sem, vmem_limit_bytes=VMEM_LIMIT)


def _dot(a, b):
    return jnp.dot(a, b, preferred_element_type=F32)


def _dot_nt(a, b):
    return lax.dot_general(a, b, (((1,), (1,)), ((), ())), preferred_element_type=F32)


def _dot_hi(a, b):
    return jnp.dot(a, b, preferred_element_type=F32, precision=lax.Precision.HIGHEST)


def _mod_kernel(c_ref, w_ref, b_ref, o_ref):
    c = c_ref[...]
    a = (c * jax.nn.sigmoid(c)).astype(BF16)
    o_ref[...] = _dot(a, w_ref[...].astype(BF16)) + b_ref[...]


def _modulation(cc, w, b, l):
    depth, D, N = w.shape
    tn = math.gcd(N, 1024)
    return pl.pallas_call(
        _mod_kernel,
        grid=(N // tn,),
        in_specs=[pl.BlockSpec((8, D), lambda j: (0, 0)),
                  pl.BlockSpec((None, D, tn), lambda j: (l, 0, j)),
                  pl.BlockSpec((None, 1, tn), lambda j: (l, 0, j))],
        out_specs=pl.BlockSpec((8, tn), lambda j: (0, j)),
        out_shape=jax.ShapeDtypeStruct((8, N), F32),
        compiler_params=_cp("arbitrary"),
        name="modulation",
    )(cc, w, b.reshape(depth, 1, N))


def _prenorm_body(x, g_ref, sc_ref, sh_ref):
    y = x * lax.rsqrt(jnp.mean(x * x, axis=-1, keepdims=True) + NORM_EPS)
    return y * g_ref[...] * (1.0 + sc_ref[...]) + sh_ref[...]


def _prenorm_kernel(x_ref, c_ref, g_ref, sc_ref, sh_ref, o_ref, *, n_lat):
    i = pl.program_id(1)

    @pl.when(i < n_lat)
    def _():
        o_ref[...] = _prenorm_body(x_ref[...], g_ref, sc_ref, sh_ref).astype(BF16)

    @pl.when(i >= n_lat)
    def _():
        o_ref[...] = _prenorm_body(c_ref[...], g_ref, sc_ref, sh_ref).astype(BF16)


def _prenorm_router_kernel(x_ref, c_ref, g_ref, sc_ref, sh_ref, wr_ref, o_ref, a_ref, *, n_lat, n_exp):
    i = pl.program_id(1)

    def run(x):
        h = _prenorm_body(x, g_ref, sc_ref, sh_ref)
        o_ref[...] = h.astype(BF16)
        logits = _dot_hi(h, wr_ref[...])
        lane = lax.broadcasted_iota(jnp.int32, logits.shape, 1)
        logits = jnp.where(lane < n_exp, logits, NEG_INF)
        e = jnp.exp(logits - jnp.max(logits, axis=-1, keepdims=True))
        a_ref[...] = e / jnp.sum(e, axis=-1, keepdims=True)

    @pl.when(i < n_lat)
    def _():
        run(x_ref[...])

    @pl.when(i >= n_lat)
    def _():
        run(c_ref[...])


def _prenorm(x, ctx, gain, modt, q_scale, q_shift, n_ctx_tiles, w_router=None):
    B, T, D = x.shape
    n_lat = T // ROW_TILE
    nt = n_lat + n_ctx_tiles
    S = nt * ROW_TILE

    def mod_map(q):
        return lambda b, i: (jnp.where(i < n_lat, b, B), q, 0, 0)

    in_specs = [
        pl.BlockSpec((None, ROW_TILE, D), lambda b, i: (b, jnp.minimum(i, n_lat - 1), 0)),
        pl.BlockSpec((None, ROW_TILE, D), lambda b, i: (b, jnp.maximum(i - n_lat, 0), 0)),
        pl.BlockSpec((1, D), lambda b, i: (0, 0)),
        pl.BlockSpec((None, None, 1, D), mod_map(q_scale)),
        pl.BlockSpec((None, None, 1, D), mod_map(q_shift)),
    ]
    out_h = pl.BlockSpec((None, ROW_TILE, D), lambda b, i: (b, i, 0))
    shape_h = jax.ShapeDtypeStruct((B, S, D), BF16)
    args = [x, ctx, gain.reshape(1, D), modt, modt]
    if w_router is None:
        return pl.pallas_call(
            functools.partial(_prenorm_kernel, n_lat=n_lat),
            grid=(B, nt), in_specs=in_specs, out_specs=out_h, out_shape=shape_h,
            compiler_params=_cp("parallel", "arbitrary"), name="prenorm",
        )(*args)
    n_exp = w_router.shape[1]
    wr = jnp.pad(w_router, ((0, 0), (0, LANE - n_exp)))
    return pl.pallas_call(
        functools.partial(_prenorm_router_kernel, n_lat=n_lat, n_exp=n_exp),
        grid=(B, nt),
        in_specs=in_specs + [pl.BlockSpec((D, LANE), lambda b, i: (0, 0))],
        out_specs=[out_h, pl.BlockSpec((None, ROW_TILE, LANE), lambda b, i: (b, i, 0))],
        out_shape=[shape_h, jax.ShapeDtypeStruct((B, S, LANE), F32)],
        compiler_params=_cp("parallel", "arbitrary"), name="prenorm_router",
    )(*args, wr)


def _mm_kernel(a_ref, b_ref, o_ref, *, sigmoid):
    acc = _dot(a_ref[...], b_ref[...].astype(BF16))
    if sigmoid:
        acc = jax.nn.sigmoid(acc)
    o_ref[...] = acc.astype(o_ref.dtype)


def _matmul(a, b, l, col0, N, out_dtype, sigmoid=False, name="matmul"):
    M, K = a.shape
    tm, tn = math.gcd(M, 1024), PROJ_TILE_N
    cb = col0 // tn
    return pl.pallas_call(
        functools.partial(_mm_kernel, sigmoid=sigmoid),
        grid=(M // tm, N // tn),
        in_specs=[pl.BlockSpec((tm, K), lambda i, j: (i, 0)),
                  pl.BlockSpec((None, K, tn), lambda i, j: (l, 0, cb + j))],
        out_specs=pl.BlockSpec((tm, tn), lambda i, j: (i, j)),
        out_shape=jax.ShapeDtypeStruct((M, N), out_dtype),
        compiler_params=_cp("parallel", "arbitrary"), name=name,
    )(a, b)


def _qk_post(x, g, seg, tables):
    lane = lax.broadcasted_iota(jnp.int32, (1, LANE), 1)
    sq = x * x
    if seg == LANE:
        ms = jnp.mean(sq, axis=-1, keepdims=True)
    else:
        lo = jnp.sum(jnp.where(lane < seg, sq, 0.0), axis=-1, keepdims=True)
        hi = jnp.sum(jnp.where(lane >= seg, sq, 0.0), axis=-1, keepdims=True)
        ms = jnp.where(lane < seg, lo, hi) * (1.0 / seg)
    y = x * lax.rsqrt(ms + NORM_EPS) * g
    if tables is not None:
        cos_ref, sin_ref = tables
        q = seg // 4
        partner = jnp.where((lane % (seg // 2)) < q, pltpu.roll(y, LANE - q, 1), pltpu.roll(y, q, 1))
        y = y * cos_ref[...] + partner * sin_ref[...]
    return y


def _proj_block_kinds():
    kinds = []
    for col in range(0, C_GATES, LANE):
        if C_QB <= col < C_VB:
            kinds.append((HEAD_DIM, True))
        elif C_QC <= col < C_VC:
            kinds.append((HEAD_DIM, False))
        elif C_QD <= col < C_VD:
            kinds.append((DF_QK_DIM, True))
        else:
            kinds.append(None)
    return kinds


def _inproj_kernel(a_ref, w_ref, g_ref, c128_ref, s128_ref, c64_ref, s64_ref, o_ref):
    j = pl.program_id(1)
    acc = _dot(a_ref[...], w_ref[...].astype(BF16))
    per_tile = PROJ_TILE_N // LANE
    kinds = _proj_block_kinds()
    tiles = [kinds[t * per_tile:(t + 1) * per_tile] for t in range(len(kinds) // per_tile)]
    plain = functools.reduce(jnp.logical_or, [j == t for t, ks in enumerate(tiles) if not any(ks)])

    @pl.when(plain)
    def _():
        o_ref[...] = acc.astype(o_ref.dtype)

    for t, ks in enumerate(tiles):
        if not any(ks):
            continue

        @pl.when(j == t)
        def _(ks=ks):
            for h, kind in enumerate(ks):
                y = acc[:, h * LANE:(h + 1) * LANE]
                if kind is not None:
                    seg, rope = kind
                    tables = None if not rope else ((c128_ref, s128_ref) if seg == HEAD_DIM else (c64_ref, s64_ref))
                    y = _qk_post(y, g_ref[h:h + 1, :], seg, tables)
                o_ref[:, h * LANE:(h + 1) * LANE] = y.astype(o_ref.dtype)


def _in_proj(h2d, w_in_bf, l, gains, rope_h, rope_d):
    M, K = h2d.shape
    tm, tn = math.gcd(M, 1024), PROJ_TILE_N
    per_tile = tn // LANE
    table = pl.BlockSpec((tm, LANE), lambda i, j: (i, 0))
    return pl.pallas_call(
        _inproj_kernel,
        grid=(M // tm, C_GATES // tn),
        in_specs=[pl.BlockSpec((tm, K), lambda i, j: (i, 0)),
                  pl.BlockSpec((None, K, tn), lambda i, j: (l, 0, j)),
                  pl.BlockSpec((None, per_tile, LANE), lambda i, j: (j, 0, 0)),
                  table, table, table, table],
        out_specs=pl.BlockSpec((tm, tn), lambda i, j: (i, j)),
        out_shape=jax.ShapeDtypeStruct((M, C_GATES), BF16),
        compiler_params=_cp("parallel", "arbitrary"), name="in_proj",
    )(h2d, w_in_bf, gains.reshape(-1, per_tile, LANE), *rope_h, *rope_d)


def _proj_gains(qn_b, kn_b, qn_c, kn_c, qn_d, kn_d):
    ones = lambda n: jnp.ones((n, LANE), F32)
    rep = lambda g, n: jnp.tile(jnp.tile(g.astype(F32), LANE // g.shape[0])[None, :], (n, 1))
    return jnp.concatenate([
        ones(C_QB // LANE), rep(qn_b, GQA_HEADS), rep(kn_b, GQA_KV), ones(GQA_KV),
        rep(qn_c, NA_HEADS), rep(kn_c, NA_HEADS), ones(NA_HEADS),
        rep(qn_d, DF_HEADS), rep(kn_d, DF_HEADS), ones(DF_HEADS)], axis=0)


def _rope_tables(B, T, S, seg):
    half = seg // 2
    nfreq = half // 2
    inv = ROPE_THETA ** (-jnp.arange(0, half, 2, dtype=F32) / half)
    pos = jnp.arange(T, dtype=jnp.int32)
    rows, cols = (pos // GRID_W).astype(F32), (pos % GRID_W).astype(F32)
    l = np.arange(LANE) % seg
    use_col = l >= half
    fidx = (l % half) % nfreq
    is_b = (l % half) >= nfreq
    ang = jnp.where(use_col[None, :], cols[:, None], rows[:, None]) * inv[fidx][None, :]
    cos, sin = jnp.cos(ang), jnp.sin(ang)
    sin = jnp.where(is_b[None, :], sin, -sin)
    cos = jnp.concatenate([cos, jnp.ones((S - T, LANE), F32)], axis=0)
    sin = jnp.concatenate([sin, jnp.zeros((S - T, LANE), F32)], axis=0)
    return jnp.tile(cos, (B, 1)), jnp.tile(sin, (B, 1))


def _softmax_rows(s, scale):
    c = scale * LOG2E
    m = jnp.max(s, axis=-1, keepdims=True)
    p = jnp.exp2(s * c - m * c)
    return p.astype(BF16), jnp.sum(p, axis=-1, keepdims=True)


def _attend(qs, k, v, scale):
    scores = [_dot_nt(q, k) for q in qs]
    outs = []
    for s in scores:
        p, l = _softmax_rows(s, scale)
        outs.append(_dot(p, v) / l)
    return outs


def _gqa_kernel(q_ref, k_ref, v_ref, o_ref, *, T, R, scale):
    i = pl.program_id(2)

    def attend(k, v):
        outs = _attend([q_ref[:, r * LANE:(r + 1) * LANE] for r in range(R)], k, v, scale)
        for r in range(R):
            o_ref[:, r * LANE:(r + 1) * LANE] = outs[r].astype(o_ref.dtype)

    @pl.when(i < T // ROW_TILE)
    def _():
        attend(k_ref[...], v_ref[...])

    @pl.when(i >= T // ROW_TILE)
    def _():
        attend(k_ref[T:, :], v_ref[T:, :])


def _gqa(proj, T, Tq, scale):
    B, S, _ = proj.shape
    R = GQA_HEADS // GQA_KV
    qb, kb, vb = C_QB // (R * LANE), C_KB // LANE, C_VB // LANE
    return pl.pallas_call(
        functools.partial(_gqa_kernel, T=T, R=R, scale=scale),
        grid=(B, GQA_KV, Tq // ROW_TILE),
        in_specs=[pl.BlockSpec((None, ROW_TILE, R * LANE), lambda b, g, i: (b, i, qb + g)),
                  pl.BlockSpec((None, S, LANE), lambda b, g, i: (b, 0, kb + g)),
                  pl.BlockSpec((None, S, LANE), lambda b, g, i: (b, 0, vb + g))],
        out_specs=pl.BlockSpec((None, ROW_TILE, R * LANE), lambda b, g, i: (b, i, g)),
        out_shape=jax.ShapeDtypeStruct((B, Tq, GQA_HEADS * LANE), BF16),
        compiler_params=_cp("parallel", "parallel", "arbitrary"), name="gqa",
    )(proj, proj, proj)


def _diff_kernel(lam_ref, q_ref, k_ref, v_ref, g_ref, o_ref, *, T, scale, out_scale):
    i = pl.program_id(2)
    lam = lam_ref[0]

    def attend(k, v):
        q = q_ref[...]
        lane = lax.broadcasted_iota(jnp.int32, q.shape, 1)
        zero = jnp.zeros_like(q)
        o1, o2 = _attend([jnp.where(lane < DF_QK_DIM, q, zero), jnp.where(lane >= DF_QK_DIM, q, zero)],
                         k, v, scale)
        o = o1 - lam * o2
        y = o * lax.rsqrt(jnp.mean(o * o, axis=-1, keepdims=True) + NORM_EPS) * g_ref[...]
        o_ref[...] = (y * out_scale).astype(o_ref.dtype)

    @pl.when(i < T // ROW_TILE)
    def _():
        attend(k_ref[...], v_ref[...])

    @pl.when(i >= T // ROW_TILE)
    def _():
        attend(k_ref[T:, :], v_ref[T:, :])


def _diff_attn(lam, proj, subln, T, Tq, out_scale):
    B, S, _ = proj.shape
    qb, kb, vb = C_QD // LANE, C_KD // LANE, C_VD // LANE
    return pl.pallas_call(
        functools.partial(_diff_kernel, T=T, scale=DF_SCALE, out_scale=out_scale),
        grid=(B, DF_HEADS, Tq // ROW_TILE),
        in_specs=[pl.BlockSpec(memory_space=pltpu.SMEM),
                  pl.BlockSpec((None, ROW_TILE, LANE), lambda b, h, i: (b, i, qb + h)),
                  pl.BlockSpec((None, S, LANE), lambda b, h, i: (b, 0, kb + h)),
                  pl.BlockSpec((None, S, LANE), lambda b, h, i: (b, 0, vb + h)),
                  pl.BlockSpec((1, LANE), lambda b, h, i: (0, 0))],
        out_specs=pl.BlockSpec((None, ROW_TILE, LANE), lambda b, h, i: (b, i, h)),
        out_shape=jax.ShapeDtypeStruct((B, Tq, DF_HEADS * LANE), BF16),
        compiler_params=_cp("parallel", "parallel", "arbitrary"), name="diff_attn",
    )(lam.reshape(1).astype(F32), proj, proj, proj, subln.astype(F32).reshape(1, LANE))


def _na_kernel(q_ref, k_ref, v_ref, bias_ref, o_ref, *, T, scale):
    i = pl.program_id(2)
    n_groups = T // ROW_TILE
    grid_rows = T // GRID_W
    q = q_ref[...]
    kc, vc = k_ref[T:, :], v_ref[T:, :]
    s_c = _dot_nt(q, kc)

    @pl.when(i < n_groups)
    def _():
        row0 = jnp.clip(i * NA_GROUP_ROWS - NA_WIN_H // 2, 0, grid_rows - NA_WIN_ROWS)
        start = pl.multiple_of(row0 * GRID_W, ROW_TILE)
        kw = k_ref[pl.ds(start, NA_WIN_ROWS * GRID_W), :]
        vw = v_ref[pl.ds(start, NA_WIN_ROWS * GRID_W), :]
        c = scale * LOG2E
        u_n = _dot_nt(q, kw) * c + bias_ref[...]
        u_c = s_c * c
        m = jnp.maximum(jnp.max(u_n, axis=-1, keepdims=True), jnp.max(u_c, axis=-1, keepdims=True))
        p_n, p_c = jnp.exp2(u_n - m), jnp.exp2(u_c - m)
        l = jnp.sum(p_n, axis=-1, keepdims=True) + jnp.sum(p_c, axis=-1, keepdims=True)
        o = (_dot(p_n.astype(BF16), vw) + _dot(p_c.astype(BF16), vc)) / l
        o_ref[...] = o.astype(o_ref.dtype)

    @pl.when(i >= n_groups)
    def _():
        p, l = _softmax_rows(s_c, scale)
        o_ref[...] = (_dot(p, vc) / l).astype(o_ref.dtype)


def _na_bias(rpb, T):
    rows = T // GRID_W
    n_groups = rows // NA_GROUP_ROWS
    n_roff, n_coff = 2 * NA_WIN_H - 1, 2 * NA_WIN_W - 1
    c = np.arange(GRID_W)[:, None]
    kc = np.arange(GRID_W)[None, :]
    cs = np.clip(c - NA_WIN_W // 2, 0, GRID_W - NA_WIN_W)
    col_valid = (kc >= cs) & (kc < cs + NA_WIN_W)
    coff = np.clip(kc - c + NA_WIN_W - 1, 0, n_coff - 1)
    col_sel = (coff[..., None] == np.arange(n_coff)).astype(np.float32)
    rpb = rpb.astype(F32) * LOG2E
    out = []
    for grp in (0, 1, n_groups - 1):
        r = grp * NA_GROUP_ROWS + np.arange(NA_GROUP_ROWS)[:, None]
        rs = np.clip(r - NA_WIN_H // 2, 0, rows - NA_WIN_H)
        row0 = np.clip(grp * NA_GROUP_ROWS - NA_WIN_H // 2, 0, rows - NA_WIN_ROWS)
        key_row = row0 + np.arange(NA_WIN_ROWS)[None, :]
        row_valid = (key_row >= rs) & (key_row < rs + NA_WIN_H)
        roff = np.clip(key_row - r + NA_WIN_H - 1, 0, n_roff - 1)
        row_sel = (roff[..., None] == np.arange(n_roff)).astype(np.float32)
        bias = jnp.einsum('ika,hab,cqb->hickq', row_sel, rpb, col_sel, precision=lax.Precision.HIGHEST)
        valid = row_valid[:, None, :, None] & col_valid[None, :, None, :]
        out.append(jnp.where(valid[None], bias, NEG_INF).reshape(rpb.shape[0], ROW_TILE, -1))
    return jnp.stack(out)


def _na(proj, bias, T, Tq):
    B, S, _ = proj.shape
    qb, kb, vb = C_QC // LANE, C_KC // LANE, C_VC // LANE
    n_groups = T // ROW_TILE
    nk = NA_WIN_ROWS * GRID_W

    def bias_map(b, h, i):
        return (jnp.where(i == 0, 0, jnp.where(i >= n_groups - 1, 2, 1)), h, 0, 0)

    return pl.pallas_call(
        functools.partial(_na_kernel, T=T, scale=ATT_SCALE),
        grid=(B, NA_HEADS, Tq // ROW_TILE),
        in_specs=[pl.BlockSpec((None, ROW_TILE, LANE), lambda b, h, i: (b, i, qb + h)),
                  pl.BlockSpec((None, S, LANE), lambda b, h, i: (b, 0, kb + h)),
                  pl.BlockSpec((None, S, LANE), lambda b, h, i: (b, 0, vb + h)),
                  pl.BlockSpec((None, None, ROW_TILE, nk), bias_map)],
        out_specs=pl.BlockSpec((None, ROW_TILE, LANE), lambda b, h, i: (b, i, h)),
        out_shape=jax.ShapeDtypeStruct((B, Tq, NA_HEADS * LANE), BF16),
        compiler_params=_cp("parallel", "parallel", "arbitrary"), name="na_attn",
    )(proj, proj, proj, bias)


def _shortconv_kernel(*refs, T, with_ctx):
    u_refs, w_refs, o_refs = refs[0:3], refs[3:6], refs[6:]
    S = u_refs[0].shape[0]
    row = lax.broadcasted_iota(jnp.int32, (S, 1), 0)
    first = (row == 0) | (row == T)
    last = (row == T - 1) | (row == S - 1)
    for n in range(3):
        u = u_refs[n][...].astype(F32)
        w = w_refs[n][...]
        prev = jnp.where(first, 0.0, pltpu.roll(u, 1, 0))
        nxt = jnp.where(last, 0.0, pltpu.roll(u, S - 1, 0))
        y = (prev * w[0:1] + u * w[1:2] + nxt * w[2:3]).astype(BF16)
        o_refs[n][...] = y[:T]
        if with_ctx:
            o_refs[3 + n][...] = y[T:]


def _shortconv(proj, w, T, with_ctx):
    B, S, _ = proj.shape
    Lc = S - T
    nct = HY_W // LANE
    in_specs = [pl.BlockSpec((None, S, LANE), functools.partial(lambda b, c, n: (b, 0, n * nct + c), n=n))
                for n in range(3)]
    in_specs += [pl.BlockSpec((3, LANE), functools.partial(lambda b, c, n: (0, n * nct + c), n=n))
                 for n in range(3)]
    out_specs = [pl.BlockSpec((T, LANE), lambda b, c: (0, b * nct + c))] * 3
    out_shape = [jax.ShapeDtypeStruct((T, B * HY_W), BF16)] * 3
    if with_ctx:
        out_specs += [pl.BlockSpec((Lc, LANE), lambda b, c: (0, b * nct + c))] * 3
        out_shape += [jax.ShapeDtypeStruct((Lc, B * HY_W), BF16)] * 3
    return pl.pallas_call(
        functools.partial(_shortconv_kernel, T=T, with_ctx=with_ctx),
        grid=(B, nct), in_specs=in_specs, out_specs=out_specs, out_shape=out_shape,
        compiler_params=_cp("parallel", "arbitrary"), name="hy_shortconv",
    )(proj, proj, proj, w, w, w)


def _filter_kernel(z_ref, f1_ref, b1_ref, fr_ref, f2_ref, b2_ref, f3f_ref, f3b_ref, dl_ref, hs_ref, hd_ref):
    z = z_ref[...]
    fr = fr_ref[...]
    hid = jnp.sin(fr * (_dot_hi(z, f1_ref[...]) + b1_ref[...]))
    hid = jnp.sin(fr * (_dot_hi(hid, f2_ref[...]) + b2_ref[...]))
    decay = jnp.exp(-z[:, 0:1] * dl_ref[...])
    fw = _dot_hi(hid, f3f_ref[...]) * decay
    bw = _dot_hi(hid, f3b_ref[...]) * decay
    row = lax.broadcasted_iota(jnp.int32, bw.shape, 0)
    bw = jnp.where(row == 0, 0.0, bw)
    inv = 1.0 / (jnp.sum(jnp.abs(fw), axis=0, keepdims=True) + jnp.sum(jnp.abs(bw), axis=0, keepdims=True))
    hs_ref[...] = ((fw + bw) * inv).astype(BF16)
    hd_ref[...] = ((fw - bw) * inv).astype(BF16)


def _hyena_filter_sums(L, f1, b1, freq, f2, b2, f3):
    t = jnp.linspace(0.0, 1.0, L, dtype=F32)[:, None]
    w = (2.0 * math.pi / L) * jnp.arange(L, dtype=F32)[:, None]
    bands = jnp.linspace(1e-4, HY_BANDS - 1, HY_BANDS, dtype=F32)[None, :]
    z = jnp.concatenate([t, jnp.cos(bands * w), -jnp.sin(bands * w)], axis=-1)
    z = jnp.pad(z, ((0, 0), (0, LANE - HY_EMB)))
    f1p = jnp.pad(f1.astype(F32), ((0, LANE - HY_EMB), (0, 0)))
    ffn = f1.shape[1]
    deltas = jnp.abs(jnp.linspace(math.log(HY_TARGET) / HY_SLOW_DECAY, math.log(HY_TARGET) / HY_FAST_DECAY,
                                  HY_W, dtype=F32))
    NC = HY_ORDER * HY_W
    dl = jnp.tile(deltas, HY_ORDER).reshape(1, NC)
    tn = 256
    small = lambda shape: pl.BlockSpec(shape, lambda j: (0, 0))
    return pl.pallas_call(
        _filter_kernel,
        grid=(NC // tn,),
        in_specs=[small((L, LANE)), small((LANE, ffn)), small((1, ffn)), small((1, ffn)),
                  small((ffn, ffn)), small((1, ffn)),
                  pl.BlockSpec((ffn, tn), lambda j: (0, j)),
                  pl.BlockSpec((ffn, tn), lambda j: (0, NC // tn + j)),
                  pl.BlockSpec((1, tn), lambda j: (0, j))],
        out_specs=[pl.BlockSpec((L, tn), lambda j: (0, j))] * 2,
        out_shape=[jax.ShapeDtypeStruct((L, NC), BF16)] * 2,
        compiler_params=_cp("arbitrary"), name="hy_filter",
    )(z, f1p, b1.astype(F32).reshape(1, ffn), freq.astype(F32).reshape(1, ffn), f2.astype(F32),
      b2.astype(F32).reshape(1, ffn), f3.astype(F32), f3.astype(F32), dl)


def _dft_tables(L):
    N = 2 * L
    r = np.arange(N)
    k = (r // (2 * DFT_HALF)) * DFT_HALF + r % DFT_HALF
    is_im = (r // DFT_HALF) % 2 == 1
    nyq = is_im & (k == 0)
    kj = jnp.asarray(k, jnp.int32)[:, None]
    im, nyq = is_im[:, None], nyq[:, None]
    n_lo = 64

    def cos_sin(step, count):
        m = (kj * (step * jnp.arange(count, dtype=jnp.int32))[None, :]) % N
        ang = m.astype(F32) * (2.0 * math.pi / N)
        return jnp.cos(ang), jnp.sin(ang)

    c1, s1 = cos_sin(n_lo, L // n_lo)
    c0, s0 = cos_sin(1, n_lo)
    p1 = jnp.where(nyq, 1.0, jnp.where(im, -s1, c1))
    q1 = jnp.where(nyq, 0.0, jnp.where(im, -c1, -s1))
    p0 = jnp.where(nyq, (1 - 2 * (jnp.arange(n_lo) % 2)).astype(F32)[None, :], c0)
    a = (p1[:, :, None] * p0[:, None, :] + q1[:, :, None] * s0[:, None, :]).reshape(N, L).astype(BF16)
    return a, a.T


def _kf_kernel(a_ref, hs_ref, hd_ref, o_ref, *, n_fft):
    i = pl.program_id(0)
    H = DFT_HALF
    re = _dot(a_ref[:H, :], hs_ref[...])
    im = _dot(a_ref[H:, :], hd_ref[...])
    o_ref[:H, :] = re * (2.0 / n_fft)
    o_ref[H:, :] = im * (2.0 / n_fft)

    @pl.when(i == 0)
    def _():
        ny = _dot(a_ref[H:H + 16, :], hs_ref[...])
        o_ref[0:1, :] = re[0:1] * (1.0 / n_fft)
        o_ref[H:H + 1, :] = ny[0:1] * (1.0 / n_fft)


def _filter_spectrum(a, hs, hd):
    N, L = a.shape
    NC = hs.shape[1]
    tm, tn = 2 * DFT_HALF, 512
    return pl.pallas_call(
        functools.partial(_kf_kernel, n_fft=N),
        grid=(N // tm, NC // tn),
        in_specs=[pl.BlockSpec((tm, L), lambda i, j: (i, 0)),
                  pl.BlockSpec((L, tn), lambda i, j: (0, j)),
                  pl.BlockSpec((L, tn), lambda i, j: (0, j))],
        out_specs=pl.BlockSpec((tm, tn), lambda i, j: (i, j)),
        out_shape=jax.ShapeDtypeStruct((N, NC), F32),
        compiler_params=_cp("parallel", "arbitrary"), name="hy_filter_spectrum",
    )(a, hs, hd)


def _fwd_kernel(a_ref, z_ref, kf_ref, p_ref):
    i = pl.program_id(0)
    H = DFT_HALF
    acc = _dot(a_ref[...], z_ref[...])
    zr, zi = acc[:H], acc[H:]
    kr, ki = kf_ref[:H, :], kf_ref[H:, :]
    row = lax.broadcasted_iota(jnp.int32, zr.shape, 0)
    real_pair = (row == 0) & (i == 0)
    p_ref[:H, :] = jnp.where(real_pair, zr * kr, zr * kr - zi * ki).astype(BF16)
    p_ref[H:, :] = jnp.where(real_pair, zi * ki, zr * ki + zi * kr).astype(BF16)


def _dft_multiply(a, z, kf, order):
    N, L = a.shape
    NB = z.shape[1] // HY_W
    tm = 2 * DFT_HALF
    return pl.pallas_call(
        _fwd_kernel,
        grid=(N // tm, NB),
        in_specs=[pl.BlockSpec((tm, L), lambda i, j: (i, 0)),
                  pl.BlockSpec((L, HY_W), lambda i, j: (0, j)),
                  pl.BlockSpec((tm, HY_W), lambda i, j: (i, order))],
        out_specs=pl.BlockSpec((tm, HY_W), lambda i, j: (i, j)),
        out_shape=jax.ShapeDtypeStruct((N, NB * HY_W), BF16),
        compiler_params=_cp("parallel", "arbitrary"), name="hy_dft_multiply",
    )(a, z, kf)


def _inv_kernel(at_ref, p_ref, x_ref, z_ref, b_ref, o_ref):
    y = _dot(at_ref[...], p_ref[...])
    z = z_ref[...].astype(F32)
    o_ref[...] = (x_ref[...].astype(F32) * (y + b_ref[...] * z)).astype(o_ref.dtype)


def _idft_gate(at, p, xo, z, bias, batch_major):
    L, N = at.shape
    NB = z.shape[1] // HY_W
    tm = min(L, 512)
    if batch_major:
        out_spec = pl.BlockSpec((None, tm, HY_W), lambda i, j: (j, i, 0))
        out_shape = jax.ShapeDtypeStruct((NB, L, HY_W), BF16)
    else:
        out_spec = pl.BlockSpec((tm, HY_W), lambda i, j: (i, j))
        out_shape = jax.ShapeDtypeStruct((L, NB * HY_W), BF16)
    return pl.pallas_call(
        _inv_kernel,
        grid=(L // tm, NB),
        in_specs=[pl.BlockSpec((tm, N), lambda i, j: (i, 0)),
                  pl.BlockSpec((N, HY_W), lambda i, j: (0, j)),
                  pl.BlockSpec((tm, HY_W), lambda i, j: (i, j)),
                  pl.BlockSpec((tm, HY_W), lambda i, j: (i, j)),
                  pl.BlockSpec((1, HY_W), lambda i, j: (0, 0))],
        out_specs=out_spec, out_shape=out_shape,
        compiler_params=_cp("parallel", "arbitrary"), name="hy_idft_gate",
    )(at, p, xo, z, bias.astype(F32).reshape(1, HY_W))


def _hyena(vxx, tables, filt, hy_bias):
    a, at = tables
    kf = _filter_spectrum(a, *filt)
    z = vxx[0]
    for o in range(HY_ORDER):
        p = _dft_multiply(a, z, kf, o)
        z = _idft_gate(at, p, vxx[1 + o], z, hy_bias[o], batch_major=(o == HY_ORDER - 1))
    return z


def _merge_kernel(ya_ref, yb_ref, yc_ref, yd_ref, g0_ref, g1_ref, g2_ref, g3_ref, wb_ref, wo_ref, x_ref, gate_ref,
                  o_ref):
    ys = (ya_ref, yb_ref, yc_ref, yd_ref)
    gs = (g0_ref, g1_ref, g2_ref, g3_ref)
    acc = _dot(ys[0][...], wb_ref[0]) * gs[0][...].astype(F32)
    for n in range(1, N_BRANCH):
        acc += _dot(ys[n][...], wb_ref[n]) * gs[n][...].astype(F32)
    o_ref[...] = x_ref[...] + gate_ref[...] * _dot(acc.astype(BF16), wo_ref[...])


def _merge(ys, gates, wb, wo, l, x, modt, q_gate, tm, row_tile0, mod_row):
    B, R, D = x.shape
    row = lambda b, i: (b, row_tile0 + i, 0)
    in_specs = [pl.BlockSpec((None, tm, BRANCH_W), row)] * N_BRANCH
    in_specs += [pl.BlockSpec((None, tm, D), functools.partial(lambda b, i, n: (b, row_tile0 + i, n), n=n))
                 for n in range(N_BRANCH)]
    in_specs += [pl.BlockSpec((None, N_BRANCH, BRANCH_W, D), lambda b, i: (l, 0, 0, 0),
                              pipeline_mode=pl.Buffered(1)),
                 pl.BlockSpec((None, D, D), lambda b, i: (l, 0, 0), pipeline_mode=pl.Buffered(1)),
                 pl.BlockSpec((None, tm, D), lambda b, i: (b, i, 0)),
                 pl.BlockSpec((None, None, 1, D), lambda b, i: (mod_row(b), q_gate, 0, 0))]
    return pl.pallas_call(
        _merge_kernel,
        grid=(B, R // tm), in_specs=in_specs,
        out_specs=pl.BlockSpec((None, tm, D), lambda b, i: (b, i, 0)),
        out_shape=jax.ShapeDtypeStruct((B, R, D), F32),
        compiler_params=_cp("parallel", "arbitrary"), name="merge",
    )(*ys, gates, gates, gates, gates, wb, wo, x, modt)


def _expert_up_kernel(x_ref, w1_ref, w3_ref, o_ref):
    x = x_ref[...]
    a = _dot(x, w1_ref[...].astype(BF16))
    b = _dot(x, w3_ref[...].astype(BF16))
    o_ref[...] = (a * jax.nn.sigmoid(a) * b).astype(o_ref.dtype)


def _expert_down_kernel(h_ref, g_ref, w2_ref, o_ref):
    o_ref[...] = (_dot(h_ref[...], w2_ref[...].astype(BF16)) * g_ref[...]).astype(o_ref.dtype)


def _experts(xg, gate, w1, w3, w2, l):
    E, M, D = xg.shape
    F = w1.shape[3]
    tf = 256
    hmid = pl.pallas_call(
        _expert_up_kernel,
        grid=(E, F // tf),
        in_specs=[pl.BlockSpec((None, M, D), lambda e, f: (e, 0, 0)),
                  pl.BlockSpec((None, None, D, tf), lambda e, f: (l, e, 0, f)),
                  pl.BlockSpec((None, None, D, tf), lambda e, f: (l, e, 0, f))],
        out_specs=pl.BlockSpec((None, M, tf), lambda e, f: (e, 0, f)),
        out_shape=jax.ShapeDtypeStruct((E, M, F), BF16),
        compiler_params=_cp("parallel", "arbitrary"), name="expert_up",
    )(xg, w1, w3)
    return pl.pallas_call(
        _expert_down_kernel,
        grid=(E, D // tf),
        in_specs=[pl.BlockSpec((None, M, F), lambda e, n: (e, 0, 0)),
                  pl.BlockSpec((None, M, 1), lambda e, n: (e, 0, 0)),
                  pl.BlockSpec((None, None, F, tf), lambda e, n: (l, e, 0, n))],
        out_specs=pl.BlockSpec((None, M, tf), lambda e, n: (e, 0, n)),
        out_shape=jax.ShapeDtypeStruct((E, M, D), BF16),
        compiler_params=_cp("parallel", "arbitrary"), name="expert_down",
    )(hmid, gate, w2)


def _combine_kernel(tok_ref, y_ref, x_ref, g_ref, o_ref, *, row0):
    E, Ct, tn = y_ref.shape
    rowid = row0 + pl.program_id(2) * ROW_TILE + lax.broadcasted_iota(jnp.int32, (ROW_TILE, 1), 0)
    onehot = jnp.where(tok_ref[...] == rowid, 1.0, 0.0).astype(BF16)
    o_ref[...] = x_ref[...] + g_ref[...] * _dot(onehot, y_ref[...].reshape(E * Ct, tn))


def _combine(tok, y, x, modt, q_gate, row0, mod_row):
    B, R, D = x.shape
    E, _, Ct, _ = y.shape
    tn = math.gcd(D, PROJ_TILE_N)
    return pl.pallas_call(
        functools.partial(_combine_kernel, row0=row0),
        grid=(B, D // tn, R // ROW_TILE),
        in_specs=[pl.BlockSpec((None, 1, E * Ct), lambda b, n, t: (b, 0, 0)),
                  pl.BlockSpec((E, None, Ct, tn), lambda b, n, t: (0, b, 0, n)),
                  pl.BlockSpec((None, ROW_TILE, tn), lambda b, n, t: (b, t, n)),
                  pl.BlockSpec((None, None, 1, tn), lambda b, n, t: (mod_row(b), q_gate, 0, n))],
        out_specs=pl.BlockSpec((None, ROW_TILE, tn), lambda b, n, t: (b, t, n)),
        out_shape=jax.ShapeDtypeStruct((B, R, D), F32),
        compiler_params=_cp("parallel", "parallel", "arbitrary"), name="moe_combine",
    )(tok, y, x, modt)


def _moe(h2, aff, T, Lc, w1, w3, w2, l):
    B, S, D = h2.shape
    E = w1.shape[1]

    def route(a):
        n = a.shape[1]
        return lax.top_k(a.transpose(0, 2, 1), CAPACITY_FACTOR * n // E)

    gate, tok = route(aff[:, :T, :E])
    if Lc:
        gate_c, tok_c = route(aff[:, T:, :E])
        gate, tok = jnp.concatenate([gate, gate_c], axis=-1), jnp.concatenate([tok, T + tok_c], axis=-1)
    Ct = tok.shape[-1]
    rows = (tok + (jnp.arange(B, dtype=jnp.int32) * S)[:, None, None]).transpose(1, 0, 2).reshape(E, B * Ct)
    xg = h2.reshape(B * S, D)[rows]
    y = _experts(xg, gate.transpose(1, 0, 2).reshape(E, B * Ct, 1), w1, w3, w2, l)
    return tok.reshape(B, 1, E * Ct), y.reshape(E, B, Ct, D)


def kernel(x, c, ctx, c_ctx, w_mod, b_mod, norm1, norm2, w_in, hy_short, hy_f1, hy_b1, hy_freq, hy_f2, hy_b2, hy_f3, hy_bias, qn_b, kn_b, qn_c, kn_c, rpb_c, qn_d, kn_d, lam_q1, lam_k1, lam_q2, lam_k2, subln_d, w_branch, w_out, w_router, w_e1, w_e3, w_e2):
    B, T, D = x.shape
    Lc = ctx.shape[1]
    S = T + Lc
    depth = w_mod.shape[0]
    assert T % ROW_TILE == 0 and Lc == ROW_TILE and T // GRID_W >= NA_WIN_ROWS

    cc = jnp.concatenate([c, c_ctx[None, :], jnp.zeros((8 - B - 1, D), F32)], axis=0)
    rope_h = _rope_tables(B, T, S, HEAD_DIM)
    rope_d = _rope_tables(B, T, S, DF_QK_DIM)
    dft_lat = _dft_tables(T)
    dft_ctx = _dft_tables(Lc)
    wb, wo = w_branch.astype(BF16), w_out.astype(BF16)

    for l in range(depth):
        last = l == depth - 1
        n_ctx = 0 if last else 1
        Tq = T + n_ctx * ROW_TILE
        lam_init = 0.8 - 0.6 * math.exp(-0.3 * l)
        lam = (jnp.exp(jnp.sum(lam_q1[l].astype(F32) * lam_k1[l].astype(F32)))
               - jnp.exp(jnp.sum(lam_q2[l].astype(F32) * lam_k2[l].astype(F32))) + lam_init)
        modt = _modulation(cc, w_mod, b_mod, l).reshape(8, 6, 1, D)

        h = _prenorm(x, ctx, norm1[l], modt, 1, 0, 1)
        h2d = h.reshape(B * S, D)
        gains = _proj_gains(qn_b[l], kn_b[l], qn_c[l], kn_c[l], qn_d[l], kn_d[l])
        proj = _in_proj(h2d, w_in, l, gains, rope_h, rope_d).reshape(B, S, C_GATES)
        gates = _matmul(h2d, w_in, l, C_GATES, N_BRANCH * D, BF16, sigmoid=True,
                        name="in_proj_gates").reshape(B, S, N_BRANCH * D)

        y_b = _gqa(proj, T, Tq, ATT_SCALE)
        y_c = _na(proj, _na_bias(rpb_c[l], T), T, Tq)
        y_d = _diff_attn(lam, proj, subln_d[l], T, Tq, 1.0 - lam_init)

        conv = _shortconv(proj, hy_short[l].astype(F32), T, with_ctx=not last)
        filt = _hyena_filter_sums(T, hy_f1[l], hy_b1[l], hy_freq[l], hy_f2[l], hy_b2[l], hy_f3[l])
        y_a = _hyena(conv[:3], dft_lat, filt, hy_bias[l])
        if not last:
            filt_c = _hyena_filter_sums(Lc, hy_f1[l], hy_b1[l], hy_freq[l], hy_f2[l], hy_b2[l], hy_f3[l])
            y_a = jnp.concatenate([y_a, _hyena(conv[3:], dft_ctx, filt_c, hy_bias[l])], axis=1)

        ys = (y_a, y_b, y_c, y_d)
        x = _merge(ys, gates, wb, wo, l, x, modt, 2, ROW_TILE, 0, lambda b: b)
        if not last:
            ctx = _merge(ys, gates, wb, wo, l, ctx, modt, 2, ROW_TILE, T // ROW_TILE, lambda b: B)

        h2, aff = _prenorm(x, ctx, norm2[l], modt, 4, 3, n_ctx, w_router=w_router[l].astype(F32))
        tok, y = _moe(h2, aff, T, n_ctx * Lc, w_e1, w_e3, w_e2, l)
        x = _combine(tok, y, x, modt, 5, 0, lambda b: b)
        if not last:
            ctx = _combine(tok, y, ctx, modt, 5, T, lambda b: B)
    return x
```

```python
import functools
import math

import numpy as np
import jax
import jax.numpy as jnp
from jax import lax
from jax.experimental import pallas as pl
from jax.experimental.pallas import tpu as pltpu

F32 = jnp.float32
BF16 = jnp.bfloat16

GRID_W = 64
HEAD_DIM = 128
BRANCH_W = 512
N_BRANCH = 4
HY_W = 512
HY_ORDER = 2
HY_BANDS = 16
HY_EMB = 1 + 2 * HY_BANDS
HY_FAST_DECAY = 0.3
HY_SLOW_DECAY = 1.5
HY_TARGET = 1e-2
GQA_HEADS = 4
GQA_KV = 2
NA_HEADS = 4
NA_WIN_H = 8
NA_WIN_W = 16
DF_HEADS = 4
DF_QK_DIM = 64
N_EXPERTS = 16
CAPACITY_FACTOR = 2
ROPE_THETA = 10000.0
NORM_EPS = 1e-6
ATT_SCALE = HEAD_DIM ** -0.5
DF_SCALE = DF_QK_DIM ** -0.5
NEG_INF = -1e30
LOG2E = math.log2(math.e)

LANE = 128
ROW_TILE = 256
PROJ_TILE_N = 512
NA_GROUP_ROWS = 4
NA_WIN_ROWS = NA_GROUP_ROWS + NA_WIN_H
DFT_HALF = 256
VMEM_LIMIT =56 * 1024 * 1024

C_HY = 0
C_QB = 3 * HY_W
C_KB = C_QB + GQA_HEADS * HEAD_DIM
C_VB = C_KB + GQA_KV * HEAD_DIM
C_QC = C_VB + GQA_KV * HEAD_DIM
C_KC = C_QC + NA_HEADS * HEAD_DIM
C_VC = C_KC + NA_HEADS * HEAD_DIM
C_QD = C_VC + NA_HEADS * HEAD_DIM
C_KD = C_QD + DF_HEADS * 2 * DF_QK_DIM
C_VD = C_KD + DF_HEADS * 2 * DF_QK_DIM
C_GATES = C_VD + DF_HEADS * HEAD_DIM


def _cp(*sem):
    return pltpu.CompilerParams(dimension_semantics=sem, vmem_limit_bytes=VMEM_LIMIT)


def _dot(a, b):
    return jnp.dot(a, b, preferred_element_type=F32)


def _dot_nt(a, b):
    return lax.dot_general(a, b, (((1,), (1,)), ((), ())), preferred_element_type=F32)


def _dot_hi(a, b):
    return jnp.dot(a, b, preferred_element_type=F32, precision=lax.Precision.HIGHEST)


def _mod_kernel(c_ref, w_ref, b_ref, o_ref):
    c = c_ref[...]
    a = (c * jax.nn.sigmoid(c)).astype(BF16)
    o_ref[...] = _dot(a, w_ref[...].astype(BF16)) + b_ref[...]


def _modulation(cc, w, b, l):
    depth, D, N = w.shape
    tn = math.gcd(N, 1024)
    return pl.pallas_call(
        _mod_kernel,
        grid=(N // tn,),
        in_specs=[pl.BlockSpec((8, D), lambda j: (0, 0)),
                  pl.BlockSpec((None, D, tn), lambda j: (l, 0, j)),
                  pl.BlockSpec((None, 1, tn), lambda j: (l, 0, j))],
        out_specs=pl.BlockSpec((8, tn), lambda j: (0, j)),
        out_shape=jax.ShapeDtypeStruct((8, N), F32),
        compiler_params=_cp("arbitrary"),
        name="modulation",
    )(cc, w, b.reshape(depth, 1, N))


def _prenorm_body(x, g_ref, sc_ref, sh_ref):
    y = x * lax.rsqrt(jnp.mean(x * x, axis=-1, keepdims=True) + NORM_EPS)
    return y * g_ref[...] * (1.0 + sc_ref[...]) + sh_ref[...]


def _prenorm_kernel(x_ref, c_ref, g_ref, sc_ref, sh_ref, o_ref, *, n_lat):
    i = pl.program_id(1)

    @pl.when(i < n_lat)
    def _():
        o_ref[...] = _prenorm_body(x_ref[...], g_ref, sc_ref, sh_ref).astype(BF16)

    @pl.when(i >= n_lat)
    def _():
        o_ref[...] = _prenorm_body(c_ref[...], g_ref, sc_ref, sh_ref).astype(BF16)


def _prenorm_router_kernel(x_ref, c_ref, g_ref, sc_ref, sh_ref, wr_ref, o_ref, a_ref, *, n_lat, n_exp):
    i = pl.program_id(1)

    def run(x):
        h = _prenorm_body(x, g_ref, sc_ref, sh_ref)
        o_ref[...] = h.astype(BF16)
        logits = _dot_hi(h, wr_ref[...])
        lane = lax.broadcasted_iota(jnp.int32, logits.shape, 1)
        logits = jnp.where(lane < n_exp, logits, NEG_INF)
        e = jnp.exp(logits - jnp.max(logits, axis=-1, keepdims=True))
        a_ref[...] = e / jnp.sum(e, axis=-1, keepdims=True)

    @pl.when(i < n_lat)
    def _():
        run(x_ref[...])

    @pl.when(i >= n_lat)
    def _():
        run(c_ref[...])


def _prenorm(x, ctx, gain, modt, q_scale, q_shift, n_ctx_tiles, w_router=None):
    B, T, D = x.shape
    n_lat = T // ROW_TILE
    nt = n_lat + n_ctx_tiles
    S = nt * ROW_TILE

    def mod_map(q):
        return lambda b, i: (jnp.where(i < n_lat, b, B), q, 0, 0)

    in_specs = [
        pl.BlockSpec((None, ROW_TILE, D), lambda b, i: (b, jnp.minimum(i, n_lat - 1), 0)),
        pl.BlockSpec((None, ROW_TILE, D), lambda b, i: (b, jnp.maximum(i - n_lat, 0), 0)),
        pl.BlockSpec((1, D), lambda b, i: (0, 0)),
        pl.BlockSpec((None, None, 1, D), mod_map(q_scale)),
        pl.BlockSpec((None, None, 1, D), mod_map(q_shift)),
    ]
    out_h = pl.BlockSpec((None, ROW_TILE, D), lambda b, i: (b, i, 0))
    shape_h = jax.ShapeDtypeStruct((B, S, D), BF16)
    args = [x, ctx, gain.reshape(1, D), modt, modt]
    if w_router is None:
        return pl.pallas_call(
            functools.partial(_prenorm_kernel, n_lat=n_lat),
            grid=(B, nt), in_specs=in_specs, out_specs=out_h, out_shape=shape_h,
            compiler_params=_cp("parallel", "arbitrary"), name="prenorm",
        )(*args)
    n_exp = w_router.shape[1]
    wr = jnp.pad(w_router, ((0, 0), (0, LANE - n_exp)))
    return pl.pallas_call(
        functools.partial(_prenorm_router_kernel, n_lat=n_lat, n_exp=n_exp),
        grid=(B, nt),
        in_specs=in_specs + [pl.BlockSpec((D, LANE), lambda b, i: (0, 0))],
        out_specs=[out_h, pl.BlockSpec((None, ROW_TILE, LANE), lambda b, i: (b, i, 0))],
        out_shape=[shape_h, jax.ShapeDtypeStruct((B, S, LANE), F32)],
        compiler_params=_cp("parallel", "arbitrary"), name="prenorm_router",
    )(*args, wr)


def _mm_kernel(a_ref, b_ref, o_ref, *, sigmoid):
    acc = _dot(a_ref[...], b_ref[...].astype(BF16))
    if sigmoid:
        acc = jax.nn.sigmoid(acc)
    o_ref[...] = acc.astype(o_ref.dtype)


def _matmul(a, b, l, col0, N, out_dtype, sigmoid=False, name="matmul"):
    M, K = a.shape
    tm, tn = math.gcd(M, 1024), PROJ_TILE_N
    cb = col0 // tn
    return pl.pallas_call(
        functools.partial(_mm_kernel, sigmoid=sigmoid),
        grid=(M // tm, N // tn),
        in_specs=[pl.BlockSpec((tm, K), lambda i, j: (i, 0)),
                  pl.BlockSpec((None, K, tn), lambda i, j: (l, 0, cb + j))],
        out_specs=pl.BlockSpec((tm, tn), lambda i, j: (i, j)),
        out_shape=jax.ShapeDtypeStruct((M, N), out_dtype),
        compiler_params=_cp("parallel", "arbitrary"), name=name,
    )(a, b)


def _qk_post(x, g, seg, tables):
    lane = lax.broadcasted_iota(jnp.int32, (1, LANE), 1)
    sq = x * x
    if seg == LANE:
        ms = jnp.mean(sq, axis=-1, keepdims=True)
    else:
        lo = jnp.sum(jnp.where(lane < seg, sq, 0.0), axis=-1, keepdims=True)
        hi = jnp.sum(jnp.where(lane >= seg, sq, 0.0), axis=-1, keepdims=True)
        ms = jnp.where(lane < seg, lo, hi) * (1.0 / seg)
    y = x * lax.rsqrt(ms + NORM_EPS) * g
    if tables is not None:
        cos_ref, sin_ref = tables
        q = seg // 4
        partner = jnp.where((lane % (seg // 2)) < q, pltpu.roll(y, LANE - q, 1), pltpu.roll(y, q, 1))
        y = y * cos_ref[...] + partner * sin_ref[...]
    return y


def _proj_block_kinds():
    kinds = []
    for col in range(0, C_GATES, LANE):
        if C_QB <= col < C_VB:
            kinds.append((HEAD_DIM, True))
        elif C_QC <= col < C_VC:
            kinds.append((HEAD_DIM, False))
        elif C_QD <= col < C_VD:
            kinds.append((DF_QK_DIM, True))
        else:
            kinds.append(None)
    return kinds


def _inproj_kernel(a_ref, w_ref, g_ref, c128_ref, s128_ref, c64_ref, s64_ref, o_ref):
    j = pl.program_id(1)
    acc = _dot(a_ref[...], w_ref[...].astype(BF16))
    per_tile = PROJ_TILE_N // LANE
    kinds = _proj_block_kinds()
    tiles = [kinds[t * per_tile:(t + 1) * per_tile] for t in range(len(kinds) // per_tile)]
    plain = functools.reduce(jnp.logical_or, [j == t for t, ks in enumerate(tiles) if not any(ks)])

    @pl.when(plain)
    def _():
        o_ref[...] = acc.astype(o_ref.dtype)

    for t, ks in enumerate(tiles):
        if not any(ks):
            continue

        @pl.when(j == t)
        def _(ks=ks):
            for h, kind in enumerate(ks):
                y = acc[:, h * LANE:(h + 1) * LANE]
                if kind is not None:
                    seg, rope = kind
                    tables = None if not rope else ((c128_ref, s128_ref) if seg == HEAD_DIM else (c64_ref, s64_ref))
                    y = _qk_post(y, g_ref[h:h + 1, :], seg, tables)
                o_ref[:, h * LANE:(h + 1) * LANE] = y.astype(o_ref.dtype)


def _in_proj(h2d, w_in_bf, l, gains, rope_h, rope_d):
    M, K = h2d.shape
    tm, tn = math.gcd(M, 1024), PROJ_TILE_N
    per_tile = tn // LANE
    table = pl.BlockSpec((tm, LANE), lambda i, j: (i, 0))
    return pl.pallas_call(
        _inproj_kernel,
        grid=(M // tm, C_GATES // tn),
        in_specs=[pl.BlockSpec((tm, K), lambda i, j: (i, 0)),
                  pl.BlockSpec((None, K, tn), lambda i, j: (l, 0, j)),
                  pl.BlockSpec((None, per_tile, LANE), lambda i, j: (j, 0, 0)),
                  table, table, table, table],
        out_specs=pl.BlockSpec((tm, tn), lambda i, j: (i, j)),
        out_shape=jax.ShapeDtypeStruct((M, C_GATES), BF16),
        compiler_params=_cp("parallel", "arbitrary"), name="in_proj",
    )(h2d, w_in_bf, gains.reshape(-1, per_tile, LANE), *rope_h, *rope_d)


def _proj_gains(qn_b, kn_b, qn_c, kn_c, qn_d, kn_d):
    ones = lambda n: jnp.ones((n, LANE), F32)
    rep = lambda g, n: jnp.tile(jnp.tile(g.astype(F32), LANE // g.shape[0])[None, :], (n, 1))
    return jnp.concatenate([
        ones(C_QB // LANE), rep(qn_b, GQA_HEADS), rep(kn_b, GQA_KV), ones(GQA_KV),
        rep(qn_c, NA_HEADS), rep(kn_c, NA_HEADS), ones(NA_HEADS),
        rep(qn_d, DF_HEADS), rep(kn_d, DF_HEADS), ones(DF_HEADS)], axis=0)


def _rope_tables(B, T, S, seg):
    half = seg // 2
    nfreq = half // 2
    inv = ROPE_THETA ** (-jnp.arange(0, half, 2, dtype=F32) / half)
    pos = jnp.arange(T, dtype=jnp.int32)
    rows, cols = (pos // GRID_W).astype(F32), (pos % GRID_W).astype(F32)
    l = np.arange(LANE) % seg
    use_col = l >= half
    fidx = (l % half) % nfreq
    is_b = (l % half) >= nfreq
    ang = jnp.where(use_col[None, :], cols[:, None], rows[:, None]) * inv[fidx][None, :]
    cos, sin = jnp.cos(ang), jnp.sin(ang)
    sin = jnp.where(is_b[None, :], sin, -sin)
    cos = jnp.concatenate([cos, jnp.ones((S - T, LANE), F32)], axis=0)
    sin = jnp.concatenate([sin, jnp.zeros((S - T, LANE), F32)], axis=0)
    return jnp.tile(cos, (B, 1)), jnp.tile(sin, (B, 1))


def _softmax_rows(s, scale):
    c = scale * LOG2E
    m = jnp.max(s, axis=-1, keepdims=True)
    p = jnp.exp2(s * c - m * c)
    return p.astype(BF16), jnp.sum(p, axis=-1, keepdims=True)


def _attend(qs, ks, vs, scale):
    scores = [_dot_nt(q, k) for q, k in zip(qs, ks)]
    outs = []
    for s, v in zip(scores, vs):
        p, l = _softmax_rows(s, scale)
        outs.append(_dot(p, v) / l)
    return outs


def _attend_t(qts, ks, vts, scale):
    c = scale * LOG2E
    scores = [_dot(k, qt) for qt, k in zip(qts, ks)]
    outs = []
    for s, vt in zip(scores, vts):
        m = jnp.max(s, axis=0, keepdims=True)
        p = jnp.exp2(s * c - m * c)
        l = jnp.sum(p, axis=0, keepdims=True)
        outs.append(_dot(vt, p.astype(BF16)) / l)
    return outs


def _store_vt(vt_ref, v):
    for h in range(vt_ref.shape[0]):
        vt_ref[h] = v[:, h * LANE:(h + 1) * LANE].T


GQA_KV_PER_STEP = 2


def _gqa_kernel(q_ref, k_ref, v_ref, o_ref, vt_ref, *, T, R, scale):
    i = pl.program_id(2)

    @pl.when(i == 0)
    def _():
        _store_vt(vt_ref, v_ref[...])

    def attend(lo):
        n_heads = q_ref.shape[1] // LANE
        qts = [q_ref[:, h * LANE:(h + 1) * LANE].T for h in range(n_heads)]
        ks = [k_ref[lo:, (h // R) * LANE:(h // R + 1) * LANE] for h in range(n_heads)]
        outs = _attend_t(qts, ks, [vt_ref[h // R, :, lo:] for h in range(n_heads)], scale)
        for h in range(n_heads):
            o_ref[:, h * LANE:(h + 1) * LANE] = outs[h].T.astype(o_ref.dtype)

    @pl.when(i < T // ROW_TILE)
    def _():
        attend(0)

    @pl.when(i >= T // ROW_TILE)
    def _():
        attend(T)


def _gqa(proj, T, Tq, scale):
    B, S, _ = proj.shape
    R = GQA_HEADS // GQA_KV
    G = GQA_KV_PER_STEP
    qb, kb, vb = C_QB // (G * R * LANE), C_KB // (G * LANE), C_VB // (G * LANE)
    return pl.pallas_call(
        functools.partial(_gqa_kernel, T=T, R=R, scale=scale),
        grid=(B, GQA_KV // G, Tq // ROW_TILE),
        in_specs=[pl.BlockSpec((None, ROW_TILE, G * R * LANE), lambda b, g, i: (b, i, qb + g)),
                  pl.BlockSpec((None, S, G * LANE), lambda b, g, i: (b, 0, kb + g)),
                  pl.BlockSpec((None, S, G * LANE), lambda b, g, i: (b, 0, vb + g))],
        out_specs=pl.BlockSpec((None, ROW_TILE, G * R * LANE), lambda b, g, i: (b, i, g)),
        out_shape=jax.ShapeDtypeStruct((B, Tq, GQA_HEADS * LANE), BF16),
        scratch_shapes=[pltpu.VMEM((G, LANE, S), BF16)],
        compiler_params=_cp("parallel", "parallel", "arbitrary"), name="gqa",
    )(proj, proj, proj)


DF_HEADS_PER_STEP = 2


def _diff_kernel(lam_ref, q_ref, k_ref, v_ref, g_ref, o_ref, vt_ref, *, T, scale, out_scale):
    i = pl.program_id(2)
    lam = lam_ref[0]

    @pl.when(i == 0)
    def _():
        _store_vt(vt_ref, v_ref[...])

    def attend(lo):
        qts, ks, vts = [], [], []
        for h in range(DF_HEADS_PER_STEP):
            qt = q_ref[:, h * LANE:(h + 1) * LANE].T
            ch = lax.broadcasted_iota(jnp.int32, qt.shape, 0)
            zero = jnp.zeros_like(qt)
            qts += [jnp.where(ch < DF_QK_DIM, qt, zero), jnp.where(ch >= DF_QK_DIM, qt, zero)]
            ks += [k_ref[lo:, h * LANE:(h + 1) * LANE]] * 2
            vts += [vt_ref[h, :, lo:]] * 2
        outs = _attend_t(qts, ks, vts, scale)
        for h in range(DF_HEADS_PER_STEP):
            o = (outs[2 * h] - lam * outs[2 * h + 1]).T
            y = o * lax.rsqrt(jnp.mean(o * o, axis=-1, keepdims=True) + NORM_EPS) * g_ref[...]
            o_ref[:, h * LANE:(h + 1) * LANE] = (y * out_scale).astype(o_ref.dtype)

    @pl.when(i < T // ROW_TILE)
    def _():
        attend(0)

    @pl.when(i >= T // ROW_TILE)
    def _():
        attend(T)


def _diff_attn(lam, proj, subln, T, Tq, out_scale):
    B, S, _ = proj.shape
    W = DF_HEADS_PER_STEP * LANE
    qb, kb, vb = C_QD // W, C_KD // W, C_VD // W
    return pl.pallas_call(
        functools.partial(_diff_kernel, T=T, scale=DF_SCALE, out_scale=out_scale),
        grid=(B, DF_HEADS // DF_HEADS_PER_STEP, Tq // ROW_TILE),
        in_specs=[pl.BlockSpec(memory_space=pltpu.SMEM),
                  pl.BlockSpec((None, ROW_TILE, W), lambda b, h, i: (b, i, qb + h)),
                  pl.BlockSpec((None, S, W), lambda b, h, i: (b, 0, kb + h)),
                  pl.BlockSpec((None, S, W), lambda b, h, i: (b, 0, vb + h)),
                  pl.BlockSpec((1, LANE), lambda b, h, i: (0, 0))],
        out_specs=pl.BlockSpec((None, ROW_TILE, W), lambda b, h, i: (b, i, h)),
        out_shape=jax.ShapeDtypeStruct((B, Tq, DF_HEADS * LANE), BF16),
        scratch_shapes=[pltpu.VMEM((DF_HEADS_PER_STEP, LANE, S), BF16)],
        compiler_params=_cp("parallel", "parallel", "arbitrary"), name="diff_attn",
    )(lam.reshape(1).astype(F32), proj, proj, proj, subln.astype(F32).reshape(1, LANE))


def _na_kernel(q_ref, k_ref, v_ref, bias_ref, o_ref, *, T, scale):
    i = pl.program_id(2)
    n_groups = T // ROW_TILE
    grid_rows = T // GRID_W
    q = q_ref[...]
    kc, vc = k_ref[T:, :], v_ref[T:, :]
    s_c = _dot_nt(q, kc)

    @pl.when(i < n_groups)
    def _():
        row0 = jnp.clip(i * NA_GROUP_ROWS - NA_WIN_H // 2, 0, grid_rows - NA_WIN_ROWS)
        start = pl.multiple_of(row0 * GRID_W, ROW_TILE)
        kw = k_ref[pl.ds(start, NA_WIN_ROWS * GRID_W), :]
        vw = v_ref[pl.ds(start, NA_WIN_ROWS * GRID_W), :]
        c = scale * LOG2E
        u_n = _dot_nt(q, kw) * c + bias_ref[...]
        u_c = s_c * c
        m = jnp.maximum(jnp.max(u_n, axis=-1, keepdims=True), jnp.max(u_c, axis=-1, keepdims=True))
        p_n, p_c = jnp.exp2(u_n - m), jnp.exp2(u_c - m)
        l = jnp.sum(p_n, axis=-1, keepdims=True) + jnp.sum(p_c, axis=-1, keepdims=True)
        o = (_dot(p_n.astype(BF16), vw) + _dot(p_c.astype(BF16), vc)) / l
        o_ref[...] = o.astype(o_ref.dtype)

    @pl.when(i >= n_groups)
    def _():
        p, l = _softmax_rows(s_c, scale)
        o_ref[...] = (_dot(p, vc) / l).astype(o_ref.dtype)


def _na_bias(rpb, T):
    rows = T // GRID_W
    n_groups = rows // NA_GROUP_ROWS
    n_roff, n_coff = 2 * NA_WIN_H - 1, 2 * NA_WIN_W - 1
    c = np.arange(GRID_W)[:, None]
    kc = np.arange(GRID_W)[None, :]
    cs = np.clip(c - NA_WIN_W // 2, 0, GRID_W - NA_WIN_W)
    col_valid = (kc >= cs) & (kc < cs + NA_WIN_W)
    coff = np.clip(kc - c + NA_WIN_W - 1, 0, n_coff - 1)
    col_sel = (coff[..., None] == np.arange(n_coff)).astype(np.float32)
    rpb = rpb.astype(F32) * LOG2E
    out = []
    for grp in (0, 1, n_groups - 1):
        r = grp * NA_GROUP_ROWS + np.arange(NA_GROUP_ROWS)[:, None]
        rs = np.clip(r - NA_WIN_H // 2, 0, rows - NA_WIN_H)
        row0 = np.clip(grp * NA_GROUP_ROWS - NA_WIN_H // 2, 0, rows - NA_WIN_ROWS)
        key_row = row0 + np.arange(NA_WIN_ROWS)[None, :]
        row_valid = (key_row >= rs) & (key_row < rs + NA_WIN_H)
        roff = np.clip(key_row - r + NA_WIN_H - 1, 0, n_roff - 1)
        row_sel = (roff[..., None] == np.arange(n_roff)).astype(np.float32)
        bias = jnp.einsum('ika,hab,cqb->hickq', row_sel, rpb, col_sel, precision=lax.Precision.HIGHEST)
        valid = row_valid[:, None, :, None] & col_valid[None, :, None, :]
        out.append(jnp.where(valid[None], bias, NEG_INF).reshape(rpb.shape[0], ROW_TILE, -1))
    return jnp.stack(out)


def _na(proj, bias, T, Tq):
    B, S, _ = proj.shape
    qb, kb, vb = C_QC // LANE, C_KC // LANE, C_VC // LANE
    n_groups = T // ROW_TILE
    nk = NA_WIN_ROWS * GRID_W

    def bias_map(b, h, i):
        return (jnp.where(i == 0, 0, jnp.where(i >= n_groups - 1, 2, 1)), h, 0, 0)

    return pl.pallas_call(
        functools.partial(_na_kernel, T=T, scale=ATT_SCALE),
        grid=(B, NA_HEADS, Tq // ROW_TILE),
        in_specs=[pl.BlockSpec((None, ROW_TILE, LANE), lambda b, h, i: (b, i, qb + h)),
                  pl.BlockSpec((None, S, LANE), lambda b, h, i: (b, 0, kb + h)),
                  pl.BlockSpec((None, S, LANE), lambda b, h, i: (b, 0, vb + h)),
                  pl.BlockSpec((None, None, ROW_TILE, nk), bias_map)],
        out_specs=pl.BlockSpec((None, ROW_TILE, LANE), lambda b, h, i: (b, i, h)),
        out_shape=jax.ShapeDtypeStruct((B, Tq, NA_HEADS * LANE), BF16),
        compiler_params=_cp("parallel", "parallel", "arbitrary"), name="na_attn",
    )(proj, proj, proj, bias)


def _shortconv_kernel(*refs, T, with_ctx):
    u_refs, w_refs, o_refs = refs[0:3], refs[3:6], refs[6:]
    S = u_refs[0].shape[0]
    row = lax.broadcasted_iota(jnp.int32, (S, 1), 0)
    first = (row == 0) | (row == T)
    last = (row == T - 1) | (row == S - 1)
    for n in range(3):
        u = u_refs[n][...].astype(F32)
        w = w_refs[n][...]
        prev = jnp.where(first, 0.0, pltpu.roll(u, 1, 0))
        nxt = jnp.where(last, 0.0, pltpu.roll(u, S - 1, 0))
        y = (prev * w[0:1] + u * w[1:2] + nxt * w[2:3]).astype(BF16)
        o_refs[n][...] = y[:T]
        if with_ctx:
            o_refs[3 + n][...] = y[T:]


def _shortconv(proj, w, T, with_ctx):
    B, S, _ = proj.shape
    Lc = S - T
    nct = HY_W // LANE
    in_specs = [pl.BlockSpec((None, S, LANE), functools.partial(lambda b, c, n: (b, 0, n * nct + c), n=n))
                for n in range(3)]
    in_specs += [pl.BlockSpec((3, LANE), functools.partial(lambda b, c, n: (0, n * nct + c), n=n))
                 for n in range(3)]
    out_specs = [pl.BlockSpec((T, LANE), lambda b, c: (0, b * nct + c))] * 3
    out_shape = [jax.ShapeDtypeStruct((T, B * HY_W), BF16)] * 3
    if with_ctx:
        out_specs += [pl.BlockSpec((Lc, LANE), lambda b, c: (0, b * nct + c))] * 3
        out_shape += [jax.ShapeDtypeStruct((Lc, B * HY_W), BF16)] * 3
    return pl.pallas_call(
        functools.partial(_shortconv_kernel, T=T, with_ctx=with_ctx),
        grid=(B, nct), in_specs=in_specs, out_specs=out_specs, out_shape=out_shape,
        compiler_params=_cp("parallel", "arbitrary"), name="hy_shortconv",
    )(proj, proj, proj, w, w, w)


def _filter_kernel(z_ref, f1_ref, b1_ref, fr_ref, f2_ref, b2_ref, f3f_ref, f3b_ref, dl_ref, hs_ref, hd_ref):
    z = z_ref[...]
    fr = fr_ref[...]
    hid = jnp.sin(fr * (_dot_hi(z, f1_ref[...]) + b1_ref[...]))
    hid = jnp.sin(fr * (_dot_hi(hid, f2_ref[...]) + b2_ref[...]))
    decay = jnp.exp(-z[:, 0:1] * dl_ref[...])
    fw = _dot_hi(hid, f3f_ref[...]) * decay
    bw = _dot_hi(hid, f3b_ref[...]) * decay
    row = lax.broadcasted_iota(jnp.int32, bw.shape, 0)
    bw = jnp.where(row == 0, 0.0, bw)
    inv = 1.0 / (jnp.sum(jnp.abs(fw), axis=0, keepdims=True) + jnp.sum(jnp.abs(bw), axis=0, keepdims=True))
    hs_ref[...] = ((fw + bw) * inv).astype(BF16)
    hd_ref[...] = ((fw - bw) * inv).astype(BF16)


def _hyena_filter_sums(L, f1, b1, freq, f2, b2, f3):
    t = jnp.linspace(0.0, 1.0, L, dtype=F32)[:, None]
    w = (2.0 * math.pi / L) * jnp.arange(L, dtype=F32)[:, None]
    bands = jnp.linspace(1e-4, HY_BANDS - 1, HY_BANDS, dtype=F32)[None, :]
    z = jnp.concatenate([t, jnp.cos(bands * w), -jnp.sin(bands * w)], axis=-1)
    z = jnp.pad(z, ((0, 0), (0, LANE - HY_EMB)))
    f1p = jnp.pad(f1.astype(F32), ((0, LANE - HY_EMB), (0, 0)))
    ffn = f1.shape[1]
    deltas = jnp.abs(jnp.linspace(math.log(HY_TARGET) / HY_SLOW_DECAY, math.log(HY_TARGET) / HY_FAST_DECAY,
                                  HY_W, dtype=F32))
    NC = HY_ORDER * HY_W
    dl = jnp.tile(deltas, HY_ORDER).reshape(1, NC)
    tn = 256
    small = lambda shape: pl.BlockSpec(shape, lambda j: (0, 0))
    return pl.pallas_call(
        _filter_kernel,
        grid=(NC // tn,),
        in_specs=[small((L, LANE)), small((LANE, ffn)), small((1, ffn)), small((1, ffn)),
                  small((ffn, ffn)), small((1, ffn)),
                  pl.BlockSpec((ffn, tn), lambda j: (0, j)),
                  pl.BlockSpec((ffn, tn), lambda j: (0, NC // tn + j)),
                  pl.BlockSpec((1, tn), lambda j: (0, j))],
        out_specs=[pl.BlockSpec((L, tn), lambda j: (0, j))] * 2,
        out_shape=[jax.ShapeDtypeStruct((L, NC), BF16)] * 2,
        compiler_params=_cp("arbitrary"), name="hy_filter",
    )(z, f1p, b1.astype(F32).reshape(1, ffn), freq.astype(F32).reshape(1, ffn), f2.astype(F32),
      b2.astype(F32).reshape(1, ffn), f3.astype(F32), f3.astype(F32), dl)


def _dft_tables(L):
    N = 2 * L
    r = np.arange(N)
    k = (r // (2 * DFT_HALF)) * DFT_HALF + r % DFT_HALF
    is_im = (r // DFT_HALF) % 2 == 1
    nyq = is_im & (k == 0)
    kj = jnp.asarray(k, jnp.int32)[:, None]
    im, nyq = is_im[:, None], nyq[:, None]
    n_lo = 64

    def cos_sin(step, count):
        m = (kj * (step * jnp.arange(count, dtype=jnp.int32))[None, :]) % N
        ang = m.astype(F32) * (2.0 * math.pi / N)
        return jnp.cos(ang), jnp.sin(ang)

    c1, s1 = cos_sin(n_lo, L // n_lo)
    c0, s0 = cos_sin(1, n_lo)
    p1 = jnp.where(nyq, 1.0, jnp.where(im, -s1, c1))
    q1 = jnp.where(nyq, 0.0, jnp.where(im, -c1, -s1))
    p0 = jnp.where(nyq, (1 - 2 * (jnp.arange(n_lo) % 2)).astype(F32)[None, :], c0)
    a = (p1[:, :, None] * p0[:, None, :] + q1[:, :, None] * s0[:, None, :]).reshape(N, L).astype(BF16)
    return a, a.T


def _kf_kernel(a_ref, hs_ref, hd_ref, o_ref, *, n_fft):
    i = pl.program_id(0)
    H = DFT_HALF
    re = _dot(a_ref[:H, :], hs_ref[...])
    im = _dot(a_ref[H:, :], hd_ref[...])
    o_ref[:H, :] = re * (2.0 / n_fft)
    o_ref[H:, :] = im * (2.0 / n_fft)

    @pl.when(i == 0)
    def _():
        ny = _dot(a_ref[H:H + 16, :], hs_ref[...])
        o_ref[0:1, :] = re[0:1] * (1.0 / n_fft)
        o_ref[H:H + 1, :] = ny[0:1] * (1.0 / n_fft)


def _filter_spectrum(a, hs, hd):
    N, L = a.shape
    NC = hs.shape[1]
    tm, tn = 2 * DFT_HALF, 512
    return pl.pallas_call(
        functools.partial(_kf_kernel, n_fft=N),
        grid=(N // tm, NC // tn),
        in_specs=[pl.BlockSpec((tm, L), lambda i, j: (i, 0)),
                  pl.BlockSpec((L, tn), lambda i, j: (0, j)),
                  pl.BlockSpec((L, tn), lambda i, j: (0, j))],
        out_specs=pl.BlockSpec((tm, tn), lambda i, j: (i, j)),
        out_shape=jax.ShapeDtypeStruct((N, NC), F32),
        compiler_params=_cp("parallel", "arbitrary"), name="hy_filter_spectrum",
    )(a, hs, hd)


def _fwd_kernel(a_ref, z_ref, kf_ref, p_ref):
    i = pl.program_id(0)
    H = DFT_HALF
    acc = _dot(a_ref[...], z_ref[...])
    zr, zi = acc[:H], acc[H:]
    kr, ki = kf_ref[:H, :], kf_ref[H:, :]
    row = lax.broadcasted_iota(jnp.int32, zr.shape, 0)
    real_pair = (row == 0) & (i == 0)
    p_ref[:H, :] = jnp.where(real_pair, zr * kr, zr * kr - zi * ki).astype(BF16)
    p_ref[H:, :] = jnp.where(real_pair, zi * ki, zr * ki + zi * kr).astype(BF16)


def _dft_multiply(a, z, kf, order):
    N, L = a.shape
    NB = z.shape[1] // HY_W
    tm = 2 * DFT_HALF
    return pl.pallas_call(
        _fwd_kernel,
        grid=(N // tm, NB),
        in_specs=[pl.BlockSpec((tm, L), lambda i, j: (i, 0)),
                  pl.BlockSpec((L, HY_W), lambda i, j: (0, j)),
                  pl.BlockSpec((tm, HY_W), lambda i, j: (i, order))],
        out_specs=pl.BlockSpec((tm, HY_W), lambda i, j: (i, j)),
        out_shape=jax.ShapeDtypeStruct((N, NB * HY_W), BF16),
        compiler_params=_cp("parallel", "arbitrary"), name="hy_dft_multiply",
    )(a, z, kf)


def _inv_kernel(at_ref, p_ref, x_ref, z_ref, b_ref, o_ref):
    y = _dot(at_ref[...], p_ref[...])
    z = z_ref[...].astype(F32)
    o_ref[...] = (x_ref[...].astype(F32) * (y + b_ref[...] * z)).astype(o_ref.dtype)


def _idft_gate(at, p, xo, z, bias, batch_major):
    L, N = at.shape
    NB = z.shape[1] // HY_W
    tm = min(L, 512)
    if batch_major:
        out_spec = pl.BlockSpec((None, tm, HY_W), lambda i, j: (j, i, 0))
        out_shape = jax.ShapeDtypeStruct((NB, L, HY_W), BF16)
    else:
        out_spec = pl.BlockSpec((tm, HY_W), lambda i, j: (i, j))
        out_shape = jax.ShapeDtypeStruct((L, NB * HY_W), BF16)
    return pl.pallas_call(
        _inv_kernel,
        grid=(L // tm, NB),
        in_specs=[pl.BlockSpec((tm, N), lambda i, j: (i, 0)),
                  pl.BlockSpec((N, HY_W), lambda i, j: (0, j)),
                  pl.BlockSpec((tm, HY_W), lambda i, j: (i, j)),
                  pl.BlockSpec((tm, HY_W), lambda i, j: (i, j)),
                  pl.BlockSpec((1, HY_W), lambda i, j: (0, 0))],
        out_specs=out_spec, out_shape=out_shape,
        compiler_params=_cp("parallel", "arbitrary"), name="hy_idft_gate",
    )(at, p, xo, z, bias.astype(F32).reshape(1, HY_W))


def _hyena(vxx, tables, filt, hy_bias):
    a, at = tables
    kf = _filter_spectrum(a, *filt)
    z = vxx[0]
    for o in range(HY_ORDER):
        p = _dft_multiply(a, z, kf, o)
        z = _idft_gate(at, p, vxx[1 + o], z, hy_bias[o], batch_major=(o == HY_ORDER - 1))
    return z


def _merge_kernel(ya_ref, yb_ref, yc_ref, yd_ref, g0_ref, g1_ref, g2_ref, g3_ref, wb_ref, wo_ref, x_ref, gate_ref,
                  o_ref):
    ys = (ya_ref, yb_ref, yc_ref, yd_ref)
    gs = (g0_ref, g1_ref, g2_ref, g3_ref)
    acc = _dot(ys[0][...], wb_ref[0]) * gs[0][...].astype(F32)
    for n in range(1, N_BRANCH):
        acc += _dot(ys[n][...], wb_ref[n]) * gs[n][...].astype(F32)
    o_ref[...] = x_ref[...] + gate_ref[...] * _dot(acc.astype(BF16), wo_ref[...])


def _merge(ys, gates, wb, wo, l, x, modt, q_gate, tm, row_tile0, mod_row):
    B, R, D = x.shape
    row = lambda b, i: (b, row_tile0 + i, 0)
    in_specs = [pl.BlockSpec((None, tm, BRANCH_W), row)] * N_BRANCH
    in_specs += [pl.BlockSpec((None, tm, D), functools.partial(lambda b, i, n: (b, row_tile0 + i, n), n=n))
                 for n in range(N_BRANCH)]
    in_specs += [pl.BlockSpec((None, N_BRANCH, BRANCH_W, D), lambda b, i: (l, 0, 0, 0),
                              pipeline_mode=pl.Buffered(1)),
                 pl.BlockSpec((None, D, D), lambda b, i: (l, 0, 0), pipeline_mode=pl.Buffered(1)),
                 pl.BlockSpec((None, tm, D), lambda b, i: (b, i, 0)),
                 pl.BlockSpec((None, None, 1, D), lambda b, i: (mod_row(b), q_gate, 0, 0))]
    return pl.pallas_call(
        _merge_kernel,
        grid=(B, R // tm), in_specs=in_specs,
        out_specs=pl.BlockSpec((None, tm, D), lambda b, i: (b, i, 0)),
        out_shape=jax.ShapeDtypeStruct((B, R, D), F32),
        compiler_params=_cp("parallel", "arbitrary"), name="merge",
    )(*ys, gates, gates, gates, gates, wb, wo, x, modt)


def _expert_up_kernel(x_ref, w1_ref, w3_ref, o_ref):
    x = x_ref[...]
    a = _dot(x, w1_ref[...].astype(BF16))
    b = _dot(x, w3_ref[...].astype(BF16))
    o_ref[...] = (a * jax.nn.sigmoid(a) * b).astype(o_ref.dtype)


def _expert_down_kernel(h_ref, g_ref, w2_ref, o_ref):
    o_ref[...] = (_dot(h_ref[...], w2_ref[...].astype(BF16)) * g_ref[...]).astype(o_ref.dtype)


def _experts(xg, gate, w1, w3, w2, l):
    E, M, D = xg.shape
    F = w1.shape[3]
    tf = 256
    hmid = pl.pallas_call(
        _expert_up_kernel,
        grid=(E, F // tf),
        in_specs=[pl.BlockSpec((None, M, D), lambda e, f: (e, 0, 0)),
                  pl.BlockSpec((None, None, D, tf), lambda e, f: (l, e, 0, f)),
                  pl.BlockSpec((None, None, D, tf), lambda e, f: (l, e, 0, f))],
        out_specs=pl.BlockSpec((None, M, tf), lambda e, f: (e, 0, f)),
        out_shape=jax.ShapeDtypeStruct((E, M, F), BF16),
        compiler_params=_cp("parallel", "arbitrary"), name="expert_up",
    )(xg, w1, w3)
    return pl.pallas_call(
        _expert_down_kernel,
        grid=(E, D // tf),
        in_specs=[pl.BlockSpec((None, M, F), lambda e, n: (e, 0, 0)),
                  pl.BlockSpec((None, M, 1), lambda e, n: (e, 0, 0)),
                  pl.BlockSpec((None, None, F, tf), lambda e, n: (l, e, 0, n))],
        out_specs=pl.BlockSpec((None, M, tf), lambda e, n: (e, 0, n)),
        out_shape=jax.ShapeDtypeStruct((E, M, D), BF16),
        compiler_params=_cp("parallel", "arbitrary"), name="expert_down",
    )(hmid, gate, w2)


def _combine_kernel(tok_ref, y_ref, x_ref, g_ref, o_ref, *, row0):
    E, Ct, tn = y_ref.shape
    rowid = row0 + pl.program_id(2) * ROW_TILE + lax.broadcasted_iota(jnp.int32, (ROW_TILE, 1), 0)
    onehot = jnp.where(tok_ref[...] == rowid, 1.0, 0.0).astype(BF16)
    o_ref[...] = x_ref[...] + g_ref[...] * _dot(onehot, y_ref[...].reshape(E * Ct, tn))


def _combine(tok, y, x, modt, q_gate, row0, mod_row):
    B, R, D = x.shape
    E, _, Ct, _ = y.shape
    tn = math.gcd(D, PROJ_TILE_N)
    return pl.pallas_call(
        functools.partial(_combine_kernel, row0=row0),
        grid=(B, D // tn, R // ROW_TILE),
        in_specs=[pl.BlockSpec((None, 1, E * Ct), lambda b, n, t: (b, 0, 0)),
                  pl.BlockSpec((E, None, Ct, tn), lambda b, n, t: (0, b, 0, n)),
                  pl.BlockSpec((None, ROW_TILE, tn), lambda b, n, t: (b, t, n)),
                  pl.BlockSpec((None, None, 1, tn), lambda b, n, t: (mod_row(b), q_gate, 0, n))],
        out_specs=pl.BlockSpec((None, ROW_TILE, tn), lambda b, n, t: (b, t, n)),
        out_shape=jax.ShapeDtypeStruct((B, R, D), F32),
        compiler_params=_cp("parallel", "parallel", "arbitrary"), name="moe_combine",
    )(tok, y, x, modt)


def _moe(h2, aff, T, Lc, w1, w3, w2, l):
    B, S, D = h2.shape
    E = w1.shape[1]

    def route(a):
        n = a.shape[1]
        return lax.top_k(a.transpose(0, 2, 1), CAPACITY_FACTOR * n // E)

    gate, tok = route(aff[:, :T, :E])
    if Lc:
        gate_c, tok_c = route(aff[:, T:, :E])
        gate, tok = jnp.concatenate([gate, gate_c], axis=-1), jnp.concatenate([tok, T + tok_c], axis=-1)
    Ct = tok.shape[-1]
    rows = (tok + (jnp.arange(B, dtype=jnp.int32) * S)[:, None, None]).transpose(1, 0, 2).reshape(E, B * Ct)
    xg = h2.reshape(B * S, D)[rows]
    y = _experts(xg, gate.transpose(1, 0, 2).reshape(E, B * Ct, 1), w1, w3, w2, l)
    return tok.reshape(B, 1, E * Ct), y.reshape(E, B, Ct, D)


def kernel(x, c, ctx, c_ctx, w_mod, b_mod, norm1, norm2, w_in, hy_short, hy_f1, hy_b1, hy_freq, hy_f2, hy_b2, hy_f3, hy_bias, qn_b, kn_b, qn_c, kn_c, rpb_c, qn_d, kn_d, lam_q1, lam_k1, lam_q2, lam_k2, subln_d, w_branch, w_out, w_router, w_e1, w_e3, w_e2):
    B, T, D = x.shape
    Lc = ctx.shape[1]
    S = T + Lc
    depth = w_mod.shape[0]
    assert T % ROW_TILE == 0 and Lc == ROW_TILE and T // GRID_W >= NA_WIN_ROWS

    cc = jnp.concatenate([c, c_ctx[None, :], jnp.zeros((8 - B - 1, D), F32)], axis=0)
    rope_h = _rope_tables(B, T, S, HEAD_DIM)
    rope_d = _rope_tables(B, T, S, DF_QK_DIM)
    dft_lat = _dft_tables(T)
    dft_ctx = _dft_tables(Lc)
    wb, wo = w_branch.astype(BF16), w_out.astype(BF16)

    for l in range(depth):
        last = l == depth - 1
        n_ctx = 0 if last else 1
        Tq = T + n_ctx * ROW_TILE
        lam_init = 0.8 - 0.6 * math.exp(-0.3 * l)
        lam = (jnp.exp(jnp.sum(lam_q1[l].astype(F32) * lam_k1[l].astype(F32)))
               - jnp.exp(jnp.sum(lam_q2[l].astype(F32) * lam_k2[l].astype(F32))) + lam_init)
        modt = _modulation(cc, w_mod, b_mod, l).reshape(8, 6, 1, D)

        h = _prenorm(x, ctx, norm1[l], modt, 1, 0, 1)
        h2d = h.reshape(B * S, D)
        gains = _proj_gains(qn_b[l], kn_b[l], qn_c[l], kn_c[l], qn_d[l], kn_d[l])
        proj = _in_proj(h2d, w_in, l, gains, rope_h, rope_d).reshape(B, S, C_GATES)
        gates = _matmul(h2d, w_in, l, C_GATES, N_BRANCH * D, BF16, sigmoid=True,
                        name="in_proj_gates").reshape(B, S, N_BRANCH * D)

        y_b = _gqa(proj, T, Tq, ATT_SCALE)
        y_c = _na(proj, _na_bias(rpb_c[l], T), T, Tq)
        y_d = _diff_attn(lam, proj, subln_d[l], T, Tq, 1.0 - lam_init)

        conv = _shortconv(proj, hy_short[l].astype(F32), T, with_ctx=not last)
        filt = _hyena_filter_sums(T, hy_f1[l], hy_b1[l], hy_freq[l], hy_f2[l], hy_b2[l], hy_f3[l])
        y_a = _hyena(conv[:3], dft_lat, filt, hy_bias[l])
        if not last:
            filt_c = _hyena_filter_sums(Lc, hy_f1[l], hy_b1[l], hy_freq[l], hy_f2[l], hy_b2[l], hy_f3[l])
            y_a = jnp.concatenate([y_a, _hyena(conv[3:], dft_ctx, filt_c, hy_bias[l])], axis=1)

        ys = (y_a, y_b, y_c, y_d)
        x = _merge(ys, gates, wb, wo, l, x, modt, 2, ROW_TILE, 0, lambda b: b)
        if not last:
            ctx = _merge(ys, gates, wb, wo, l, ctx, modt, 2, ROW_TILE, T // ROW_TILE, lambda b: B)

        h2, aff = _prenorm(x, ctx, norm2[l], modt, 4, 3, n_ctx, w_router=w_router[l].astype(F32))
        tok, y = _moe(h2, aff, T, n_ctx * Lc, w_e1, w_e3, w_e2, l)
        x = _combine(tok, y, x, modt, 5, 0, lambda b: b)
        if not last:
            ctx = _combine(tok, y, ctx, modt, 5, T, lambda b: B)
    return x
```

```python
import functools
import math

import numpy as np
import jax
import jax.numpy as jnp
from jax import lax
from jax.experimental import pallas as pl
from jax.experimental.pallas import tpu as pltpu

F32 = jnp.float32
BF16 = jnp.bfloat16

GRID_W = 64
HEAD_DIM = 128
BRANCH_W = 512
N_BRANCH = 4
HY_W = 512
HY_ORDER = 2
HY_BANDS = 16
HY_EMB = 1 + 2 * HY_BANDS
HY_FAST_DECAY = 0.3
HY_SLOW_DECAY = 1.5
HY_TARGET = 1e-2
GQA_HEADS = 4
GQA_KV = 2
NA_HEADS = 4
NA_WIN_H = 8
NA_WIN_W = 16
DF_HEADS = 4
DF_QK_DIM = 64
N_EXPERTS = 16
CAPACITY_FACTOR = 2
ROPE_THETA = 10000.0
NORM_EPS = 1e-6
ATT_SCALE = HEAD_DIM ** -0.5
DF_SCALE = DF_QK_DIM ** -0.5
NEG_INF = -1e30
LOG2E = math.log2(math.e)

LANE = 128
ROW_TILE = 256
PROJ_TILE_N = 512
NA_GROUP_ROWS = 4
NA_WIN_ROWS = NA_GROUP_ROWS + NA_WIN_H
DFT_HALF = 256
VMEM_LIMIT =56 * 1024 * 1024

C_HY = 0
C_QB = 3 * HY_W
C_KB = C_QB + GQA_HEADS * HEAD_DIM
C_VB = C_KB + GQA_KV * HEAD_DIM
C_QC = C_VB + GQA_KV * HEAD_DIM
C_KC = C_QC + NA_HEADS * HEAD_DIM
C_VC = C_KC + NA_HEADS * HEAD_DIM
C_QD = C_VC + NA_HEADS * HEAD_DIM
C_KD = C_QD + DF_HEADS * 2 * DF_QK_DIM
C_VD = C_KD + DF_HEADS * 2 * DF_QK_DIM
C_GATES = C_VD + DF_HEADS * HEAD_DIM


def _cp(*sem):
    return pltpu.CompilerParams(dimension_semantics=sem, vmem_limit_bytes=VMEM_LIMIT)


def _dot(a, b):
    return jnp.dot(a, b, preferred_element_type=F32)


def _dot_nt(a, b):
    return lax.dot_general(a, b, (((1,), (1,)), ((), ())), preferred_element_type=F32)


def _dot_hi(a, b):
    return jnp.dot(a, b, preferred_element_type=F32, precision=lax.Precision.HIGHEST)


def _mod_kernel(c_ref, w_ref, b_ref, o_ref):
    c = c_ref[...]
    a = (c * jax.nn.sigmoid(c)).astype(BF16)
    o_ref[...] = _dot(a, w_ref[...].astype(BF16)) + b_ref[...]


def _modulation(cc, w, b, l):
    depth, D, N = w.shape
    tn = math.gcd(N, 1024)
    return pl.pallas_call(
        _mod_kernel,
        grid=(N // tn,),
        in_specs=[pl.BlockSpec((8, D), lambda j: (0, 0)),
                  pl.BlockSpec((None, D, tn), lambda j: (l, 0, j)),
                  pl.BlockSpec((None, 1, tn), lambda j: (l, 0, j))],
        out_specs=pl.BlockSpec((8, tn), lambda j: (0, j)),
        out_shape=jax.ShapeDtypeStruct((8, N), F32),
        compiler_params=_cp("arbitrary"),
        name="modulation",
    )(cc, w, b.reshape(depth, 1, N))


def _prenorm_body(x, g_ref, sc_ref, sh_ref):
    y = x * lax.rsqrt(jnp.mean(x * x, axis=-1, keepdims=True) + NORM_EPS)
    return y * g_ref[...] * (1.0 + sc_ref[...]) + sh_ref[...]


def _prenorm_kernel(x_ref, c_ref, g_ref, sc_ref, sh_ref, o_ref, *, n_lat):
    i = pl.program_id(1)

    @pl.when(i < n_lat)
    def _():
        o_ref[...] = _prenorm_body(x_ref[...], g_ref, sc_ref, sh_ref).astype(BF16)

    @pl.when(i >= n_lat)
    def _():
        o_ref[...] = _prenorm_body(c_ref[...], g_ref, sc_ref, sh_ref).astype(BF16)


def _prenorm_router_kernel(x_ref, c_ref, g_ref, sc_ref, sh_ref, wr_ref, o_ref, a_ref, *, n_lat, n_exp):
    i = pl.program_id(1)

    def run(x):
        h = _prenorm_body(x, g_ref, sc_ref, sh_ref)
        o_ref[...] = h.astype(BF16)
        logits = _dot_hi(h, wr_ref[...])
        lane = lax.broadcasted_iota(jnp.int32, logits.shape, 1)
        logits = jnp.where(lane < n_exp, logits, NEG_INF)
        e = jnp.exp(logits - jnp.max(logits, axis=-1, keepdims=True))
        a_ref[...] = e / jnp.sum(e, axis=-1, keepdims=True)

    @pl.when(i < n_lat)
    def _():
        run(x_ref[...])

    @pl.when(i >= n_lat)
    def _():
        run(c_ref[...])


def _prenorm(x, ctx, gain, modt, q_scale, q_shift, n_ctx_tiles, w_router=None):
    B, T, D = x.shape
    n_lat = T // ROW_TILE
    nt = n_lat + n_ctx_tiles
    S = nt * ROW_TILE

    def mod_map(q):
        return lambda b, i: (jnp.where(i < n_lat, b, B), q, 0, 0)

    in_specs = [
        pl.BlockSpec((None, ROW_TILE, D), lambda b, i: (b, jnp.minimum(i, n_lat - 1), 0)),
        pl.BlockSpec((None, ROW_TILE, D), lambda b, i: (b, jnp.maximum(i - n_lat, 0), 0)),
        pl.BlockSpec((1, D), lambda b, i: (0, 0)),
        pl.BlockSpec((None, None, 1, D), mod_map(q_scale)),
        pl.BlockSpec((None, None, 1, D), mod_map(q_shift)),
    ]
    out_h = pl.BlockSpec((None, ROW_TILE, D), lambda b, i: (b, i, 0))
    shape_h = jax.ShapeDtypeStruct((B, S, D), BF16)
    args = [x, ctx, gain.reshape(1, D), modt, modt]
    if w_router is None:
        return pl.pallas_call(
            functools.partial(_prenorm_kernel, n_lat=n_lat),
            grid=(B, nt), in_specs=in_specs, out_specs=out_h, out_shape=shape_h,
            compiler_params=_cp("parallel", "arbitrary"), name="prenorm",
        )(*args)
    n_exp = w_router.shape[1]
    wr = jnp.pad(w_router, ((0, 0), (0, LANE - n_exp)))
    return pl.pallas_call(
        functools.partial(_prenorm_router_kernel, n_lat=n_lat, n_exp=n_exp),
        grid=(B, nt),
        in_specs=in_specs + [pl.BlockSpec((D, LANE), lambda b, i: (0, 0))],
        out_specs=[out_h, pl.BlockSpec((None, ROW_TILE, LANE), lambda b, i: (b, i, 0))],
        out_shape=[shape_h, jax.ShapeDtypeStruct((B, S, LANE), F32)],
        compiler_params=_cp("parallel", "arbitrary"), name="prenorm_router",
    )(*args, wr)


def _mm_kernel(a_ref, b_ref, o_ref, *, sigmoid):
    acc = _dot(a_ref[...], b_ref[...].astype(BF16))
    if sigmoid:
        acc = jax.nn.sigmoid(acc)
    o_ref[...] = acc.astype(o_ref.dtype)


def _matmul(a, b, l, col0, N, out_dtype, sigmoid=False, name="matmul"):
    M, K = a.shape
    tm, tn = math.gcd(M, 1024), PROJ_TILE_N
    cb = col0 // tn
    return pl.pallas_call(
        functools.partial(_mm_kernel, sigmoid=sigmoid),
        grid=(M // tm, N // tn),
        in_specs=[pl.BlockSpec((tm, K), lambda i, j: (i, 0)),
                  pl.BlockSpec((None, K, tn), lambda i, j: (l, 0, cb + j))],
        out_specs=pl.BlockSpec((tm, tn), lambda i, j: (i, j)),
        out_shape=jax.ShapeDtypeStruct((M, N), out_dtype),
        compiler_params=_cp("parallel", "arbitrary"), name=name,
    )(a, b)


def _qk_post(x, g, seg, tables):
    lane = lax.broadcasted_iota(jnp.int32, (1, LANE), 1)
    sq = x * x
    if seg == LANE:
        ms = jnp.mean(sq, axis=-1, keepdims=True)
    else:
        lo = jnp.sum(jnp.where(lane < seg, sq, 0.0), axis=-1, keepdims=True)
        hi = jnp.sum(jnp.where(lane >= seg, sq, 0.0), axis=-1, keepdims=True)
        ms = jnp.where(lane < seg, lo, hi) * (1.0 / seg)
    y = x * lax.rsqrt(ms + NORM_EPS) * g
    if tables is not None:
        cos_ref, sin_ref = tables
        q = seg // 4
        partner = jnp.where((lane % (seg // 2)) < q, pltpu.roll(y, LANE - q, 1), pltpu.roll(y, q, 1))
        y = y * cos_ref[...] + partner * sin_ref[...]
    return y


def _proj_block_kinds():
    kinds = []
    for col in range(0, C_GATES, LANE):
        if C_QB <= col < C_VB:
            kinds.append((HEAD_DIM, True))
        elif C_QC <= col < C_VC:
            kinds.append((HEAD_DIM, False))
        elif C_QD <= col < C_VD:
            kinds.append((DF_QK_DIM, True))
        else:
            kinds.append(None)
    return kinds


def _inproj_kernel(a_ref, w_ref, g_ref, c128_ref, s128_ref, c64_ref, s64_ref, o_ref):
    j = pl.program_id(1)
    acc = _dot(a_ref[...], w_ref[...].astype(BF16))
    per_tile = PROJ_TILE_N // LANE
    kinds = _proj_block_kinds()
    tiles = [kinds[t * per_tile:(t + 1) * per_tile] for t in range(len(kinds) // per_tile)]
    plain = functools.reduce(jnp.logical_or, [j == t for t, ks in enumerate(tiles) if not any(ks)])

    @pl.when(plain)
    def _():
        o_ref[...] = acc.astype(o_ref.dtype)

    for t, ks in enumerate(tiles):
        if not any(ks):
            continue

        @pl.when(j == t)
        def _(ks=ks):
            for h, kind in enumerate(ks):
                y = acc[:, h * LANE:(h + 1) * LANE]
                if kind is not None:
                    seg, rope = kind
                    tables = None if not rope else ((c128_ref, s128_ref) if seg == HEAD_DIM else (c64_ref, s64_ref))
                    y = _qk_post(y, g_ref[h:h + 1, :], seg, tables)
                o_ref[:, h * LANE:(h + 1) * LANE] = y.astype(o_ref.dtype)


def _in_proj(h2d, w_in_bf, l, gains, rope_h, rope_d):
    M, K = h2d.shape
    tm, tn = math.gcd(M, 1024), PROJ_TILE_N
    per_tile = tn // LANE
    table = pl.BlockSpec((tm, LANE), lambda i, j: (i, 0))
    return pl.pallas_call(
        _inproj_kernel,
        grid=(M // tm, C_GATES // tn),
        in_specs=[pl.BlockSpec((tm, K), lambda i, j: (i, 0)),
                  pl.BlockSpec((None, K, tn), lambda i, j: (l, 0, j)),
                  pl.BlockSpec((None, per_tile, LANE), lambda i, j: (j, 0, 0)),
                  table, table, table, table],
        out_specs=pl.BlockSpec((tm, tn), lambda i, j: (i, j)),
        out_shape=jax.ShapeDtypeStruct((M, C_GATES), BF16),
        compiler_params=_cp("parallel", "arbitrary"), name="in_proj",
    )(h2d, w_in_bf, gains.reshape(-1, per_tile, LANE), *rope_h, *rope_d)


def _proj_gains(qn_b, kn_b, qn_c, kn_c, qn_d, kn_d):
    ones = lambda n: jnp.ones((n, LANE), F32)
    rep = lambda g, n: jnp.tile(jnp.tile(g.astype(F32), LANE // g.shape[0])[None, :], (n, 1))
    return jnp.concatenate([
        ones(C_QB // LANE), rep(qn_b, GQA_HEADS), rep(kn_b, GQA_KV), ones(GQA_KV),
        rep(qn_c, NA_HEADS), rep(kn_c, NA_HEADS), ones(NA_HEADS),
        rep(qn_d, DF_HEADS), rep(kn_d, DF_HEADS), ones(DF_HEADS)], axis=0)


def _rope_tables(B, T, S, seg):
    half = seg // 2
    nfreq = half // 2
    inv = ROPE_THETA ** (-jnp.arange(0, half, 2, dtype=F32) / half)
    pos = jnp.arange(T, dtype=jnp.int32)
    rows, cols = (pos // GRID_W).astype(F32), (pos % GRID_W).astype(F32)
    l = np.arange(LANE) % seg
    use_col = l >= half
    fidx = (l % half) % nfreq
    is_b = (l % half) >= nfreq
    ang = jnp.where(use_col[None, :], cols[:, None], rows[:, None]) * inv[fidx][None, :]
    cos, sin = jnp.cos(ang), jnp.sin(ang)
    sin = jnp.where(is_b[None, :], sin, -sin)
    cos = jnp.concatenate([cos, jnp.ones((S - T, LANE), F32)], axis=0)
    sin = jnp.concatenate([sin, jnp.zeros((S - T, LANE), F32)], axis=0)
    return jnp.tile(cos, (B, 1)), jnp.tile(sin, (B, 1))


def _softmax_rows(s, scale):
    c = scale * LOG2E
    m = jnp.max(s, axis=-1, keepdims=True)
    p = jnp.exp2(s * c - m * c)
    return p.astype(BF16), jnp.sum(p, axis=-1, keepdims=True)


def _attend(qs, ks, vs, scale):
    scores = [_dot_nt(q, k) for q, k in zip(qs, ks)]
    outs = []
    for s, v in zip(scores, vs):
        p, l = _softmax_rows(s, scale)
        outs.append(_dot(p, v) / l)
    return outs


def _attend_t(qts, ks, vts, scale):
    c = scale * LOG2E
    scores = [_dot(k, qt) for qt, k in zip(qts, ks)]
    outs = []
    for s, vt in zip(scores, vts):
        m = jnp.max(s, axis=0, keepdims=True)
        p = jnp.exp2(s * c - m * c)
        l = jnp.sum(p, axis=0, keepdims=True)
        outs.append(_dot(vt, p.astype(BF16)) / l)
    return outs


def _store_vt(vt_ref, v):
    for h in range(vt_ref.shape[0]):
        vt_ref[h] = v[:, h * LANE:(h + 1) * LANE].T


GQA_KV_PER_STEP = 2


def _gqa_kernel(q_ref, k_ref, v_ref, o_ref, vt_ref, *, T, R, scale):
    i = pl.program_id(2)

    @pl.when(i == 0)
    def _():
        _store_vt(vt_ref, v_ref[...])

    def attend(lo):
        n_heads = q_ref.shape[1] // LANE
        qts = [q_ref[:, h * LANE:(h + 1) * LANE].T for h in range(n_heads)]
        ks = [k_ref[lo:, (h // R) * LANE:(h // R + 1) * LANE] for h in range(n_heads)]
        outs = _attend_t(qts, ks, [vt_ref[h // R, :, lo:] for h in range(n_heads)], scale)
        for h in range(n_heads):
            o_ref[:, h * LANE:(h + 1) * LANE] = outs[h].T.astype(o_ref.dtype)

    @pl.when(i < T // ROW_TILE)
    def _():
        attend(0)

    @pl.when(i >= T // ROW_TILE)
    def _():
        attend(T)


def _gqa(proj, T, Tq, scale):
    B, S, _ = proj.shape
    R = GQA_HEADS // GQA_KV
    G = GQA_KV_PER_STEP
    qb, kb, vb = C_QB // (G * R * LANE), C_KB // (G * LANE), C_VB // (G * LANE)
    return pl.pallas_call(
        functools.partial(_gqa_kernel, T=T, R=R, scale=scale),
        grid=(B, GQA_KV // G, Tq // ROW_TILE),
        in_specs=[pl.BlockSpec((None, ROW_TILE, G * R * LANE), lambda b, g, i: (b, i, qb + g)),
                  pl.BlockSpec((None, S, G * LANE), lambda b, g, i: (b, 0, kb + g)),
                  pl.BlockSpec((None, S, G * LANE), lambda b, g, i: (b, 0, vb + g))],
        out_specs=pl.BlockSpec((None, ROW_TILE, G * R * LANE), lambda b, g, i: (b, i, g)),
        out_shape=jax.ShapeDtypeStruct((B, Tq, GQA_HEADS * LANE), BF16),
        scratch_shapes=[pltpu.VMEM((G, LANE, S), BF16)],
        compiler_params=_cp("parallel", "parallel", "arbitrary"), name="gqa",
    )(proj, proj, proj)


DF_HEADS_PER_STEP = 2


def _diff_kernel(lam_ref, q_ref, k_ref, v_ref, g_ref, o_ref, vt_ref, *, T, scale, out_scale):
    i = pl.program_id(2)
    lam = lam_ref[0]

    @pl.when(i == 0)
    def _():
        _store_vt(vt_ref, v_ref[...])

    def attend(lo):
        qts, ks, vts = [], [], []
        for h in range(DF_HEADS_PER_STEP):
            qt = q_ref[:, h * LANE:(h + 1) * LANE].T
            ch = lax.broadcasted_iota(jnp.int32, qt.shape, 0)
            zero = jnp.zeros_like(qt)
            qts += [jnp.where(ch < DF_QK_DIM, qt, zero), jnp.where(ch >= DF_QK_DIM, qt, zero)]
            ks += [k_ref[lo:, h * LANE:(h + 1) * LANE]] * 2
            vts += [vt_ref[h, :, lo:]] * 2
        outs = _attend_t(qts, ks, vts, scale)
        for h in range(DF_HEADS_PER_STEP):
            o = (outs[2 * h] - lam * outs[2 * h + 1]).T
            y = o * lax.rsqrt(jnp.mean(o * o, axis=-1, keepdims=True) + NORM_EPS) * g_ref[...]
            o_ref[:, h * LANE:(h + 1) * LANE] = (y * out_scale).astype(o_ref.dtype)

    @pl.when(i < T // ROW_TILE)
    def _():
        attend(0)

    @pl.when(i >= T // ROW_TILE)
    def _():
        attend(T)


def _diff_attn(lam, proj, subln, T, Tq, out_scale):
    B, S, _ = proj.shape
    W = DF_HEADS_PER_STEP * LANE
    qb, kb, vb = C_QD // W, C_KD // W, C_VD // W
    return pl.pallas_call(
        functools.partial(_diff_kernel, T=T, scale=DF_SCALE, out_scale=out_scale),
        grid=(B, DF_HEADS // DF_HEADS_PER_STEP, Tq // ROW_TILE),
        in_specs=[pl.BlockSpec(memory_space=pltpu.SMEM),
                  pl.BlockSpec((None, ROW_TILE, W), lambda b, h, i: (b, i, qb + h)),
                  pl.BlockSpec((None, S, W), lambda b, h, i: (b, 0, kb + h)),
                  pl.BlockSpec((None, S, W), lambda b, h, i: (b, 0, vb + h)),
                  pl.BlockSpec((1, LANE), lambda b, h, i: (0, 0))],
        out_specs=pl.BlockSpec((None, ROW_TILE, W), lambda b, h, i: (b, i, h)),
        out_shape=jax.ShapeDtypeStruct((B, Tq, DF_HEADS * LANE), BF16),
        scratch_shapes=[pltpu.VMEM((DF_HEADS_PER_STEP, LANE, S), BF16)],
        compiler_params=_cp("parallel", "parallel", "arbitrary"), name="diff_attn",
    )(lam.reshape(1).astype(F32), proj, proj, proj, subln.astype(F32).reshape(1, LANE))


def _na_kernel(q_ref, k_ref, v_ref, bias_ref, o_ref, *, T, scale):
    i = pl.program_id(2)
    n_groups = T // ROW_TILE
    grid_rows = T // GRID_W
    q = q_ref[...]
    kc, vc = k_ref[T:, :], v_ref[T:, :]
    s_c = _dot_nt(q, kc)

    @pl.when(i < n_groups)
    def _():
        row0 = jnp.clip(i * NA_GROUP_ROWS - NA_WIN_H // 2, 0, grid_rows - NA_WIN_ROWS)
        start = pl.multiple_of(row0 * GRID_W, ROW_TILE)
        kw = k_ref[pl.ds(start, NA_WIN_ROWS * GRID_W), :]
        vw = v_ref[pl.ds(start, NA_WIN_ROWS * GRID_W), :]
        c = scale * LOG2E
        u_n = _dot_nt(q, kw) * c + bias_ref[...]
        u_c = s_c * c
        m = jnp.maximum(jnp.max(u_n, axis=-1, keepdims=True), jnp.max(u_c, axis=-1, keepdims=True))
        p_n, p_c = jnp.exp2(u_n - m), jnp.exp2(u_c - m)
        l = jnp.sum(p_n, axis=-1, keepdims=True) + jnp.sum(p_c, axis=-1, keepdims=True)
        o = (_dot(p_n.astype(BF16), vw) + _dot(p_c.astype(BF16), vc)) / l
        o_ref[...] = o.astype(o_ref.dtype)

    @pl.when(i >= n_groups)
    def _():
        p, l = _softmax_rows(s_c, scale)
        o_ref[...] = (_dot(p, vc) / l).astype(o_ref.dtype)


def _na_bias(rpb, T):
    rows = T // GRID_W
    n_groups = rows // NA_GROUP_ROWS
    n_roff, n_coff = 2 * NA_WIN_H - 1, 2 * NA_WIN_W - 1
    c = np.arange(GRID_W)[:, None]
    kc = np.arange(GRID_W)[None, :]
    cs = np.clip(c - NA_WIN_W // 2, 0, GRID_W - NA_WIN_W)
    col_valid = (kc >= cs) & (kc < cs + NA_WIN_W)
    coff = np.clip(kc - c + NA_WIN_W - 1, 0, n_coff - 1)
    col_sel = (coff[..., None] == np.arange(n_coff)).astype(np.float32)
    rpb = rpb.astype(F32) * LOG2E
    out = []
    for grp in (0, 1, n_groups - 1):
        r = grp * NA_GROUP_ROWS + np.arange(NA_GROUP_ROWS)[:, None]
        rs = np.clip(r - NA_WIN_H // 2, 0, rows - NA_WIN_H)
        row0 = np.clip(grp * NA_GROUP_ROWS - NA_WIN_H // 2, 0, rows - NA_WIN_ROWS)
        key_row = row0 + np.arange(NA_WIN_ROWS)[None, :]
        row_valid = (key_row >= rs) & (key_row < rs + NA_WIN_H)
        roff = np.clip(key_row - r + NA_WIN_H - 1, 0, n_roff - 1)
        row_sel = (roff[..., None] == np.arange(n_roff)).astype(np.float32)
        bias = jnp.einsum('ika,hab,cqb->hickq', row_sel, rpb, col_sel, precision=lax.Precision.HIGHEST)
        valid = row_valid[:, None, :, None] & col_valid[None, :, None, :]
        out.append(jnp.where(valid[None], bias, NEG_INF).reshape(rpb.shape[0], ROW_TILE, -1))
    return jnp.stack(out)


def _na(proj, bias, T, Tq):
    B, S, _ = proj.shape
    qb, kb, vb = C_QC // LANE, C_KC // LANE, C_VC // LANE
    n_groups = T // ROW_TILE
    nk = NA_WIN_ROWS * GRID_W

    def bias_map(b, h, i):
        return (jnp.where(i == 0, 0, jnp.where(i >= n_groups - 1, 2, 1)), h, 0, 0)

    return pl.pallas_call(
        functools.partial(_na_kernel, T=T, scale=ATT_SCALE),
        grid=(B, NA_HEADS, Tq // ROW_TILE),
        in_specs=[pl.BlockSpec((None, ROW_TILE, LANE), lambda b, h, i: (b, i, qb + h)),
                  pl.BlockSpec((None, S, LANE), lambda b, h, i: (b, 0, kb + h)),
                  pl.BlockSpec((None, S, LANE), lambda b, h, i: (b, 0, vb + h)),
                  pl.BlockSpec((None, None, ROW_TILE, nk), bias_map)],
        out_specs=pl.BlockSpec((None, ROW_TILE, LANE), lambda b, h, i: (b, i, h)),
        out_shape=jax.ShapeDtypeStruct((B, Tq, NA_HEADS * LANE), BF16),
        compiler_params=_cp("parallel", "parallel", "arbitrary"), name="na_attn",
    )(proj, proj, proj, bias)


def _shortconv_kernel(*refs, T, with_ctx):
    u_refs, w_refs, o_refs = refs[0:3], refs[3:6], refs[6:]
    S = u_refs[0].shape[0]
    row = lax.broadcasted_iota(jnp.int32, (S, 1), 0)
    first = (row == 0) | (row == T)
    last = (row == T - 1) | (row == S - 1)
    for n in range(3):
        u = u_refs[n][...].astype(F32)
        w = w_refs[n][...]
        prev = jnp.where(first, 0.0, pltpu.roll(u, 1, 0))
        nxt = jnp.where(last, 0.0, pltpu.roll(u, S - 1, 0))
        y = (prev * w[0:1] + u * w[1:2] + nxt * w[2:3]).astype(BF16)
        o_refs[n][...] = y[:T]
        if with_ctx:
            o_refs[3 + n][...] = y[T:]


def _shortconv(proj, w, T, with_ctx):
    B, S, _ = proj.shape
    Lc = S - T
    nct = HY_W // LANE
    in_specs = [pl.BlockSpec((None, S, LANE), functools.partial(lambda b, c, n: (b, 0, n * nct + c), n=n))
                for n in range(3)]
    in_specs += [pl.BlockSpec((3, LANE), functools.partial(lambda b, c, n: (0, n * nct + c), n=n))
                 for n in range(3)]
    out_specs = [pl.BlockSpec((T, LANE), lambda b, c: (0, b * nct + c))] * 3
    out_shape = [jax.ShapeDtypeStruct((T, B * HY_W), BF16)] * 3
    if with_ctx:
        out_specs += [pl.BlockSpec((Lc, LANE), lambda b, c: (0, b * nct + c))] * 3
        out_shape += [jax.ShapeDtypeStruct((Lc, B * HY_W), BF16)] * 3
    return pl.pallas_call(
        functools.partial(_shortconv_kernel, T=T, with_ctx=with_ctx),
        grid=(B, nct), in_specs=in_specs, out_specs=out_specs, out_shape=out_shape,
        compiler_params=_cp("parallel", "arbitrary"), name="hy_shortconv",
    )(proj, proj, proj, w, w, w)


def _filter_hidden_kernel(z_ref, f1_ref, b1_ref, fr_ref, f2_ref, b2_ref, o_ref):
    fr = fr_ref[...]
    hid = jnp.sin(fr * (_dot_hi(z_ref[...], f1_ref[...]) + b1_ref[...]))
    o_ref[...] = jnp.sin(fr * (_dot_hi(hid, f2_ref[...]) + b2_ref[...]))


def _filter_kernel(h_ref, hb_ref, t_ref, tb_ref, f3f_ref, f3b_ref, dl_ref, *o_refs, circular):
    fw = _dot_hi(h_ref[...], f3f_ref[...]) * jnp.exp(-t_ref[...] * dl_ref[...])
    bw = _dot_hi(hb_ref[...], f3b_ref[...]) * jnp.exp(-tb_ref[...] * dl_ref[...])
    row = lax.broadcasted_iota(jnp.int32, bw.shape, 0)
    bw = jnp.where(row == 0, 0.0, bw)
    inv = 1.0 / (jnp.sum(jnp.abs(fw), axis=0, keepdims=True) + jnp.sum(jnp.abs(bw), axis=0, keepdims=True))
    if circular:
        o_refs[0][0] = (fw * inv).astype(BF16)
        o_refs[0][1] = (bw * inv).astype(BF16)
    else:
        o_refs[0][...] = ((fw + bw) * inv).astype(BF16)
        o_refs[1][...] = ((fw - bw) * inv).astype(BF16)


def _hyena_filters(L, f1, b1, freq, f2, b2, f3, circular):
    t = jnp.linspace(0.0, 1.0, L, dtype=F32)[:, None]
    w = (2.0 * math.pi / L) * jnp.arange(L, dtype=F32)[:, None]
    bands = jnp.linspace(1e-4, HY_BANDS - 1, HY_BANDS, dtype=F32)[None, :]
    z = jnp.concatenate([t, jnp.cos(bands * w), -jnp.sin(bands * w)], axis=-1)
    z = jnp.pad(z, ((0, 0), (0, LANE - HY_EMB)))
    f1p = jnp.pad(f1.astype(F32), ((0, LANE - HY_EMB), (0, 0)))
    ffn = f1.shape[1]
    small = lambda shape: pl.BlockSpec(shape, lambda j: (0, 0))
    hid = pl.pallas_call(
        _filter_hidden_kernel,
        grid=(1,),
        in_specs=[small((L, LANE)), small((LANE, ffn)), small((1, ffn)), small((1, ffn)), small((ffn, ffn)),
                  small((1, ffn))],
        out_specs=small((L, ffn)), out_shape=jax.ShapeDtypeStruct((L, ffn), F32),
        compiler_params=_cp("arbitrary"), name="hy_filter_hidden",
    )(z, f1p, b1.astype(F32).reshape(1, ffn), freq.astype(F32).reshape(1, ffn), f2.astype(F32),
      b2.astype(F32).reshape(1, ffn))
    back = (lambda a: jnp.roll(jnp.flip(a, axis=0), 1, axis=0)) if circular else (lambda a: a)
    deltas = jnp.abs(jnp.linspace(math.log(HY_TARGET) / HY_SLOW_DECAY, math.log(HY_TARGET) / HY_FAST_DECAY,
                                  HY_W, dtype=F32))
    NC = HY_ORDER * HY_W
    dl = jnp.tile(deltas, HY_ORDER).reshape(1, NC)
    tn = 256
    if circular:
        out_specs = [pl.BlockSpec((2, L, tn), lambda j: (0, 0, j))]
        out_shape = [jax.ShapeDtypeStruct((2, L, NC), BF16)]
    else:
        out_specs = [pl.BlockSpec((L, tn), lambda j: (0, j))] * 2
        out_shape = [jax.ShapeDtypeStruct((L, NC), BF16)] * 2
    out = pl.pallas_call(
        functools.partial(_filter_kernel, circular=circular),
        grid=(NC // tn,),
        in_specs=[small((L, ffn)), small((L, ffn)), small((L, 1)), small((L, 1)),
                  pl.BlockSpec((ffn, tn), lambda j: (0, j)),
                  pl.BlockSpec((ffn, tn), lambda j: (0, NC // tn + j)),
                  pl.BlockSpec((1, tn), lambda j: (0, j))],
        out_specs=out_specs, out_shape=out_shape,
        compiler_params=_cp("arbitrary"), name="hy_filter",
    )(hid, back(hid), t, back(t), f3.astype(F32), f3.astype(F32), dl)
    return out[0].reshape(2 * L, NC) if circular else out


def _dft_tables(L):
    N = 2 * L
    r = np.arange(N)
    k = (r // (2 * DFT_HALF)) * DFT_HALF + r % DFT_HALF
    is_im = (r // DFT_HALF) % 2 == 1
    nyq = is_im & (k == 0)
    kj = jnp.asarray(k, jnp.int32)[:, None]
    im, nyq = is_im[:, None], nyq[:, None]
    n_lo = 64

    def cos_sin(step, count):
        m = (kj * (step * jnp.arange(count, dtype=jnp.int32))[None, :]) % N
        ang = m.astype(F32) * (2.0 * math.pi / N)
        return jnp.cos(ang), jnp.sin(ang)

    c1, s1 = cos_sin(n_lo, L // n_lo)
    c0, s0 = cos_sin(1, n_lo)
    p1 = jnp.where(nyq, 1.0, jnp.where(im, -s1, c1))
    q1 = jnp.where(nyq, 0.0, jnp.where(im, -c1, -s1))
    p0 = jnp.where(nyq, (1 - 2 * (jnp.arange(n_lo) % 2)).astype(F32)[None, :], c0)
    a = (p1[:, :, None] * p0[:, None, :] + q1[:, :, None] * s0[:, None, :]).reshape(N, L).astype(BF16)
    return a, a.T


def _kf_kernel(a_ref, hs_ref, hd_ref, o_ref, *, n_fft):
    i = pl.program_id(0)
    H = DFT_HALF
    re = _dot(a_ref[:H, :], hs_ref[...])
    im = _dot(a_ref[H:, :], hd_ref[...])
    o_ref[:H, :] = re * (2.0 / n_fft)
    o_ref[H:, :] = im * (2.0 / n_fft)

    @pl.when(i == 0)
    def _():
        ny = _dot(a_ref[H:H + 16, :], hs_ref[...])
        o_ref[0:1, :] = re[0:1] * (1.0 / n_fft)
        o_ref[H:H + 1, :] = ny[0:1] * (1.0 / n_fft)


def _filter_spectrum(a, hs, hd):
    N, L = a.shape
    NC = hs.shape[1]
    tm, tn = 2 * DFT_HALF, 512
    return pl.pallas_call(
        functools.partial(_kf_kernel, n_fft=N),
        grid=(N // tm, NC // tn),
        in_specs=[pl.BlockSpec((tm, L), lambda i, j: (i, 0)),
                  pl.BlockSpec((L, tn), lambda i, j: (0, j)),
                  pl.BlockSpec((L, tn), lambda i, j: (0, j))],
        out_specs=pl.BlockSpec((tm, tn), lambda i, j: (i, j)),
        out_shape=jax.ShapeDtypeStruct((N, NC), F32),
        compiler_params=_cp("parallel", "arbitrary"), name="hy_filter_spectrum",
    )(a, hs, hd)


def _fwd_kernel(a_ref, z_ref, kf_ref, p_ref):
    i = pl.program_id(0)
    H = DFT_HALF
    acc = _dot(a_ref[...], z_ref[...])
    zr, zi = acc[:H], acc[H:]
    kr, ki = kf_ref[:H, :], kf_ref[H:, :]
    row = lax.broadcasted_iota(jnp.int32, zr.shape, 0)
    real_pair = (row == 0) & (i == 0)
    p_ref[:H, :] = jnp.where(real_pair, zr * kr, zr * kr - zi * ki).astype(BF16)
    p_ref[H:, :] = jnp.where(real_pair, zi * ki, zr * ki + zi * kr).astype(BF16)


def _dft_multiply(a, z, kf, order):
    N, L = a.shape
    NB = z.shape[1] // HY_W
    tm = 2 * DFT_HALF
    return pl.pallas_call(
        _fwd_kernel,
        grid=(N // tm, NB),
        in_specs=[pl.BlockSpec((tm, L), lambda i, j: (i, 0)),
                  pl.BlockSpec((L, HY_W), lambda i, j: (0, j)),
                  pl.BlockSpec((tm, HY_W), lambda i, j: (i, order))],
        out_specs=pl.BlockSpec((tm, HY_W), lambda i, j: (i, j)),
        out_shape=jax.ShapeDtypeStruct((N, NB * HY_W), BF16),
        compiler_params=_cp("parallel", "arbitrary"), name="hy_dft_multiply",
    )(a, z, kf)


def _inv_kernel(at_ref, p_ref, x_ref, z_ref, b_ref, o_ref):
    y = _dot(at_ref[...], p_ref[...])
    z = z_ref[...].astype(F32)
    o_ref[...] = (x_ref[...].astype(F32) * (y + b_ref[...] * z)).astype(o_ref.dtype)


def _idft_gate(at, p, xo, z, bias):
    L, N = at.shape
    NB = z.shape[1] // HY_W
    tm = min(L, 512)
    return pl.pallas_call(
        _inv_kernel,
        grid=(L // tm, NB),
        in_specs=[pl.BlockSpec((tm, N), lambda i, j: (i, 0)),
                  pl.BlockSpec((N, HY_W), lambda i, j: (0, j)),
                  pl.BlockSpec((tm, HY_W), lambda i, j: (i, j)),
                  pl.BlockSpec((tm, HY_W), lambda i, j: (i, j)),
                  pl.BlockSpec((1, HY_W), lambda i, j: (0, 0))],
        out_specs=pl.BlockSpec((tm, HY_W), lambda i, j: (i, j)),
        out_shape=jax.ShapeDtypeStruct((L, NB * HY_W), BF16),
        compiler_params=_cp("parallel", "arbitrary"), name="hy_idft_gate",
    )(at, p, xo, z, bias.astype(F32).reshape(1, HY_W))


def _hyena(vxx, tables, filt, hy_bias):
    a, at = tables
    kf = _filter_spectrum(a, *filt)
    z = vxx[0]
    for o in range(HY_ORDER):
        p = _dft_multiply(a, z, kf, o)
        z = _idft_gate(at, p, vxx[1 + o], z, hy_bias[o])
    return z


FFT_N2 = LANE
FFT_K1B = 8
FFT_LANES = 8192


def _fft_tables(L):
    N = 2 * L
    N1 = N // FFT_N2
    KH = -(-(N1 // 2 + 1) // FFT_K1B) * FFT_K1B
    k1 = jnp.arange(KH, dtype=jnp.int32)
    n1 = jnp.arange(N1, dtype=jnp.int32)
    ang1 = ((k1[:, None] * n1[None, :]) % N1).astype(F32) * (2.0 * math.pi / N1)
    c1, s1 = jnp.cos(ang1), jnp.sin(ang1)
    f1 = jnp.concatenate([c1, -s1], axis=0).astype(BF16)
    mult = jnp.where((k1 == 0) | (k1 == N1 // 2), 1.0, jnp.where(k1 < N1 // 2, 2.0, 0.0))[:, None]
    g2 = jnp.concatenate([c1 * mult, -s1 * mult], axis=0)[:, :N1 // 2].T.astype(BF16)
    k2 = jnp.arange(FFT_N2, dtype=jnp.int32)
    kk = k1[:, None, None] + N1 * k2[None, :, None]
    th = ((kk * k2[None, None, :]) % N).astype(F32) * (2.0 * math.pi / N)
    mr, mi = jnp.cos(th), -jnp.sin(th)
    wf = jnp.concatenate([jnp.concatenate([mr, -mi], axis=2),
                          jnp.concatenate([mi, mr], axis=2)], axis=1).astype(BF16)
    return f1, wf, jnp.swapaxes(wf, 1, 2), g2


def _fft_s1_kernel(f_ref, x_ref, o_ref):
    o_ref[...] = _dot(f_ref[...], x_ref[...]).astype(o_ref.dtype)


def _fft_stage1(f, x2d):
    R, K = f.shape
    M2 = x2d.shape[1]
    tn = math.gcd(M2, FFT_LANES)
    return pl.pallas_call(
        _fft_s1_kernel,
        grid=(M2 // tn,),
        in_specs=[pl.BlockSpec((R, K), lambda j: (0, 0)),
                  pl.BlockSpec((K, tn), lambda j: (0, j))],
        out_specs=pl.BlockSpec((R, tn), lambda j: (0, j)),
        out_shape=jax.ShapeDtypeStruct((R, M2), BF16),
        compiler_params=_cp("arbitrary"), name="hy_fft_stage1",
    )(f, x2d)


def _fft_s2_kernel(*refs, with_filter, scale):
    if with_filter:
        w_ref, a_ref, kf_ref, o_ref = refs
    else:
        w_ref, a_ref, o_ref = refs
    H = FFT_N2
    for kk in range(FFT_K1B):
        a = jnp.concatenate([a_ref[0, kk], a_ref[1, kk]], axis=0)
        x = _dot(w_ref[kk], a)
        xr, xi = x[:H], x[H:]
        if with_filter:
            kr, ki = kf_ref[0, kk], kf_ref[1, kk]
            xr, xi = xr * kr - xi * ki, xr * ki + xi * kr
        elif scale != 1.0:
            xr, xi = xr * scale, xi * scale
        o_ref[0, kk] = xr.astype(o_ref.dtype)
        o_ref[1, kk] = xi.astype(o_ref.dtype)


def _fft_stage2(w, a4, kf4=None, order=0, out_dtype=BF16, scale=1.0):
    _, KH, N2, cols = a4.shape
    tn = HY_W
    blk = lambda col: pl.BlockSpec((2, FFT_K1B, N2, tn), col)
    in_specs = [pl.BlockSpec((FFT_K1B, 2 * N2, 2 * N2), lambda i, j: (i, 0, 0)),
                blk(lambda i, j: (0, i, 0, j))]
    args = [w, a4]
    if kf4 is not None:
        in_specs.append(blk(lambda i, j: (0, i, 0, order)))
        args.append(kf4)
    return pl.pallas_call(
        functools.partial(_fft_s2_kernel, with_filter=kf4 is not None, scale=scale),
        grid=(KH // FFT_K1B, cols // tn),
        in_specs=in_specs,
        out_specs=blk(lambda i, j: (0, i, 0, j)),
        out_shape=jax.ShapeDtypeStruct(a4.shape, out_dtype),
        compiler_params=_cp("parallel", "arbitrary"), name="hy_fft_stage2",
    )(*args)


def _ifft_s2_kernel(g_ref, q_ref, x_ref, z_ref, b_ref, o_ref):
    y = _dot(g_ref[...], q_ref[...])
    z = z_ref[...].astype(F32)
    o_ref[...] = (x_ref[...].astype(F32) * (y + b_ref[...] * z)).astype(o_ref.dtype)


def _ifft_stage2_gate(g2, q4, xo, z, bias):
    _, KH, N2, cols = q4.shape
    H1 = g2.shape[0]
    L = z.shape[0]
    M2 = N2 * cols
    tn = math.gcd(M2, FFT_LANES)
    bias_t = jnp.tile(bias.astype(F32), tn // HY_W).reshape(1, tn)
    row = pl.BlockSpec((H1, tn), lambda j: (0, j))
    out = pl.pallas_call(
        _ifft_s2_kernel,
        grid=(M2 // tn,),
        in_specs=[pl.BlockSpec((H1, 2 * KH), lambda j: (0, 0)),
                  pl.BlockSpec((2 * KH, tn), lambda j: (0, j)),
                  row, row,
                  pl.BlockSpec((1, tn), lambda j: (0, 0))],
        out_specs=row,
        out_shape=jax.ShapeDtypeStruct((H1, M2), BF16),
        compiler_params=_cp("arbitrary"), name="hy_ifft_stage2_gate",
    )(g2, q4.reshape(2 * KH, M2), xo.reshape(H1, M2), z.reshape(H1, M2), bias_t)
    return out.reshape(L, cols)


def _hyena_fft(vxx, tables, kern, hy_bias):
    f1, wf, wi, g2 = tables
    L, cols = vxx[0].shape
    KH, N1 = f1.shape[0] // 2, f1.shape[1]
    N = 2 * L
    NC = kern.shape[1]
    ka = _fft_stage1(f1, kern.reshape(N1, FFT_N2 * NC)).reshape(2, KH, FFT_N2, NC)
    kf = _fft_stage2(wf, ka, out_dtype=F32, scale=1.0 / N)
    z = vxx[0]
    for o in range(HY_ORDER):
        a = _fft_stage1(f1[:, :N1 // 2], z.reshape(N1 // 2, FFT_N2 * cols)).reshape(2, KH, FFT_N2, cols)
        p = _fft_stage2(wf, a, kf4=kf, order=o)
        q = _fft_stage2(wi, p)
        z = _ifft_stage2_gate(g2, q, vxx[1 + o], z, hy_bias[o])
    return z


def _merge_kernel(ya_ref, yb_ref, yc_ref, yd_ref, g0_ref, g1_ref, g2_ref, g3_ref, wb_ref, wo_ref, x_ref, gate_ref,
                  o_ref):
    ys = (ya_ref, yb_ref, yc_ref, yd_ref)
    gs = (g0_ref, g1_ref, g2_ref, g3_ref)
    acc = _dot(ys[0][...], wb_ref[0]) * gs[0][...].astype(F32)
    for n in range(1, N_BRANCH):
        acc += _dot(ys[n][...], wb_ref[n]) * gs[n][...].astype(F32)
    o_ref[...] = x_ref[...] + gate_ref[...] * _dot(acc.astype(BF16), wo_ref[...])


def _merge(ys, gates, wb, wo, l, x, modt, q_gate, tm, row_tile0, mod_row):
    B, R, D = x.shape
    row = lambda b, i: (b, row_tile0 + i, 0)
    in_specs = [pl.BlockSpec((tm, BRANCH_W), lambda b, i: (i, b))]
    in_specs += [pl.BlockSpec((None, tm, BRANCH_W), row)] * (N_BRANCH - 1)
    in_specs += [pl.BlockSpec((None, tm, D), functools.partial(lambda b, i, n: (b, row_tile0 + i, n), n=n))
                 for n in range(N_BRANCH)]
    in_specs += [pl.BlockSpec((None, N_BRANCH, BRANCH_W, D), lambda b, i: (l, 0, 0, 0),
                              pipeline_mode=pl.Buffered(1)),
                 pl.BlockSpec((None, D, D), lambda b, i: (l, 0, 0), pipeline_mode=pl.Buffered(1)),
                 pl.BlockSpec((None, tm, D), lambda b, i: (b, i, 0)),
                 pl.BlockSpec((None, None, 1, D), lambda b, i: (mod_row(b), q_gate, 0, 0))]
    return pl.pallas_call(
        _merge_kernel,
        grid=(B, R // tm), in_specs=in_specs,
        out_specs=pl.BlockSpec((None, tm, D), lambda b, i: (b, i, 0)),
        out_shape=jax.ShapeDtypeStruct((B, R, D), F32),
        compiler_params=_cp("parallel", "arbitrary"), name="merge",
    )(*ys, gates, gates, gates, gates, wb, wo, x, modt)


def _expert_up_kernel(x_ref, w1_ref, w3_ref, o_ref):
    x = x_ref[...]
    a = _dot(x, w1_ref[...].astype(BF16))
    b = _dot(x, w3_ref[...].astype(BF16))
    o_ref[...] = (a * jax.nn.sigmoid(a) * b).astype(o_ref.dtype)


def _expert_down_kernel(h_ref, g_ref, w2_ref, o_ref):
    o_ref[...] = (_dot(h_ref[...], w2_ref[...].astype(BF16)) * g_ref[...]).astype(o_ref.dtype)


def _experts(xg, gate, w1, w3, w2, l):
    E, M, D = xg.shape
    F = w1.shape[3]
    tf = 256
    hmid = pl.pallas_call(
        _expert_up_kernel,
        grid=(E, F // tf),
        in_specs=[pl.BlockSpec((None, M, D), lambda e, f: (e, 0, 0)),
                  pl.BlockSpec((None, None, D, tf), lambda e, f: (l, e, 0, f)),
                  pl.BlockSpec((None, None, D, tf), lambda e, f: (l, e, 0, f))],
        out_specs=pl.BlockSpec((None, M, tf), lambda e, f: (e, 0, f)),
        out_shape=jax.ShapeDtypeStruct((E, M, F), BF16),
        compiler_params=_cp("parallel", "arbitrary"), name="expert_up",
    )(xg, w1, w3)
    return pl.pallas_call(
        _expert_down_kernel,
        grid=(E, D // tf),
        in_specs=[pl.BlockSpec((None, M, F), lambda e, n: (e, 0, 0)),
                  pl.BlockSpec((None, M, 1), lambda e, n: (e, 0, 0)),
                  pl.BlockSpec((None, None, F, tf), lambda e, n: (l, e, 0, n))],
        out_specs=pl.BlockSpec((None, M, tf), lambda e, n: (e, 0, n)),
        out_shape=jax.ShapeDtypeStruct((E, M, D), BF16),
        compiler_params=_cp("parallel", "arbitrary"), name="expert_down",
    )(hmid, gate, w2)


def _combine_kernel(tok_ref, y_ref, x_ref, g_ref, o_ref, *, row0):
    E, Ct, tn = y_ref.shape
    rowid = row0 + pl.program_id(2) * ROW_TILE + lax.broadcasted_iota(jnp.int32, (ROW_TILE, 1), 0)
    onehot = jnp.where(tok_ref[...] == rowid, 1.0, 0.0).astype(BF16)
    o_ref[...] = x_ref[...] + g_ref[...] * _dot(onehot, y_ref[...].reshape(E * Ct, tn))


def _combine(tok, y, x, modt, q_gate, row0, mod_row):
    B, R, D = x.shape
    E, _, Ct, _ = y.shape
    tn = math.gcd(D, PROJ_TILE_N)
    return pl.pallas_call(
        functools.partial(_combine_kernel, row0=row0),
        grid=(B, D // tn, R // ROW_TILE),
        in_specs=[pl.BlockSpec((None, 1, E * Ct), lambda b, n, t: (b, 0, 0)),
                  pl.BlockSpec((E, None, Ct, tn), lambda b, n, t: (0, b, 0, n)),
                  pl.BlockSpec((None, ROW_TILE, tn), lambda b, n, t: (b, t, n)),
                  pl.BlockSpec((None, None, 1, tn), lambda b, n, t: (mod_row(b), q_gate, 0, n))],
        out_specs=pl.BlockSpec((None, ROW_TILE, tn), lambda b, n, t: (b, t, n)),
        out_shape=jax.ShapeDtypeStruct((B, R, D), F32),
        compiler_params=_cp("parallel", "parallel", "arbitrary"), name="moe_combine",
    )(tok, y, x, modt)


def _moe(h2, aff, T, Lc, w1, w3, w2, l):
    B, S, D = h2.shape
    E = w1.shape[1]

    def route(a):
        n = a.shape[1]
        return lax.top_k(a.transpose(0, 2, 1), CAPACITY_FACTOR * n // E)

    gate, tok = route(aff[:, :T, :E])
    if Lc:
        gate_c, tok_c = route(aff[:, T:, :E])
        gate, tok = jnp.concatenate([gate, gate_c], axis=-1), jnp.concatenate([tok, T + tok_c], axis=-1)
    Ct = tok.shape[-1]
    rows = (tok + (jnp.arange(B, dtype=jnp.int32) * S)[:, None, None]).transpose(1, 0, 2).reshape(E, B * Ct)
    xg = h2.reshape(B * S, D)[rows]
    y = _experts(xg, gate.transpose(1, 0, 2).reshape(E, B * Ct, 1), w1, w3, w2, l)
    return tok.reshape(B, 1, E * Ct), y.reshape(E, B, Ct, D)


def kernel(x, c, ctx, c_ctx, w_mod, b_mod, norm1, norm2, w_in, hy_short, hy_f1, hy_b1, hy_freq, hy_f2, hy_b2, hy_f3, hy_bias, qn_b, kn_b, qn_c, kn_c, rpb_c, qn_d, kn_d, lam_q1, lam_k1, lam_q2, lam_k2, subln_d, w_branch, w_out, w_router, w_e1, w_e3, w_e2):
    B, T, D = x.shape
    Lc = ctx.shape[1]
    S = T + Lc
    depth = w_mod.shape[0]
    assert T % ROW_TILE == 0 and Lc == ROW_TILE and T // GRID_W >= NA_WIN_ROWS

    cc = jnp.concatenate([c, c_ctx[None, :], jnp.zeros((8 - B - 1, D), F32)], axis=0)
    rope_h = _rope_tables(B, T, S, HEAD_DIM)
    rope_d = _rope_tables(B, T, S, DF_QK_DIM)
    use_fft = (2 * T // FFT_N2) % 32 == 0
    dft_lat = _fft_tables(T) if use_fft else _dft_tables(T)
    dft_ctx = _dft_tables(Lc)
    wb, wo = w_branch.astype(BF16), w_out.astype(BF16)

    for l in range(depth):
        last = l == depth - 1
        n_ctx = 0 if last else 1
        Tq = T + n_ctx * ROW_TILE
        lam_init = 0.8 - 0.6 * math.exp(-0.3 * l)
        lam = (jnp.exp(jnp.sum(lam_q1[l].astype(F32) * lam_k1[l].astype(F32)))
               - jnp.exp(jnp.sum(lam_q2[l].astype(F32) * lam_k2[l].astype(F32))) + lam_init)
        modt = _modulation(cc, w_mod, b_mod, l).reshape(8, 6, 1, D)

        h = _prenorm(x, ctx, norm1[l], modt, 1, 0, 1)
        h2d = h.reshape(B * S, D)
        gains = _proj_gains(qn_b[l], kn_b[l], qn_c[l], kn_c[l], qn_d[l], kn_d[l])
        proj = _in_proj(h2d, w_in, l, gains, rope_h, rope_d).reshape(B, S, C_GATES)
        gates = _matmul(h2d, w_in, l, C_GATES, N_BRANCH * D, BF16, sigmoid=True,
                        name="in_proj_gates").reshape(B, S, N_BRANCH * D)

        y_b = _gqa(proj, T, Tq, ATT_SCALE)
        y_c = _na(proj, _na_bias(rpb_c[l], T), T, Tq)
        y_d = _diff_attn(lam, proj, subln_d[l], T, Tq, 1.0 - lam_init)

        conv = _shortconv(proj, hy_short[l].astype(F32), T, with_ctx=not last)
        filt = _hyena_filters(T, hy_f1[l], hy_b1[l], hy_freq[l], hy_f2[l], hy_b2[l], hy_f3[l], circular=use_fft)
        y_a = (_hyena_fft if use_fft else _hyena)(conv[:3], dft_lat, filt, hy_bias[l])
        x_mid = _merge((y_a, y_b, y_c, y_d), gates, wb, wo, l, x, modt, 2, ROW_TILE, 0, lambda b: b)
        if not last:
            filt_c = _hyena_filters(Lc, hy_f1[l], hy_b1[l], hy_freq[l], hy_f2[l], hy_b2[l], hy_f3[l],
                                    circular=False)
            y_a_ctx = _hyena(conv[3:], dft_ctx, filt_c, hy_bias[l])
            ctx = _merge((y_a_ctx, y_b, y_c, y_d), gates, wb, wo, l, ctx, modt, 2, ROW_TILE, T // ROW_TILE,
                         lambda b: B)
        x = x_mid

        h2, aff = _prenorm(x, ctx, norm2[l], modt, 4, 3, n_ctx, w_router=w_router[l].astype(F32))
        tok, y = _moe(h2, aff, T, n_ctx * Lc, w_e1, w_e3, w_e2, l)
        x = _combine(tok, y, x, modt, 5, 0, lambda b: b)
        if not last:
            ctx = _combine(tok, y, ctx, modt, 5, T, lambda b: B)
    return x
```

```python
import functools
import math

import numpy as np
import jax
import jax.numpy as jnp
from jax import lax
from jax.experimental import pallas as pl
from jax.experimental.pallas import tpu as pltpu

F32 = jnp.float32
BF16 = jnp.bfloat16

GRID_W = 64
HEAD_DIM = 128
BRANCH_W = 512
N_BRANCH = 4
HY_W = 512
HY_ORDER = 2
HY_BANDS = 16
HY_EMB = 1 + 2 * HY_BANDS
HY_FAST_DECAY = 0.3
HY_SLOW_DECAY = 1.5
HY_TARGET = 1e-2
GQA_HEADS = 4
GQA_KV = 2
NA_HEADS = 4
NA_WIN_H = 8
NA_WIN_W = 16
DF_HEADS = 4
DF_QK_DIM = 64
N_EXPERTS = 16
CAPACITY_FACTOR = 2
ROPE_THETA = 10000.0
NORM_EPS = 1e-6
ATT_SCALE = HEAD_DIM ** -0.5
DF_SCALE = DF_QK_DIM ** -0.5
NEG_INF = -1e30
LOG2E = math.log2(math.e)

LANE = 128
ROW_TILE = 256
PROJ_TILE_N = 512
NA_GROUP_ROWS = 4
NA_WIN_ROWS = NA_GROUP_ROWS + NA_WIN_H
DFT_HALF = 256
VMEM_LIMIT =56 * 1024 * 1024

C_HY = 0
C_QB = 3 * HY_W
C_KB = C_QB + GQA_HEADS * HEAD_DIM
C_VB = C_KB + GQA_KV * HEAD_DIM
C_QC = C_VB + GQA_KV * HEAD_DIM
C_KC = C_QC + NA_HEADS * HEAD_DIM
C_VC = C_KC + NA_HEADS * HEAD_DIM
C_QD = C_VC + NA_HEADS * HEAD_DIM
C_KD = C_QD + DF_HEADS * 2 * DF_QK_DIM
C_VD = C_KD + DF_HEADS * 2 * DF_QK_DIM
C_GATES = C_VD + DF_HEADS * HEAD_DIM


def _cp(*sem):
    return pltpu.CompilerParams(dimension_semantics=sem, vmem_limit_bytes=VMEM_LIMIT)


def _dot(a, b):
    return jnp.dot(a, b, preferred_element_type=F32)


def _dot_nt(a, b):
    return lax.dot_general(a, b, (((1,), (1,)), ((), ())), preferred_element_type=F32)


def _dot_hi(a, b):
    return jnp.dot(a, b, preferred_element_type=F32, precision=lax.Precision.HIGHEST)


def _mod_kernel(c_ref, w_ref, b_ref, o_ref):
    c = c_ref[...]
    a = (c * jax.nn.sigmoid(c)).astype(BF16)
    o_ref[...] = _dot(a, w_ref[...].astype(BF16)) + b_ref[...]


def _modulation(cc, w, b, l):
    depth, D, N = w.shape
    tn = math.gcd(N, 1024)
    return pl.pallas_call(
        _mod_kernel,
        grid=(N // tn,),
        in_specs=[pl.BlockSpec((8, D), lambda j: (0, 0)),
                  pl.BlockSpec((None, D, tn), lambda j: (l, 0, j)),
                  pl.BlockSpec((None, 1, tn), lambda j: (l, 0, j))],
        out_specs=pl.BlockSpec((8, tn), lambda j: (0, j)),
        out_shape=jax.ShapeDtypeStruct((8, N), F32),
        compiler_params=_cp("arbitrary"),
        name="modulation",
    )(cc, w, b.reshape(depth, 1, N))


def _prenorm_body(x, g_ref, sc_ref, sh_ref):
    y = x * lax.rsqrt(jnp.mean(x * x, axis=-1, keepdims=True) + NORM_EPS)
    return y * g_ref[...] * (1.0 + sc_ref[...]) + sh_ref[...]


def _prenorm_kernel(x_ref, c_ref, g_ref, sc_ref, sh_ref, o_ref, *, n_lat):
    i = pl.program_id(1)

    @pl.when(i < n_lat)
    def _():
        o_ref[...] = _prenorm_body(x_ref[...], g_ref, sc_ref, sh_ref).astype(BF16)

    @pl.when(i >= n_lat)
    def _():
        o_ref[...] = _prenorm_body(c_ref[...], g_ref, sc_ref, sh_ref).astype(BF16)


def _prenorm_router_kernel(x_ref, c_ref, g_ref, sc_ref, sh_ref, wr_ref, o_ref, a_ref, *, n_lat, n_exp):
    i = pl.program_id(1)

    def run(x):
        h = _prenorm_body(x, g_ref, sc_ref, sh_ref)
        o_ref[...] = h.astype(BF16)
        logits = _dot_hi(h, wr_ref[...])
        lane = lax.broadcasted_iota(jnp.int32, logits.shape, 1)
        logits = jnp.where(lane < n_exp, logits, NEG_INF)
        e = jnp.exp(logits - jnp.max(logits, axis=-1, keepdims=True))
        a_ref[...] = e / jnp.sum(e, axis=-1, keepdims=True)

    @pl.when(i < n_lat)
    def _():
        run(x_ref[...])

    @pl.when(i >= n_lat)
    def _():
        run(c_ref[...])


def _prenorm(x, ctx, gain, modt, q_scale, q_shift, n_ctx_tiles, w_router=None):
    B, T, D = x.shape
    n_lat = T // ROW_TILE
    nt = n_lat + n_ctx_tiles
    S = nt * ROW_TILE

    def mod_map(q):
        return lambda b, i: (jnp.where(i < n_lat, b, B), q, 0, 0)

    in_specs = [
        pl.BlockSpec((None, ROW_TILE, D), lambda b, i: (b, jnp.minimum(i, n_lat - 1), 0)),
        pl.BlockSpec((None, ROW_TILE, D), lambda b, i: (b, jnp.maximum(i - n_lat, 0), 0)),
        pl.BlockSpec((1, D), lambda b, i: (0, 0)),
        pl.BlockSpec((None, None, 1, D), mod_map(q_scale)),
        pl.BlockSpec((None, None, 1, D), mod_map(q_shift)),
    ]
    out_h = pl.BlockSpec((None, ROW_TILE, D), lambda b, i: (b, i, 0))
    shape_h = jax.ShapeDtypeStruct((B, S, D), BF16)
    args = [x, ctx, gain.reshape(1, D), modt, modt]
    if w_router is None:
        return pl.pallas_call(
            functools.partial(_prenorm_kernel, n_lat=n_lat),
            grid=(B, nt), in_specs=in_specs, out_specs=out_h, out_shape=shape_h,
            compiler_params=_cp("parallel", "arbitrary"), name="prenorm",
        )(*args)
    n_exp = w_router.shape[1]
    wr = jnp.pad(w_router, ((0, 0), (0, LANE - n_exp)))
    return pl.pallas_call(
        functools.partial(_prenorm_router_kernel, n_lat=n_lat, n_exp=n_exp),
        grid=(B, nt),
        in_specs=in_specs + [pl.BlockSpec((D, LANE), lambda b, i: (0, 0))],
        out_specs=[out_h, pl.BlockSpec((None, ROW_TILE, LANE), lambda b, i: (b, i, 0))],
        out_shape=[shape_h, jax.ShapeDtypeStruct((B, S, LANE), F32)],
        compiler_params=_cp("parallel", "arbitrary"), name="prenorm_router",
    )(*args, wr)


def _mm_kernel(a_ref, b_ref, o_ref, *, sigmoid):
    acc = _dot(a_ref[...], b_ref[...].astype(BF16))
    if sigmoid:
        acc = jax.nn.sigmoid(acc)
    o_ref[...] = acc.astype(o_ref.dtype)


def _matmul(a, b, l, col0, N, out_dtype, sigmoid=False, name="matmul"):
    M, K = a.shape
    tm, tn = math.gcd(M, 1024), PROJ_TILE_N
    cb = col0 // tn
    return pl.pallas_call(
        functools.partial(_mm_kernel, sigmoid=sigmoid),
        grid=(M // tm, N // tn),
        in_specs=[pl.BlockSpec((tm, K), lambda i, j: (i, 0)),
                  pl.BlockSpec((None, K, tn), lambda i, j: (l, 0, cb + j))],
        out_specs=pl.BlockSpec((tm, tn), lambda i, j: (i, j)),
        out_shape=jax.ShapeDtypeStruct((M, N), out_dtype),
        compiler_params=_cp("parallel", "arbitrary"), name=name,
    )(a, b)


def _qk_post(x, g, seg, tables):
    lane = lax.broadcasted_iota(jnp.int32, (1, LANE), 1)
    sq = x * x
    if seg == LANE:
        ms = jnp.mean(sq, axis=-1, keepdims=True)
    else:
        lo = jnp.sum(jnp.where(lane < seg, sq, 0.0), axis=-1, keepdims=True)
        hi = jnp.sum(jnp.where(lane >= seg, sq, 0.0), axis=-1, keepdims=True)
        ms = jnp.where(lane < seg, lo, hi) * (1.0 / seg)
    y = x * lax.rsqrt(ms + NORM_EPS) * g
    if tables is not None:
        cos_ref, sin_ref = tables
        q = seg // 4
        partner = jnp.where((lane % (seg // 2)) < q, pltpu.roll(y, LANE - q, 1), pltpu.roll(y, q, 1))
        y = y * cos_ref[...] + partner * sin_ref[...]
    return y


def _proj_block_kinds():
    kinds = []
    for col in range(0, C_GATES, LANE):
        if C_QB <= col < C_VB:
            kinds.append((HEAD_DIM, True))
        elif C_QC <= col < C_VC:
            kinds.append((HEAD_DIM, False))
        elif C_QD <= col < C_VD:
            kinds.append((DF_QK_DIM, True))
        else:
            kinds.append(None)
    return kinds


def _inproj_kernel(a_ref, w_ref, g_ref, c128_ref, s128_ref, c64_ref, s64_ref, o_ref):
    j = pl.program_id(1)
    acc = _dot(a_ref[...], w_ref[...].astype(BF16))
    per_tile = PROJ_TILE_N // LANE
    kinds = _proj_block_kinds()
    tiles = [kinds[t * per_tile:(t + 1) * per_tile] for t in range(len(kinds) // per_tile)]
    plain = functools.reduce(jnp.logical_or, [j == t for t, ks in enumerate(tiles) if not any(ks)])

    @pl.when(plain)
    def _():
        o_ref[...] = acc.astype(o_ref.dtype)

    for t, ks in enumerate(tiles):
        if not any(ks):
            continue

        @pl.when(j == t)
        def _(ks=ks):
            for h, kind in enumerate(ks):
                y = acc[:, h * LANE:(h + 1) * LANE]
                if kind is not None:
                    seg, rope = kind
                    tables = None if not rope else ((c128_ref, s128_ref) if seg == HEAD_DIM else (c64_ref, s64_ref))
                    y = _qk_post(y, g_ref[h:h + 1, :], seg, tables)
                o_ref[:, h * LANE:(h + 1) * LANE] = y.astype(o_ref.dtype)


def _in_proj(h2d, w_in_bf, l, gains, rope_h, rope_d):
    M, K = h2d.shape
    tm, tn = math.gcd(M, 1024), PROJ_TILE_N
    per_tile = tn // LANE
    table = pl.BlockSpec((tm, LANE), lambda i, j: (i, 0))
    return pl.pallas_call(
        _inproj_kernel,
        grid=(M // tm, C_GATES // tn),
        in_specs=[pl.BlockSpec((tm, K), lambda i, j: (i, 0)),
                  pl.BlockSpec((None, K, tn), lambda i, j: (l, 0, j)),
                  pl.BlockSpec((None, per_tile, LANE), lambda i, j: (j, 0, 0)),
                  table, table, table, table],
        out_specs=pl.BlockSpec((tm, tn), lambda i, j: (i, j)),
        out_shape=jax.ShapeDtypeStruct((M, C_GATES), BF16),
        compiler_params=_cp("parallel", "arbitrary"), name="in_proj",
    )(h2d, w_in_bf, gains.reshape(-1, per_tile, LANE), *rope_h, *rope_d)


def _proj_gains(qn_b, kn_b, qn_c, kn_c, qn_d, kn_d):
    ones = lambda n: jnp.ones((n, LANE), F32)
    rep = lambda g, n: jnp.tile(jnp.tile(g.astype(F32), LANE // g.shape[0])[None, :], (n, 1))
    return jnp.concatenate([
        ones(C_QB // LANE), rep(qn_b, GQA_HEADS), rep(kn_b, GQA_KV), ones(GQA_KV),
        rep(qn_c, NA_HEADS), rep(kn_c, NA_HEADS), ones(NA_HEADS),
        rep(qn_d, DF_HEADS), rep(kn_d, DF_HEADS), ones(DF_HEADS)], axis=0)


def _rope_tables(B, T, S, seg):
    half = seg // 2
    nfreq = half // 2
    inv = ROPE_THETA ** (-jnp.arange(0, half, 2, dtype=F32) / half)
    pos = jnp.arange(T, dtype=jnp.int32)
    rows, cols = (pos // GRID_W).astype(F32), (pos % GRID_W).astype(F32)
    l = np.arange(LANE) % seg
    use_col = l >= half
    fidx = (l % half) % nfreq
    is_b = (l % half) >= nfreq
    ang = jnp.where(use_col[None, :], cols[:, None], rows[:, None]) * inv[fidx][None, :]
    cos, sin = jnp.cos(ang), jnp.sin(ang)
    sin = jnp.where(is_b[None, :], sin, -sin)
    cos = jnp.concatenate([cos, jnp.ones((S - T, LANE), F32)], axis=0)
    sin = jnp.concatenate([sin, jnp.zeros((S - T, LANE), F32)], axis=0)
    return jnp.tile(cos, (B, 1)), jnp.tile(sin, (B, 1))


def _softmax_rows(s, scale):
    c = scale * LOG2E
    m = jnp.max(s, axis=-1, keepdims=True)
    p = jnp.exp2(s * c - m * c)
    return p.astype(BF16), jnp.sum(p, axis=-1, keepdims=True)


def _attend(qs, ks, vs, scale):
    scores = [_dot_nt(q, k) for q, k in zip(qs, ks)]
    outs = []
    for s, v in zip(scores, vs):
        p, l = _softmax_rows(s, scale)
        outs.append(_dot(p, v) / l)
    return outs


def _attend_t(qts, ks, vts, scale):
    c = scale * LOG2E
    scores = [_dot(k, qt) for qt, k in zip(qts, ks)]
    outs = []
    for s, vt in zip(scores, vts):
        m = jnp.max(s, axis=0, keepdims=True)
        p = jnp.exp2(s * c - m * c)
        l = jnp.sum(p, axis=0, keepdims=True)
        outs.append(_dot(vt, p.astype(BF16)) / l)
    return outs


def _store_vt(vt_ref, v):
    for h in range(vt_ref.shape[0]):
        vt_ref[h] = v[:, h * LANE:(h + 1) * LANE].T


GQA_KV_PER_STEP = 2


def _gqa_kernel(q_ref, k_ref, v_ref, o_ref, vt_ref, *, T, R, scale):
    i = pl.program_id(2)

    @pl.when(i == 0)
    def _():
        _store_vt(vt_ref, v_ref[...])

    def attend(lo):
        n_heads = q_ref.shape[1] // LANE
        qts = [q_ref[:, h * LANE:(h + 1) * LANE].T for h in range(n_heads)]
        ks = [k_ref[lo:, (h // R) * LANE:(h // R + 1) * LANE] for h in range(n_heads)]
        outs = _attend_t(qts, ks, [vt_ref[h // R, :, lo:] for h in range(n_heads)], scale)
        for h in range(n_heads):
            o_ref[:, h * LANE:(h + 1) * LANE] = outs[h].T.astype(o_ref.dtype)

    @pl.when(i < T // ROW_TILE)
    def _():
        attend(0)

    @pl.when(i >= T // ROW_TILE)
    def _():
        attend(T)


def _gqa(proj, T, Tq, scale):
    B, S, _ = proj.shape
    R = GQA_HEADS // GQA_KV
    G = GQA_KV_PER_STEP
    qb, kb, vb = C_QB // (G * R * LANE), C_KB // (G * LANE), C_VB // (G * LANE)
    return pl.pallas_call(
        functools.partial(_gqa_kernel, T=T, R=R, scale=scale),
        grid=(B, GQA_KV // G, Tq // ROW_TILE),
        in_specs=[pl.BlockSpec((None, ROW_TILE, G * R * LANE), lambda b, g, i: (b, i, qb + g)),
                  pl.BlockSpec((None, S, G * LANE), lambda b, g, i: (b, 0, kb + g)),
                  pl.BlockSpec((None, S, G * LANE), lambda b, g, i: (b, 0, vb + g))],
        out_specs=pl.BlockSpec((None, ROW_TILE, G * R * LANE), lambda b, g, i: (b, i, g)),
        out_shape=jax.ShapeDtypeStruct((B, Tq, GQA_HEADS * LANE), BF16),
        scratch_shapes=[pltpu.VMEM((G, LANE, S), BF16)],
        compiler_params=_cp("parallel", "parallel", "arbitrary"), name="gqa",
    )(proj, proj, proj)


DF_HEADS_PER_STEP = 2


def _diff_kernel(lam_ref, q_ref, k_ref, v_ref, g_ref, o_ref, vt_ref, *, T, scale, out_scale):
    i = pl.program_id(2)
    lam = lam_ref[0]

    @pl.when(i == 0)
    def _():
        _store_vt(vt_ref, v_ref[...])

    def attend(lo):
        qts, ks, vts = [], [], []
        for h in range(DF_HEADS_PER_STEP):
            qt = q_ref[:, h * LANE:(h + 1) * LANE].T
            ch = lax.broadcasted_iota(jnp.int32, qt.shape, 0)
            zero = jnp.zeros_like(qt)
            qts += [jnp.where(ch < DF_QK_DIM, qt, zero), jnp.where(ch >= DF_QK_DIM, qt, zero)]
            ks += [k_ref[lo:, h * LANE:(h + 1) * LANE]] * 2
            vts += [vt_ref[h, :, lo:]] * 2
        outs = _attend_t(qts, ks, vts, scale)
        for h in range(DF_HEADS_PER_STEP):
            o = (outs[2 * h] - lam * outs[2 * h + 1]).T
            y = o * lax.rsqrt(jnp.mean(o * o, axis=-1, keepdims=True) + NORM_EPS) * g_ref[...]
            o_ref[:, h * LANE:(h + 1) * LANE] = (y * out_scale).astype(o_ref.dtype)

    @pl.when(i < T // ROW_TILE)
    def _():
        attend(0)

    @pl.when(i >= T // ROW_TILE)
    def _():
        attend(T)


def _diff_attn(lam, proj, subln, T, Tq, out_scale):
    B, S, _ = proj.shape
    W = DF_HEADS_PER_STEP * LANE
    qb, kb, vb = C_QD // W, C_KD // W, C_VD // W
    return pl.pallas_call(
        functools.partial(_diff_kernel, T=T, scale=DF_SCALE, out_scale=out_scale),
        grid=(B, DF_HEADS // DF_HEADS_PER_STEP, Tq // ROW_TILE),
        in_specs=[pl.BlockSpec(memory_space=pltpu.SMEM),
                  pl.BlockSpec((None, ROW_TILE, W), lambda b, h, i: (b, i, qb + h)),
                  pl.BlockSpec((None, S, W), lambda b, h, i: (b, 0, kb + h)),
                  pl.BlockSpec((None, S, W), lambda b, h, i: (b, 0, vb + h)),
                  pl.BlockSpec((1, LANE), lambda b, h, i: (0, 0))],
        out_specs=pl.BlockSpec((None, ROW_TILE, W), lambda b, h, i: (b, i, h)),
        out_shape=jax.ShapeDtypeStruct((B, Tq, DF_HEADS * LANE), BF16),
        scratch_shapes=[pltpu.VMEM((DF_HEADS_PER_STEP, LANE, S), BF16)],
        compiler_params=_cp("parallel", "parallel", "arbitrary"), name="diff_attn",
    )(lam.reshape(1).astype(F32), proj, proj, proj, subln.astype(F32).reshape(1, LANE))


NA_HEADS_PER_STEP = 4


def _na_kernel(q_ref, k_ref, v_ref, bias_ref, o_ref, *, T, scale):
    i = pl.program_id(2)
    n_groups = T // ROW_TILE
    grid_rows = T // GRID_W
    c = scale * LOG2E
    heads = [slice(h * LANE, (h + 1) * LANE) for h in range(NA_HEADS_PER_STEP)]
    qts = [q_ref[:, hs].T for hs in heads]
    u_cs = [_dot(k_ref[T:, hs], qt) * c for hs, qt in zip(heads, qts)]
    vcts = [v_ref[T:, hs].T for hs in heads]

    @pl.when(i < n_groups)
    def _():
        row0 = jnp.clip(i * NA_GROUP_ROWS - NA_WIN_H // 2, 0, grid_rows - NA_WIN_ROWS)
        win = pl.ds(pl.multiple_of(row0 * GRID_W, ROW_TILE), NA_WIN_ROWS * GRID_W)
        u_ns = [_dot(k_ref[win, hs], qt) * c + bias_ref[h] for h, (hs, qt) in enumerate(zip(heads, qts))]
        for h, hs in enumerate(heads):
            u_n, u_c = u_ns[h], u_cs[h]
            m = jnp.maximum(jnp.max(u_n, axis=0, keepdims=True), jnp.max(u_c, axis=0, keepdims=True))
            p_n, p_c = jnp.exp2(u_n - m), jnp.exp2(u_c - m)
            l = jnp.sum(p_n, axis=0, keepdims=True) + jnp.sum(p_c, axis=0, keepdims=True)
            ot = (_dot(v_ref[win, hs].T, p_n.astype(BF16)) + _dot(vcts[h], p_c.astype(BF16))) / l
            o_ref[:, hs] = ot.T.astype(o_ref.dtype)

    @pl.when(i >= n_groups)
    def _():
        for h, hs in enumerate(heads):
            u_c = u_cs[h]
            p = jnp.exp2(u_c - jnp.max(u_c, axis=0, keepdims=True))
            ot = _dot(vcts[h], p.astype(BF16)) / jnp.sum(p, axis=0, keepdims=True)
            o_ref[:, hs] = ot.T.astype(o_ref.dtype)


def _na_bias(rpb, T):
    rows = T // GRID_W
    n_groups = rows // NA_GROUP_ROWS
    n_roff, n_coff = 2 * NA_WIN_H - 1, 2 * NA_WIN_W - 1
    c = np.arange(GRID_W)[:, None]
    kc = np.arange(GRID_W)[None, :]
    cs = np.clip(c - NA_WIN_W // 2, 0, GRID_W - NA_WIN_W)
    col_valid = (kc >= cs) & (kc < cs + NA_WIN_W)
    coff = np.clip(kc - c + NA_WIN_W - 1, 0, n_coff - 1)
    col_sel = (coff[..., None] == np.arange(n_coff)).astype(np.float32)
    rpb = rpb.astype(F32) * LOG2E
    out = []
    for grp in (0, 1, n_groups - 1):
        r = grp * NA_GROUP_ROWS + np.arange(NA_GROUP_ROWS)[:, None]
        rs = np.clip(r - NA_WIN_H // 2, 0, rows - NA_WIN_H)
        row0 = np.clip(grp * NA_GROUP_ROWS - NA_WIN_H // 2, 0, rows - NA_WIN_ROWS)
        key_row = row0 + np.arange(NA_WIN_ROWS)[None, :]
        row_valid = (key_row >= rs) & (key_row < rs + NA_WIN_H)
        roff = np.clip(key_row - r + NA_WIN_H - 1, 0, n_roff - 1)
        row_sel = (roff[..., None] == np.arange(n_roff)).astype(np.float32)
        bias = jnp.einsum('ika,hab,cqb->hkqic', row_sel, rpb, col_sel, precision=lax.Precision.HIGHEST)
        valid = row_valid.T[:, None, :, None] & col_valid.T[None, :, None, :]
        out.append(jnp.where(valid[None], bias, NEG_INF).reshape(rpb.shape[0], -1, ROW_TILE))
    return jnp.stack(out)


def _na(proj, bias, T, Tq):
    B, S, _ = proj.shape
    W = NA_HEADS_PER_STEP * LANE
    qb, kb, vb = C_QC // W, C_KC // W, C_VC // W
    n_groups = T // ROW_TILE
    nk = NA_WIN_ROWS * GRID_W

    def bias_map(b, h, i):
        return (jnp.where(i == 0, 0, jnp.where(i >= n_groups - 1, 2, 1)), h, 0, 0)

    return pl.pallas_call(
        functools.partial(_na_kernel, T=T, scale=ATT_SCALE),
        grid=(B, NA_HEADS // NA_HEADS_PER_STEP, Tq // ROW_TILE),
        in_specs=[pl.BlockSpec((None, ROW_TILE, W), lambda b, h, i: (b, i, qb + h)),
                  pl.BlockSpec((None, S, W), lambda b, h, i: (b, 0, kb + h)),
                  pl.BlockSpec((None, S, W), lambda b, h, i: (b, 0, vb + h)),
                  pl.BlockSpec((None, NA_HEADS_PER_STEP, nk, ROW_TILE), bias_map)],
        out_specs=pl.BlockSpec((None, ROW_TILE, W), lambda b, h, i: (b, i, h)),
        out_shape=jax.ShapeDtypeStruct((B, Tq, NA_HEADS * LANE), BF16),
        compiler_params=_cp("parallel", "parallel", "arbitrary"), name="na_attn",
    )(proj, proj, proj, bias)


def _shortconv_kernel(*refs, T, with_ctx):
    u_refs, w_refs, o_refs = refs[0:3], refs[3:6], refs[6:]
    S = u_refs[0].shape[0]
    row = lax.broadcasted_iota(jnp.int32, (S, 1), 0)
    first = (row == 0) | (row == T)
    last = (row == T - 1) | (row == S - 1)
    for n in range(3):
        u = u_refs[n][...].astype(F32)
        w = w_refs[n][...]
        prev = jnp.where(first, 0.0, pltpu.roll(u, 1, 0))
        nxt = jnp.where(last, 0.0, pltpu.roll(u, S - 1, 0))
        y = (prev * w[0:1] + u * w[1:2] + nxt * w[2:3]).astype(BF16)
        o_refs[n][...] = y[:T]
        if with_ctx:
            o_refs[3 + n][...] = y[T:]


def _shortconv(proj, w, T, with_ctx):
    B, S, _ = proj.shape
    Lc = S - T
    nct = HY_W // LANE
    in_specs = [pl.BlockSpec((None, S, LANE), functools.partial(lambda b, c, n: (b, 0, n * nct + c), n=n))
                for n in range(3)]
    in_specs += [pl.BlockSpec((3, LANE), functools.partial(lambda b, c, n: (0, n * nct + c), n=n))
                 for n in range(3)]
    out_specs = [pl.BlockSpec((T, LANE), lambda b, c: (0, b * nct + c))] * 3
    out_shape = [jax.ShapeDtypeStruct((T, B * HY_W), BF16)] * 3
    if with_ctx:
        out_specs += [pl.BlockSpec((Lc, LANE), lambda b, c: (0, b * nct + c))] * 3
        out_shape += [jax.ShapeDtypeStruct((Lc, B * HY_W), BF16)] * 3
    return pl.pallas_call(
        functools.partial(_shortconv_kernel, T=T, with_ctx=with_ctx),
        grid=(B, nct), in_specs=in_specs, out_specs=out_specs, out_shape=out_shape,
        compiler_params=_cp("parallel", "arbitrary"), name="hy_shortconv",
    )(proj, proj, proj, w, w, w)


def _filter_hidden_kernel(z_ref, f1_ref, b1_ref, fr_ref, f2_ref, b2_ref, o_ref):
    fr = fr_ref[...]
    hid = jnp.sin(fr * (_dot_hi(z_ref[...], f1_ref[...]) + b1_ref[...]))
    o_ref[...] = jnp.sin(fr * (_dot_hi(hid, f2_ref[...]) + b2_ref[...]))


def _filter_kernel(h_ref, hb_ref, t_ref, tb_ref, f3f_ref, f3b_ref, dl_ref, *o_refs, circular):
    fw = _dot_hi(h_ref[...], f3f_ref[...]) * jnp.exp(-t_ref[...] * dl_ref[...])
    bw = _dot_hi(hb_ref[...], f3b_ref[...]) * jnp.exp(-tb_ref[...] * dl_ref[...])
    row = lax.broadcasted_iota(jnp.int32, bw.shape, 0)
    bw = jnp.where(row == 0, 0.0, bw)
    inv = 1.0 / (jnp.sum(jnp.abs(fw), axis=0, keepdims=True) + jnp.sum(jnp.abs(bw), axis=0, keepdims=True))
    if circular:
        o_refs[0][0] = (fw * inv).astype(BF16)
        o_refs[0][1] = (bw * inv).astype(BF16)
    else:
        o_refs[0][...] = ((fw + bw) * inv).astype(BF16)
        o_refs[1][...] = ((fw - bw) * inv).astype(BF16)


def _hyena_filters(L, f1, b1, freq, f2, b2, f3, circular):
    t = jnp.linspace(0.0, 1.0, L, dtype=F32)[:, None]
    w = (2.0 * math.pi / L) * jnp.arange(L, dtype=F32)[:, None]
    bands = jnp.linspace(1e-4, HY_BANDS - 1, HY_BANDS, dtype=F32)[None, :]
    z = jnp.concatenate([t, jnp.cos(bands * w), -jnp.sin(bands * w)], axis=-1)
    z = jnp.pad(z, ((0, 0), (0, LANE - HY_EMB)))
    f1p = jnp.pad(f1.astype(F32), ((0, LANE - HY_EMB), (0, 0)))
    ffn = f1.shape[1]
    small = lambda shape: pl.BlockSpec(shape, lambda j: (0, 0))
    hid = pl.pallas_call(
        _filter_hidden_kernel,
        grid=(1,),
        in_specs=[small((L, LANE)), small((LANE, ffn)), small((1, ffn)), small((1, ffn)), small((ffn, ffn)),
                  small((1, ffn))],
        out_specs=small((L, ffn)), out_shape=jax.ShapeDtypeStruct((L, ffn), F32),
        compiler_params=_cp("arbitrary"), name="hy_filter_hidden",
    )(z, f1p, b1.astype(F32).reshape(1, ffn), freq.astype(F32).reshape(1, ffn), f2.astype(F32),
      b2.astype(F32).reshape(1, ffn))
    back = (lambda a: jnp.roll(jnp.flip(a, axis=0), 1, axis=0)) if circular else (lambda a: a)
    deltas = jnp.abs(jnp.linspace(math.log(HY_TARGET) / HY_SLOW_DECAY, math.log(HY_TARGET) / HY_FAST_DECAY,
                                  HY_W, dtype=F32))
    NC = HY_ORDER * HY_W
    dl = jnp.tile(deltas, HY_ORDER).reshape(1, NC)
    tn = 256
    if circular:
        out_specs = [pl.BlockSpec((2, L, tn), lambda j: (0, 0, j))]
        out_shape = [jax.ShapeDtypeStruct((2, L, NC), BF16)]
    else:
        out_specs = [pl.BlockSpec((L, tn), lambda j: (0, j))] * 2
        out_shape = [jax.ShapeDtypeStruct((L, NC), BF16)] * 2
    out = pl.pallas_call(
        functools.partial(_filter_kernel, circular=circular),
        grid=(NC // tn,),
        in_specs=[small((L, ffn)), small((L, ffn)), small((L, 1)), small((L, 1)),
                  pl.BlockSpec((ffn, tn), lambda j: (0, j)),
                  pl.BlockSpec((ffn, tn), lambda j: (0, NC // tn + j)),
                  pl.BlockSpec((1, tn), lambda j: (0, j))],
        out_specs=out_specs, out_shape=out_shape,
        compiler_params=_cp("arbitrary"), name="hy_filter",
    )(hid, back(hid), t, back(t), f3.astype(F32), f3.astype(F32), dl)
    return out[0].reshape(2 * L, NC) if circular else out


def _dft_tables(L):
    N = 2 * L
    r = np.arange(N)
    k = (r // (2 * DFT_HALF)) * DFT_HALF + r % DFT_HALF
    is_im = (r // DFT_HALF) % 2 == 1
    nyq = is_im & (k == 0)
    kj = jnp.asarray(k, jnp.int32)[:, None]
    im, nyq = is_im[:, None], nyq[:, None]
    n_lo = 64

    def cos_sin(step, count):
        m = (kj * (step * jnp.arange(count, dtype=jnp.int32))[None, :]) % N
        ang = m.astype(F32) * (2.0 * math.pi / N)
        return jnp.cos(ang), jnp.sin(ang)

    c1, s1 = cos_sin(n_lo, L // n_lo)
    c0, s0 = cos_sin(1, n_lo)
    p1 = jnp.where(nyq, 1.0, jnp.where(im, -s1, c1))
    q1 = jnp.where(nyq, 0.0, jnp.where(im, -c1, -s1))
    p0 = jnp.where(nyq, (1 - 2 * (jnp.arange(n_lo) % 2)).astype(F32)[None, :], c0)
    a = (p1[:, :, None] * p0[:, None, :] + q1[:, :, None] * s0[:, None, :]).reshape(N, L).astype(BF16)
    return a, a.T


def _kf_kernel(a_ref, hs_ref, hd_ref, o_ref, *, n_fft):
    i = pl.program_id(0)
    H = DFT_HALF
    re = _dot(a_ref[:H, :], hs_ref[...])
    im = _dot(a_ref[H:, :], hd_ref[...])
    o_ref[:H, :] = re * (2.0 / n_fft)
    o_ref[H:, :] = im * (2.0 / n_fft)

    @pl.when(i == 0)
    def _():
        ny = _dot(a_ref[H:H + 16, :], hs_ref[...])
        o_ref[0:1, :] = re[0:1] * (1.0 / n_fft)
        o_ref[H:H + 1, :] = ny[0:1] * (1.0 / n_fft)


def _filter_spectrum(a, hs, hd):
    N, L = a.shape
    NC = hs.shape[1]
    tm, tn = 2 * DFT_HALF, 512
    return pl.pallas_call(
        functools.partial(_kf_kernel, n_fft=N),
        grid=(N // tm, NC // tn),
        in_specs=[pl.BlockSpec((tm, L), lambda i, j: (i, 0)),
                  pl.BlockSpec((L, tn), lambda i, j: (0, j)),
                  pl.BlockSpec((L, tn), lambda i, j: (0, j))],
        out_specs=pl.BlockSpec((tm, tn), lambda i, j: (i, j)),
        out_shape=jax.ShapeDtypeStruct((N, NC), F32),
        compiler_params=_cp("parallel", "arbitrary"), name="hy_filter_spectrum",
    )(a, hs, hd)


def _fwd_kernel(a_ref, z_ref, kf_ref, p_ref):
    i = pl.program_id(0)
    H = DFT_HALF
    acc = _dot(a_ref[...], z_ref[...])
    zr, zi = acc[:H], acc[H:]
    kr, ki = kf_ref[:H, :], kf_ref[H:, :]
    row = lax.broadcasted_iota(jnp.int32, zr.shape, 0)
    real_pair = (row == 0) & (i == 0)
    p_ref[:H, :] = jnp.where(real_pair, zr * kr, zr * kr - zi * ki).astype(BF16)
    p_ref[H:, :] = jnp.where(real_pair, zi * ki, zr * ki + zi * kr).astype(BF16)


def _dft_multiply(a, z, kf, order):
    N, L = a.shape
    NB = z.shape[1] // HY_W
    tm = 2 * DFT_HALF
    return pl.pallas_call(
        _fwd_kernel,
        grid=(N // tm, NB),
        in_specs=[pl.BlockSpec((tm, L), lambda i, j: (i, 0)),
                  pl.BlockSpec((L, HY_W), lambda i, j: (0, j)),
                  pl.BlockSpec((tm, HY_W), lambda i, j: (i, order))],
        out_specs=pl.BlockSpec((tm, HY_W), lambda i, j: (i, j)),
        out_shape=jax.ShapeDtypeStruct((N, NB * HY_W), BF16),
        compiler_params=_cp("parallel", "arbitrary"), name="hy_dft_multiply",
    )(a, z, kf)


def _inv_kernel(at_ref, p_ref, x_ref, z_ref, b_ref, o_ref):
    y = _dot(at_ref[...], p_ref[...])
    z = z_ref[...].astype(F32)
    o_ref[...] = (x_ref[...].astype(F32) * (y + b_ref[...] * z)).astype(o_ref.dtype)


def _idft_gate(at, p, xo, z, bias):
    L, N = at.shape
    NB = z.shape[1] // HY_W
    tm = min(L, 512)
    return pl.pallas_call(
        _inv_kernel,
        grid=(L // tm, NB),
        in_specs=[pl.BlockSpec((tm, N), lambda i, j: (i, 0)),
                  pl.BlockSpec((N, HY_W), lambda i, j: (0, j)),
                  pl.BlockSpec((tm, HY_W), lambda i, j: (i, j)),
                  pl.BlockSpec((tm, HY_W), lambda i, j: (i, j)),
                  pl.BlockSpec((1, HY_W), lambda i, j: (0, 0))],
        out_specs=pl.BlockSpec((tm, HY_W), lambda i, j: (i, j)),
        out_shape=jax.ShapeDtypeStruct((L, NB * HY_W), BF16),
        compiler_params=_cp("parallel", "arbitrary"), name="hy_idft_gate",
    )(at, p, xo, z, bias.astype(F32).reshape(1, HY_W))


def _hyena(vxx, tables, filt, hy_bias):
    a, at = tables
    kf = _filter_spectrum(a, *filt)
    z = vxx[0]
    for o in range(HY_ORDER):
        p = _dft_multiply(a, z, kf, o)
        z = _idft_gate(at, p, vxx[1 + o], z, hy_bias[o])
    return z


FFT_N2 = LANE
FFT_K1B = 8
FFT_LANES = 8192


def _fft_tables(L):
    N = 2 * L
    N1 = N // FFT_N2
    KH = -(-(N1 // 2 + 1) // FFT_K1B) * FFT_K1B
    k1 = jnp.arange(KH, dtype=jnp.int32)
    n1 = jnp.arange(N1, dtype=jnp.int32)
    ang1 = ((k1[:, None] * n1[None, :]) % N1).astype(F32) * (2.0 * math.pi / N1)
    c1, s1 = jnp.cos(ang1), jnp.sin(ang1)
    f1 = jnp.concatenate([c1, -s1], axis=0).astype(BF16)
    mult = jnp.where((k1 == 0) | (k1 == N1 // 2), 1.0, jnp.where(k1 < N1 // 2, 2.0, 0.0))[:, None]
    g2 = jnp.concatenate([c1 * mult, -s1 * mult], axis=0)[:, :N1 // 2].T.astype(BF16)
    k2 = jnp.arange(FFT_N2, dtype=jnp.int32)
    kk = k1[:, None, None] + N1 * k2[None, :, None]
    th = ((kk * k2[None, None, :]) % N).astype(F32) * (2.0 * math.pi / N)
    mr, mi = jnp.cos(th), -jnp.sin(th)
    wf = jnp.concatenate([jnp.concatenate([mr, -mi], axis=2),
                          jnp.concatenate([mi, mr], axis=2)], axis=1).astype(BF16)
    return f1, wf, jnp.swapaxes(wf, 1, 2), g2


def _fft_s1_kernel(f_ref, x_ref, o_ref):
    o_ref[...] = _dot(f_ref[...], x_ref[...]).astype(o_ref.dtype)


def _fft_stage1(f, x2d):
    R, K = f.shape
    M2 = x2d.shape[1]
    tn = math.gcd(M2, FFT_LANES)
    return pl.pallas_call(
        _fft_s1_kernel,
        grid=(M2 // tn,),
        in_specs=[pl.BlockSpec((R, K), lambda j: (0, 0)),
                  pl.BlockSpec((K, tn), lambda j: (0, j))],
        out_specs=pl.BlockSpec((R, tn), lambda j: (0, j)),
        out_shape=jax.ShapeDtypeStruct((R, M2), BF16),
        compiler_params=_cp("arbitrary"), name="hy_fft_stage1",
    )(f, x2d)


def _fft_s2_kernel(*refs, with_filter, scale):
    if with_filter:
        w_ref, a_ref, kf_ref, o_ref = refs
    else:
        w_ref, a_ref, o_ref = refs
    H = FFT_N2
    for kk in range(FFT_K1B):
        a = jnp.concatenate([a_ref[0, kk], a_ref[1, kk]], axis=0)
        x = _dot(w_ref[kk], a)
        xr, xi = x[:H], x[H:]
        if with_filter:
            kr, ki = kf_ref[0, kk], kf_ref[1, kk]
            xr, xi = xr * kr - xi * ki, xr * ki + xi * kr
        elif scale != 1.0:
            xr, xi = xr * scale, xi * scale
        o_ref[0, kk] = xr.astype(o_ref.dtype)
        o_ref[1, kk] = xi.astype(o_ref.dtype)


def _fft_stage2(w, a4, kf4=None, order=0, out_dtype=BF16, scale=1.0):
    _, KH, N2, cols = a4.shape
    tn = HY_W
    blk = lambda col: pl.BlockSpec((2, FFT_K1B, N2, tn), col)
    in_specs = [pl.BlockSpec((FFT_K1B, 2 * N2, 2 * N2), lambda i, j: (i, 0, 0)),
                blk(lambda i, j: (0, i, 0, j))]
    args = [w, a4]
    if kf4 is not None:
        in_specs.append(blk(lambda i, j: (0, i, 0, order)))
        args.append(kf4)
    return pl.pallas_call(
        functools.partial(_fft_s2_kernel, with_filter=kf4 is not None, scale=scale),
        grid=(KH // FFT_K1B, cols // tn),
        in_specs=in_specs,
        out_specs=blk(lambda i, j: (0, i, 0, j)),
        out_shape=jax.ShapeDtypeStruct(a4.shape, out_dtype),
        compiler_params=_cp("parallel", "arbitrary"), name="hy_fft_stage2",
    )(*args)


def _ifft_s2_kernel(g_ref, q_ref, x_ref, z_ref, b_ref, o_ref):
    y = _dot(g_ref[...], q_ref[...])
    z = z_ref[...].astype(F32)
    o_ref[...] = (x_ref[...].astype(F32) * (y + b_ref[...] * z)).astype(o_ref.dtype)


def _ifft_stage2_gate(g2, q4, xo, z, bias):
    _, KH, N2, cols = q4.shape
    H1 = g2.shape[0]
    L = z.shape[0]
    M2 = N2 * cols
    tn = math.gcd(M2, FFT_LANES)
    bias_t = jnp.tile(bias.astype(F32), tn // HY_W).reshape(1, tn)
    row = pl.BlockSpec((H1, tn), lambda j: (0, j))
    out = pl.pallas_call(
        _ifft_s2_kernel,
        grid=(M2 // tn,),
        in_specs=[pl.BlockSpec((H1, 2 * KH), lambda j: (0, 0)),
                  pl.BlockSpec((2 * KH, tn), lambda j: (0, j)),
                  row, row,
                  pl.BlockSpec((1, tn), lambda j: (0, 0))],
        out_specs=row,
        out_shape=jax.ShapeDtypeStruct((H1, M2), BF16),
        compiler_params=_cp("arbitrary"), name="hy_ifft_stage2_gate",
    )(g2, q4.reshape(2 * KH, M2), xo.reshape(H1, M2), z.reshape(H1, M2), bias_t)
    return out.reshape(L, cols)


def _hyena_fft(vxx, tables, kern, hy_bias):
    f1, wf, wi, g2 = tables
    L, cols = vxx[0].shape
    KH, N1 = f1.shape[0] // 2, f1.shape[1]
    N = 2 * L
    NC = kern.shape[1]
    ka = _fft_stage1(f1, kern.reshape(N1, FFT_N2 * NC)).reshape(2, KH, FFT_N2, NC)
    kf = _fft_stage2(wf, ka, out_dtype=F32, scale=1.0 / N)
    z = vxx[0]
    for o in range(HY_ORDER):
        a = _fft_stage1(f1[:, :N1 // 2], z.reshape(N1 // 2, FFT_N2 * cols)).reshape(2, KH, FFT_N2, cols)
        p = _fft_stage2(wf, a, kf4=kf, order=o)
        q = _fft_stage2(wi, p)
        z = _ifft_stage2_gate(g2, q, vxx[1 + o], z, hy_bias[o])
    return z


def _merge_kernel(ya_ref, yb_ref, yc_ref, yd_ref, g0_ref, g1_ref, g2_ref, g3_ref, wb_ref, wo_ref, x_ref, gate_ref,
                  o_ref):
    ys = (ya_ref, yb_ref, yc_ref, yd_ref)
    gs = (g0_ref, g1_ref, g2_ref, g3_ref)
    acc = _dot(ys[0][...], wb_ref[0]) * gs[0][...].astype(F32)
    for n in range(1, N_BRANCH):
        acc += _dot(ys[n][...], wb_ref[n]) * gs[n][...].astype(F32)
    o_ref[...] = x_ref[...] + gate_ref[...] * _dot(acc.astype(BF16), wo_ref[...])


def _merge(ys, gates, wb, wo, l, x, modt, q_gate, tm, row_tile0, mod_row):
    B, R, D = x.shape
    row = lambda b, i: (b, row_tile0 + i, 0)
    in_specs = [pl.BlockSpec((tm, BRANCH_W), lambda b, i: (i, b))]
    in_specs += [pl.BlockSpec((None, tm, BRANCH_W), row)] * (N_BRANCH - 1)
    in_specs += [pl.BlockSpec((None, tm, D), functools.partial(lambda b, i, n: (b, row_tile0 + i, n), n=n))
                 for n in range(N_BRANCH)]
    in_specs += [pl.BlockSpec((None, N_BRANCH, BRANCH_W, D), lambda b, i: (l, 0, 0, 0),
                              pipeline_mode=pl.Buffered(1)),
                 pl.BlockSpec((None, D, D), lambda b, i: (l, 0, 0), pipeline_mode=pl.Buffered(1)),
                 pl.BlockSpec((None, tm, D), lambda b, i: (b, i, 0)),
                 pl.BlockSpec((None, None, 1, D), lambda b, i: (mod_row(b), q_gate, 0, 0))]
    return pl.pallas_call(
        _merge_kernel,
        grid=(B, R // tm), in_specs=in_specs,
        out_specs=pl.BlockSpec((None, tm, D), lambda b, i: (b, i, 0)),
        out_shape=jax.ShapeDtypeStruct((B, R, D), F32),
        compiler_params=_cp("parallel", "arbitrary"), name="merge",
    )(*ys, gates, gates, gates, gates, wb, wo, x, modt)


def _expert_up_kernel(x_ref, w1_ref, w3_ref, o_ref):
    x = x_ref[...]
    a = _dot(x, w1_ref[...].astype(BF16))
    b = _dot(x, w3_ref[...].astype(BF16))
    o_ref[...] = (a * jax.nn.sigmoid(a) * b).astype(o_ref.dtype)


def _expert_down_kernel(h_ref, g_ref, w2_ref, o_ref):
    o_ref[...] = (_dot(h_ref[...], w2_ref[...].astype(BF16)) * g_ref[...]).astype(o_ref.dtype)


def _experts(xg, gate, w1, w3, w2, l):
    E, M, D = xg.shape
    F = w1.shape[3]
    tf = 256
    hmid = pl.pallas_call(
        _expert_up_kernel,
        grid=(E, F // tf),
        in_specs=[pl.BlockSpec((None, M, D), lambda e, f: (e, 0, 0)),
                  pl.BlockSpec((None, None, D, tf), lambda e, f: (l, e, 0, f)),
                  pl.BlockSpec((None, None, D, tf), lambda e, f: (l, e, 0, f))],
        out_specs=pl.BlockSpec((None, M, tf), lambda e, f: (e, 0, f)),
        out_shape=jax.ShapeDtypeStruct((E, M, F), BF16),
        compiler_params=_cp("parallel", "arbitrary"), name="expert_up",
    )(xg, w1, w3)
    return pl.pallas_call(
        _expert_down_kernel,
        grid=(E, D // tf),
        in_specs=[pl.BlockSpec((None, M, F), lambda e, n: (e, 0, 0)),
                  pl.BlockSpec((None, M, 1), lambda e, n: (e, 0, 0)),
                  pl.BlockSpec((None, None, F, tf), lambda e, n: (l, e, 0, n))],
        out_specs=pl.BlockSpec((None, M, tf), lambda e, n: (e, 0, n)),
        out_shape=jax.ShapeDtypeStruct((E, M, D), BF16),
        compiler_params=_cp("parallel", "arbitrary"), name="expert_down",
    )(hmid, gate, w2)


def _combine_kernel(tok_ref, y_ref, x_ref, g_ref, o_ref, *, row0):
    E, Ct, tn = y_ref.shape
    rowid = row0 + pl.program_id(2) * ROW_TILE + lax.broadcasted_iota(jnp.int32, (ROW_TILE, 1), 0)
    onehot = jnp.where(tok_ref[...] == rowid, 1.0, 0.0).astype(BF16)
    o_ref[...] = x_ref[...] + g_ref[...] * _dot(onehot, y_ref[...].reshape(E * Ct, tn))


def _combine(tok, y, x, modt, q_gate, row0, mod_row):
    B, R, D = x.shape
    E, _, Ct, _ = y.shape
    tn = math.gcd(D, PROJ_TILE_N)
    return pl.pallas_call(
        functools.partial(_combine_kernel, row0=row0),
        grid=(B, D // tn, R // ROW_TILE),
        in_specs=[pl.BlockSpec((None, 1, E * Ct), lambda b, n, t: (b, 0, 0)),
                  pl.BlockSpec((E, None, Ct, tn), lambda b, n, t: (0, b, 0, n)),
                  pl.BlockSpec((None, ROW_TILE, tn), lambda b, n, t: (b, t, n)),
                  pl.BlockSpec((None, None, 1, tn), lambda b, n, t: (mod_row(b), q_gate, 0, n))],
        out_specs=pl.BlockSpec((None, ROW_TILE, tn), lambda b, n, t: (b, t, n)),
        out_shape=jax.ShapeDtypeStruct((B, R, D), F32),
        compiler_params=_cp("parallel", "parallel", "arbitrary"), name="moe_combine",
    )(tok, y, x, modt)


def _moe(h2, aff, T, Lc, w1, w3, w2, l):
    B, S, D = h2.shape
    E = w1.shape[1]

    def route(a):
        n = a.shape[1]
        return lax.top_k(a.transpose(0, 2, 1), CAPACITY_FACTOR * n // E)

    gate, tok = route(aff[:, :T, :E])
    if Lc:
        gate_c, tok_c = route(aff[:, T:, :E])
        gate, tok = jnp.concatenate([gate, gate_c], axis=-1), jnp.concatenate([tok, T + tok_c], axis=-1)
    Ct = tok.shape[-1]
    rows = (tok + (jnp.arange(B, dtype=jnp.int32) * S)[:, None, None]).transpose(1, 0, 2).reshape(E, B * Ct)
    xg = h2.reshape(B * S, D)[rows]
    y = _experts(xg, gate.transpose(1, 0, 2).reshape(E, B * Ct, 1), w1, w3, w2, l)
    return tok.reshape(B, 1, E * Ct), y.reshape(E, B, Ct, D)


def kernel(x, c, ctx, c_ctx, w_mod, b_mod, norm1, norm2, w_in, hy_short, hy_f1, hy_b1, hy_freq, hy_f2, hy_b2, hy_f3, hy_bias, qn_b, kn_b, qn_c, kn_c, rpb_c, qn_d, kn_d, lam_q1, lam_k1, lam_q2, lam_k2, subln_d, w_branch, w_out, w_router, w_e1, w_e3, w_e2):
    B, T, D = x.shape
    Lc = ctx.shape[1]
    S = T + Lc
    depth = w_mod.shape[0]
    assert T % ROW_TILE == 0 and Lc == ROW_TILE and T // GRID_W >= NA_WIN_ROWS

    cc = jnp.concatenate([c, c_ctx[None, :], jnp.zeros((8 - B - 1, D), F32)], axis=0)
    rope_h = _rope_tables(B, T, S, HEAD_DIM)
    rope_d = _rope_tables(B, T, S, DF_QK_DIM)
    use_fft = (2 * T // FFT_N2) % 32 == 0
    dft_lat = _fft_tables(T) if use_fft else _dft_tables(T)
    dft_ctx = _dft_tables(Lc)
    wb, wo = w_branch.astype(BF16), w_out.astype(BF16)

    for l in range(depth):
        last = l == depth - 1
        n_ctx = 0 if last else 1
        Tq = T + n_ctx * ROW_TILE
        lam_init = 0.8 - 0.6 * math.exp(-0.3 * l)
        lam = (jnp.exp(jnp.sum(lam_q1[l].astype(F32) * lam_k1[l].astype(F32)))
               - jnp.exp(jnp.sum(lam_q2[l].astype(F32) * lam_k2[l].astype(F32))) + lam_init)
        modt = _modulation(cc, w_mod, b_mod, l).reshape(8, 6, 1, D)

        h = _prenorm(x, ctx, norm1[l], modt, 1, 0, 1)
        h2d = h.reshape(B * S, D)
        gains = _proj_gains(qn_b[l], kn_b[l], qn_c[l], kn_c[l], qn_d[l], kn_d[l])
        proj = _in_proj(h2d, w_in, l, gains, rope_h, rope_d).reshape(B, S, C_GATES)
        gates = _matmul(h2d, w_in, l, C_GATES, N_BRANCH * D, BF16, sigmoid=True,
                        name="in_proj_gates").reshape(B, S, N_BRANCH * D)

        y_b = _gqa(proj, T, Tq, ATT_SCALE)
        y_c = _na(proj, _na_bias(rpb_c[l], T), T, Tq)
        y_d = _diff_attn(lam, proj, subln_d[l], T, Tq, 1.0 - lam_init)

        conv = _shortconv(proj, hy_short[l].astype(F32), T, with_ctx=not last)
        filt = _hyena_filters(T, hy_f1[l], hy_b1[l], hy_freq[l], hy_f2[l], hy_b2[l], hy_f3[l], circular=use_fft)
        y_a = (_hyena_fft if use_fft else _hyena)(conv[:3], dft_lat, filt, hy_bias[l])
        x_mid = _merge((y_a, y_b, y_c, y_d), gates, wb, wo, l, x, modt, 2, ROW_TILE, 0, lambda b: b)
        if not last:
            filt_c = _hyena_filters(Lc, hy_f1[l], hy_b1[l], hy_freq[l], hy_f2[l], hy_b2[l], hy_f3[l],
                                    circular=False)
            y_a_ctx = _hyena(conv[3:], dft_ctx, filt_c, hy_bias[l])
            ctx = _merge((y_a_ctx, y_b, y_c, y_d), gates, wb, wo, l, ctx, modt, 2, ROW_TILE, T // ROW_TILE,
                         lambda b: B)
        x = x_mid

        h2, aff = _prenorm(x, ctx, norm2[l], modt, 4, 3, n_ctx, w_router=w_router[l].astype(F32))
        tok, y = _moe(h2, aff, T, n_ctx * Lc, w_e1, w_e3, w_e2, l)
        x = _combine(tok, y, x, modt, 5, 0, lambda b: b)
        if not last:
            ctx = _combine(tok, y, ctx, modt, 5, T, lambda b: B)
    return x
```

```python
import functools
import math

import numpy as np
import jax
import jax.numpy as jnp
from jax import lax
from jax.experimental import pallas as pl
from jax.experimental.pallas import tpu as pltpu

F32 = jnp.float32
BF16 = jnp.bfloat16

GRID_W = 64
HEAD_DIM = 128
BRANCH_W = 512
N_BRANCH = 4
HY_W = 512
HY_ORDER = 2
HY_BANDS = 16
HY_EMB = 1 + 2 * HY_BANDS
HY_FAST_DECAY = 0.3
HY_SLOW_DECAY = 1.5
HY_TARGET = 1e-2
GQA_HEADS = 4
GQA_KV = 2
NA_HEADS = 4
NA_WIN_H = 8
NA_WIN_W = 16
DF_HEADS = 4
DF_QK_DIM = 64
N_EXPERTS = 16
CAPACITY_FACTOR = 2
ROPE_THETA = 10000.0
NORM_EPS = 1e-6
ATT_SCALE = HEAD_DIM ** -0.5
DF_SCALE = DF_QK_DIM ** -0.5
NEG_INF = -1e30
LOG2E = math.log2(math.e)

LANE = 128
ROW_TILE = 256
PROJ_TILE_N = 512
NA_GROUP_ROWS = 4
NA_WIN_ROWS = NA_GROUP_ROWS + NA_WIN_H
DFT_HALF = 256
VMEM_LIMIT =56 * 1024 * 1024

C_HY = 0
C_QB = 3 * HY_W
C_KB = C_QB + GQA_HEADS * HEAD_DIM
C_VB = C_KB + GQA_KV * HEAD_DIM
C_QC = C_VB + GQA_KV * HEAD_DIM
C_KC = C_QC + NA_HEADS * HEAD_DIM
C_VC = C_KC + NA_HEADS * HEAD_DIM
C_QD = C_VC + NA_HEADS * HEAD_DIM
C_KD = C_QD + DF_HEADS * 2 * DF_QK_DIM
C_VD = C_KD + DF_HEADS * 2 * DF_QK_DIM
C_GATES = C_VD + DF_HEADS * HEAD_DIM


def _cp(*sem):
    return pltpu.CompilerParams(dimension_semantics=sem, vmem_limit_bytes=VMEM_LIMIT)


def _dot(a, b):
    return jnp.dot(a, b, preferred_element_type=F32)


def _dot_nt(a, b):
    return lax.dot_general(a, b, (((1,), (1,)), ((), ())), preferred_element_type=F32)


def _dot_hi(a, b):
    return jnp.dot(a, b, preferred_element_type=F32, precision=lax.Precision.HIGHEST)


def _sigmoid(x):
    return 0.5 * jnp.tanh(0.5 * x) + 0.5


def _mod_kernel(c_ref, w_ref, b_ref, o_ref):
    c = c_ref[...]
    a = (c * jax.nn.sigmoid(c)).astype(BF16)
    o_ref[...] = _dot(a, w_ref[...].astype(BF16)) + b_ref[...]


def _modulation(cc, w, b, l):
    depth, D, N = w.shape
    tn = math.gcd(N, 1024)
    return pl.pallas_call(
        _mod_kernel,
        grid=(N // tn,),
        in_specs=[pl.BlockSpec((8, D), lambda j: (0, 0)),
                  pl.BlockSpec((None, D, tn), lambda j: (l, 0, j)),
                  pl.BlockSpec((None, 1, tn), lambda j: (l, 0, j))],
        out_specs=pl.BlockSpec((8, tn), lambda j: (0, j)),
        out_shape=jax.ShapeDtypeStruct((8, N), F32),
        compiler_params=_cp("arbitrary"),
        name="modulation",
    )(cc, w, b.reshape(depth, 1, N))


def _prenorm_body(x, g_ref, sc_ref, sh_ref):
    y = x * lax.rsqrt(jnp.mean(x * x, axis=-1, keepdims=True) + NORM_EPS)
    return y * g_ref[...] * (1.0 + sc_ref[...]) + sh_ref[...]


def _prenorm_kernel(x_ref, c_ref, g_ref, sc_ref, sh_ref, o_ref, *, n_lat):
    i = pl.program_id(1)

    @pl.when(i < n_lat)
    def _():
        o_ref[...] = _prenorm_body(x_ref[...], g_ref, sc_ref, sh_ref).astype(BF16)

    @pl.when(i >= n_lat)
    def _():
        o_ref[...] = _prenorm_body(c_ref[...], g_ref, sc_ref, sh_ref).astype(BF16)


def _prenorm_router_kernel(x_ref, c_ref, g_ref, sc_ref, sh_ref, wr_ref, o_ref, a_ref, *, n_lat, n_exp):
    i = pl.program_id(1)

    def run(x):
        h = _prenorm_body(x, g_ref, sc_ref, sh_ref)
        o_ref[...] = h.astype(BF16)
        logits = _dot_hi(h, wr_ref[...])
        lane = lax.broadcasted_iota(jnp.int32, logits.shape, 1)
        logits = jnp.where(lane < n_exp, logits, NEG_INF)
        e = jnp.exp(logits - jnp.max(logits, axis=-1, keepdims=True))
        a_ref[...] = e / jnp.sum(e, axis=-1, keepdims=True)

    @pl.when(i < n_lat)
    def _():
        run(x_ref[...])

    @pl.when(i >= n_lat)
    def _():
        run(c_ref[...])


def _prenorm(x, ctx, gain, modt, q_scale, q_shift, n_ctx_tiles, w_router=None):
    B, T, D = x.shape
    n_lat = T // ROW_TILE
    nt = n_lat + n_ctx_tiles
    S = nt * ROW_TILE

    def mod_map(q):
        return lambda b, i: (jnp.where(i < n_lat, b, B), q, 0, 0)

    in_specs = [
        pl.BlockSpec((None, ROW_TILE, D), lambda b, i: (b, jnp.minimum(i, n_lat - 1), 0)),
        pl.BlockSpec((None, ROW_TILE, D), lambda b, i: (b, jnp.maximum(i - n_lat, 0), 0)),
        pl.BlockSpec((1, D), lambda b, i: (0, 0)),
        pl.BlockSpec((None, None, 1, D), mod_map(q_scale)),
        pl.BlockSpec((None, None, 1, D), mod_map(q_shift)),
    ]
    out_h = pl.BlockSpec((None, ROW_TILE, D), lambda b, i: (b, i, 0))
    shape_h = jax.ShapeDtypeStruct((B, S, D), BF16)
    args = [x, ctx, gain.reshape(1, D), modt, modt]
    if w_router is None:
        return pl.pallas_call(
            functools.partial(_prenorm_kernel, n_lat=n_lat),
            grid=(B, nt), in_specs=in_specs, out_specs=out_h, out_shape=shape_h,
            compiler_params=_cp("parallel", "arbitrary"), name="prenorm",
        )(*args)
    n_exp = w_router.shape[1]
    wr = jnp.pad(w_router, ((0, 0), (0, LANE - n_exp)))
    return pl.pallas_call(
        functools.partial(_prenorm_router_kernel, n_lat=n_lat, n_exp=n_exp),
        grid=(B, nt),
        in_specs=in_specs + [pl.BlockSpec((D, LANE), lambda b, i: (0, 0))],
        out_specs=[out_h, pl.BlockSpec((None, ROW_TILE, LANE), lambda b, i: (b, i, 0))],
        out_shape=[shape_h, jax.ShapeDtypeStruct((B, S, LANE), F32)],
        compiler_params=_cp("parallel", "arbitrary"), name="prenorm_router",
    )(*args, wr)


def _mm_kernel(a_ref, b_ref, o_ref, *, sigmoid):
    acc = _dot(a_ref[...], b_ref[...].astype(BF16))
    if sigmoid:
        acc = _sigmoid(acc)
    o_ref[...] = acc.astype(o_ref.dtype)


def _matmul(a, b, l, col0, N, out_dtype, sigmoid=False, name="matmul"):
    M, K = a.shape
    tm, tn = math.gcd(M, 1024), PROJ_TILE_N
    cb = col0 // tn
    return pl.pallas_call(
        functools.partial(_mm_kernel, sigmoid=sigmoid),
        grid=(M // tm, N // tn),
        in_specs=[pl.BlockSpec((tm, K), lambda i, j: (i, 0)),
                  pl.BlockSpec((None, K, tn), lambda i, j: (l, 0, cb + j))],
        out_specs=pl.BlockSpec((tm, tn), lambda i, j: (i, j)),
        out_shape=jax.ShapeDtypeStruct((M, N), out_dtype),
        compiler_params=_cp("parallel", "arbitrary"), name=name,
    )(a, b)


def _qk_post(x, g, seg, tables):
    lane = lax.broadcasted_iota(jnp.int32, (1, LANE), 1)
    sq = x * x
    if seg == LANE:
        ms = jnp.mean(sq, axis=-1, keepdims=True)
    else:
        lo = jnp.sum(jnp.where(lane < seg, sq, 0.0), axis=-1, keepdims=True)
        hi = jnp.sum(jnp.where(lane >= seg, sq, 0.0), axis=-1, keepdims=True)
        ms = jnp.where(lane < seg, lo, hi) * (1.0 / seg)
    y = x * lax.rsqrt(ms + NORM_EPS) * g
    if tables is not None:
        cos_ref, sin_ref = tables
        q = seg // 4
        partner = jnp.where((lane % (seg // 2)) < q, pltpu.roll(y, LANE - q, 1), pltpu.roll(y, q, 1))
        y = y * cos_ref[...] + partner * sin_ref[...]
    return y


def _proj_block_kinds():
    kinds = []
    for col in range(0, C_GATES, LANE):
        if C_QB <= col < C_VB:
            kinds.append((HEAD_DIM, True))
        elif C_QC <= col < C_VC:
            kinds.append((HEAD_DIM, False))
        elif C_QD <= col < C_VD:
            kinds.append((DF_QK_DIM, True))
        else:
            kinds.append(None)
    return kinds


def _inproj_kernel(a_ref, w_ref, g_ref, c128_ref, s128_ref, c64_ref, s64_ref, o_ref):
    j = pl.program_id(1)
    acc = _dot(a_ref[...], w_ref[...].astype(BF16))
    per_tile = PROJ_TILE_N // LANE
    kinds = _proj_block_kinds()
    tiles = [kinds[t * per_tile:(t + 1) * per_tile] for t in range(len(kinds) // per_tile)]
    plain = functools.reduce(jnp.logical_or, [j == t for t, ks in enumerate(tiles) if not any(ks)])

    @pl.when(plain)
    def _():
        o_ref[...] = acc.astype(o_ref.dtype)

    for t, ks in enumerate(tiles):
        if not any(ks):
            continue

        @pl.when(j == t)
        def _(ks=ks):
            for h, kind in enumerate(ks):
                y = acc[:, h * LANE:(h + 1) * LANE]
                if kind is not None:
                    seg, rope = kind
                    tables = None if not rope else ((c128_ref, s128_ref) if seg == HEAD_DIM else (c64_ref, s64_ref))
                    y = _qk_post(y, g_ref[h:h + 1, :], seg, tables)
                o_ref[:, h * LANE:(h + 1) * LANE] = y.astype(o_ref.dtype)


def _in_proj(h2d, w_in_bf, l, gains, rope_h, rope_d):
    M, K = h2d.shape
    tm, tn = math.gcd(M, 1024), PROJ_TILE_N
    per_tile = tn // LANE
    table = pl.BlockSpec((tm, LANE), lambda i, j: (i, 0))
    return pl.pallas_call(
        _inproj_kernel,
        grid=(M // tm, C_GATES // tn),
        in_specs=[pl.BlockSpec((tm, K), lambda i, j: (i, 0)),
                  pl.BlockSpec((None, K, tn), lambda i, j: (l, 0, j)),
                  pl.BlockSpec((None, per_tile, LANE), lambda i, j: (j, 0, 0)),
                  table, table, table, table],
        out_specs=pl.BlockSpec((tm, tn), lambda i, j: (i, j)),
        out_shape=jax.ShapeDtypeStruct((M, C_GATES), BF16),
        compiler_params=_cp("parallel", "arbitrary"), name="in_proj",
    )(h2d, w_in_bf, gains.reshape(-1, per_tile, LANE), *rope_h, *rope_d)


def _proj_gains(qn_b, kn_b, qn_c, kn_c, qn_d, kn_d):
    ones = lambda n: jnp.ones((n, LANE), F32)
    rep = lambda g, n: jnp.tile(jnp.tile(g.astype(F32), LANE // g.shape[0])[None, :], (n, 1))
    return jnp.concatenate([
        ones(C_QB // LANE), rep(qn_b, GQA_HEADS), rep(kn_b, GQA_KV), ones(GQA_KV),
        rep(qn_c, NA_HEADS), rep(kn_c, NA_HEADS), ones(NA_HEADS),
        rep(qn_d, DF_HEADS), rep(kn_d, DF_HEADS), ones(DF_HEADS)], axis=0)


def _rope_tables(B, T, S, seg):
    half = seg // 2
    nfreq = half // 2
    inv = ROPE_THETA ** (-jnp.arange(0, half, 2, dtype=F32) / half)
    pos = jnp.arange(T, dtype=jnp.int32)
    rows, cols = (pos // GRID_W).astype(F32), (pos % GRID_W).astype(F32)
    l = np.arange(LANE) % seg
    use_col = l >= half
    fidx = (l % half) % nfreq
    is_b = (l % half) >= nfreq
    ang = jnp.where(use_col[None, :], cols[:, None], rows[:, None]) * inv[fidx][None, :]
    cos, sin = jnp.cos(ang), jnp.sin(ang)
    sin = jnp.where(is_b[None, :], sin, -sin)
    cos = jnp.concatenate([cos, jnp.ones((S - T, LANE), F32)], axis=0)
    sin = jnp.concatenate([sin, jnp.zeros((S - T, LANE), F32)], axis=0)
    return jnp.tile(cos, (B, 1)), jnp.tile(sin, (B, 1))


def _softmax_rows(s, scale):
    c = scale * LOG2E
    m = jnp.max(s, axis=-1, keepdims=True)
    p = jnp.exp2(s * c - m * c)
    return p.astype(BF16), jnp.sum(p, axis=-1, keepdims=True)


def _attend(qs, ks, vs, scale):
    scores = [_dot_nt(q, k) for q, k in zip(qs, ks)]
    outs = []
    for s, v in zip(scores, vs):
        p, l = _softmax_rows(s, scale)
        outs.append(_dot(p, v) / l)
    return outs


def _attend_t(qts, ks, vts, scale):
    c = scale * LOG2E
    scores = [_dot(k, qt) for qt, k in zip(qts, ks)]
    outs = []
    for s, vt in zip(scores, vts):
        m = jnp.max(s, axis=0, keepdims=True)
        p = jnp.exp2(s * c - m * c)
        l = jnp.sum(p, axis=0, keepdims=True)
        outs.append(_dot(vt, p.astype(BF16)) / l)
    return outs


def _store_vt(vt_ref, v):
    for h in range(vt_ref.shape[0]):
        vt_ref[h] = v[:, h * LANE:(h + 1) * LANE].T


GQA_KV_PER_STEP = 2


def _gqa_kernel(q_ref, k_ref, v_ref, o_ref, vt_ref, *, T, R, scale):
    i = pl.program_id(2)

    @pl.when(i == 0)
    def _():
        _store_vt(vt_ref, v_ref[...])

    def attend(lo):
        n_heads = q_ref.shape[1] // LANE
        qts = [q_ref[:, h * LANE:(h + 1) * LANE].T for h in range(n_heads)]
        ks = [k_ref[lo:, (h // R) * LANE:(h // R + 1) * LANE] for h in range(n_heads)]
        outs = _attend_t(qts, ks, [vt_ref[h // R, :, lo:] for h in range(n_heads)], scale)
        for h in range(n_heads):
            o_ref[:, h * LANE:(h + 1) * LANE] = outs[h].T.astype(o_ref.dtype)

    @pl.when(i < T // ROW_TILE)
    def _():
        attend(0)

    @pl.when(i >= T // ROW_TILE)
    def _():
        attend(T)


def _gqa(proj, T, Tq, scale):
    B, S, _ = proj.shape
    R = GQA_HEADS // GQA_KV
    G = GQA_KV_PER_STEP
    qb, kb, vb = C_QB // (G * R * LANE), C_KB // (G * LANE), C_VB // (G * LANE)
    return pl.pallas_call(
        functools.partial(_gqa_kernel, T=T, R=R, scale=scale),
        grid=(B, GQA_KV // G, Tq // ROW_TILE),
        in_specs=[pl.BlockSpec((None, ROW_TILE, G * R * LANE), lambda b, g, i: (b, i, qb + g)),
                  pl.BlockSpec((None, S, G * LANE), lambda b, g, i: (b, 0, kb + g)),
                  pl.BlockSpec((None, S, G * LANE), lambda b, g, i: (b, 0, vb + g))],
        out_specs=pl.BlockSpec((None, ROW_TILE, G * R * LANE), lambda b, g, i: (b, i, g)),
        out_shape=jax.ShapeDtypeStruct((B, Tq, GQA_HEADS * LANE), BF16),
        scratch_shapes=[pltpu.VMEM((G, LANE, S), BF16)],
        compiler_params=_cp("parallel", "parallel", "arbitrary"), name="gqa",
    )(proj, proj, proj)


DF_HEADS_PER_STEP = 2


def _diff_kernel(lam_ref, q_ref, k_ref, v_ref, g_ref, o_ref, vt_ref, *, T, scale, out_scale):
    i = pl.program_id(2)
    lam = lam_ref[0]

    @pl.when(i == 0)
    def _():
        _store_vt(vt_ref, v_ref[...])

    def attend(lo):
        qts, ks, vts = [], [], []
        for h in range(DF_HEADS_PER_STEP):
            qt = q_ref[:, h * LANE:(h + 1) * LANE].T
            ch = lax.broadcasted_iota(jnp.int32, qt.shape, 0)
            zero = jnp.zeros_like(qt)
            qts += [jnp.where(ch < DF_QK_DIM, qt, zero), jnp.where(ch >= DF_QK_DIM, qt, zero)]
            ks += [k_ref[lo:, h * LANE:(h + 1) * LANE]] * 2
            vts += [vt_ref[h, :, lo:]] * 2
        outs = _attend_t(qts, ks, vts, scale)
        for h in range(DF_HEADS_PER_STEP):
            o = (outs[2 * h] - lam * outs[2 * h + 1]).T
            y = o * lax.rsqrt(jnp.mean(o * o, axis=-1, keepdims=True) + NORM_EPS) * g_ref[...]
            o_ref[:, h * LANE:(h + 1) * LANE] = (y * out_scale).astype(o_ref.dtype)

    @pl.when(i < T // ROW_TILE)
    def _():
        attend(0)

    @pl.when(i >= T // ROW_TILE)
    def _():
        attend(T)


def _diff_attn(lam, proj, subln, T, Tq, out_scale):
    B, S, _ = proj.shape
    W = DF_HEADS_PER_STEP * LANE
    qb, kb, vb = C_QD // W, C_KD // W, C_VD // W
    return pl.pallas_call(
        functools.partial(_diff_kernel, T=T, scale=DF_SCALE, out_scale=out_scale),
        grid=(B, DF_HEADS // DF_HEADS_PER_STEP, Tq // ROW_TILE),
        in_specs=[pl.BlockSpec(memory_space=pltpu.SMEM),
                  pl.BlockSpec((None, ROW_TILE, W), lambda b, h, i: (b, i, qb + h)),
                  pl.BlockSpec((None, S, W), lambda b, h, i: (b, 0, kb + h)),
                  pl.BlockSpec((None, S, W), lambda b, h, i: (b, 0, vb + h)),
                  pl.BlockSpec((1, LANE), lambda b, h, i: (0, 0))],
        out_specs=pl.BlockSpec((None, ROW_TILE, W), lambda b, h, i: (b, i, h)),
        out_shape=jax.ShapeDtypeStruct((B, Tq, DF_HEADS * LANE), BF16),
        scratch_shapes=[pltpu.VMEM((DF_HEADS_PER_STEP, LANE, S), BF16)],
        compiler_params=_cp("parallel", "parallel", "arbitrary"), name="diff_attn",
    )(lam.reshape(1).astype(F32), proj, proj, proj, subln.astype(F32).reshape(1, LANE))


NA_HEADS_PER_STEP = 4


def _na_kernel(q_ref, k_ref, v_ref, bias_ref, o_ref, *, T, scale):
    i = pl.program_id(2)
    n_groups = T // ROW_TILE
    grid_rows = T // GRID_W
    c = scale * LOG2E
    heads = [slice(h * LANE, (h + 1) * LANE) for h in range(NA_HEADS_PER_STEP)]
    qts = [q_ref[:, hs].T for hs in heads]
    u_cs = [_dot(k_ref[T:, hs], qt) * c for hs, qt in zip(heads, qts)]
    vcts = [v_ref[T:, hs].T for hs in heads]

    @pl.when(i < n_groups)
    def _():
        row0 = jnp.clip(i * NA_GROUP_ROWS - NA_WIN_H // 2, 0, grid_rows - NA_WIN_ROWS)
        win = pl.ds(pl.multiple_of(row0 * GRID_W, ROW_TILE), NA_WIN_ROWS * GRID_W)
        u_ns = [_dot(k_ref[win, hs], qt) * c + bias_ref[h] for h, (hs, qt) in enumerate(zip(heads, qts))]
        for h, hs in enumerate(heads):
            u_n, u_c = u_ns[h], u_cs[h]
            m = jnp.maximum(jnp.max(u_n, axis=0, keepdims=True), jnp.max(u_c, axis=0, keepdims=True))
            p_n, p_c = jnp.exp2(u_n - m), jnp.exp2(u_c - m)
            l = jnp.sum(p_n, axis=0, keepdims=True) + jnp.sum(p_c, axis=0, keepdims=True)
            ot = (_dot(v_ref[win, hs].T, p_n.astype(BF16)) + _dot(vcts[h], p_c.astype(BF16))) / l
            o_ref[:, hs] = ot.T.astype(o_ref.dtype)

    @pl.when(i >= n_groups)
    def _():
        for h, hs in enumerate(heads):
            u_c = u_cs[h]
            p = jnp.exp2(u_c - jnp.max(u_c, axis=0, keepdims=True))
            ot = _dot(vcts[h], p.astype(BF16)) / jnp.sum(p, axis=0, keepdims=True)
            o_ref[:, hs] = ot.T.astype(o_ref.dtype)


def _na_bias(rpb, T):
    rows = T // GRID_W
    n_groups = rows // NA_GROUP_ROWS
    n_roff, n_coff = 2 * NA_WIN_H - 1, 2 * NA_WIN_W - 1
    c = np.arange(GRID_W)[:, None]
    kc = np.arange(GRID_W)[None, :]
    cs = np.clip(c - NA_WIN_W // 2, 0, GRID_W - NA_WIN_W)
    col_valid = (kc >= cs) & (kc < cs + NA_WIN_W)
    coff = np.clip(kc - c + NA_WIN_W - 1, 0, n_coff - 1)
    col_sel = (coff[..., None] == np.arange(n_coff)).astype(np.float32)
    rpb = rpb.astype(F32) * LOG2E
    out = []
    for grp in (0, 1, n_groups - 1):
        r = grp * NA_GROUP_ROWS + np.arange(NA_GROUP_ROWS)[:, None]
        rs = np.clip(r - NA_WIN_H // 2, 0, rows - NA_WIN_H)
        row0 = np.clip(grp * NA_GROUP_ROWS - NA_WIN_H // 2, 0, rows - NA_WIN_ROWS)
        key_row = row0 + np.arange(NA_WIN_ROWS)[None, :]
        row_valid = (key_row >= rs) & (key_row < rs + NA_WIN_H)
        roff = np.clip(key_row - r + NA_WIN_H - 1, 0, n_roff - 1)
        row_sel = (roff[..., None] == np.arange(n_roff)).astype(np.float32)
        bias = jnp.einsum('ika,hab,cqb->hkqic', row_sel, rpb, col_sel, precision=lax.Precision.HIGHEST)
        valid = row_valid.T[:, None, :, None] & col_valid.T[None, :, None, :]
        out.append(jnp.where(valid[None], bias, NEG_INF).reshape(rpb.shape[0], -1, ROW_TILE))
    return jnp.stack(out)


def _na(proj, bias, T, Tq):
    B, S, _ = proj.shape
    W = NA_HEADS_PER_STEP * LANE
    qb, kb, vb = C_QC // W, C_KC // W, C_VC // W
    n_groups = T // ROW_TILE
    nk = NA_WIN_ROWS * GRID_W

    def bias_map(b, h, i):
        return (jnp.where(i == 0, 0, jnp.where(i >= n_groups - 1, 2, 1)), h, 0, 0)

    return pl.pallas_call(
        functools.partial(_na_kernel, T=T, scale=ATT_SCALE),
        grid=(B, NA_HEADS // NA_HEADS_PER_STEP, Tq // ROW_TILE),
        in_specs=[pl.BlockSpec((None, ROW_TILE, W), lambda b, h, i: (b, i, qb + h)),
                  pl.BlockSpec((None, S, W), lambda b, h, i: (b, 0, kb + h)),
                  pl.BlockSpec((None, S, W), lambda b, h, i: (b, 0, vb + h)),
                  pl.BlockSpec((None, NA_HEADS_PER_STEP, nk, ROW_TILE), bias_map)],
        out_specs=pl.BlockSpec((None, ROW_TILE, W), lambda b, h, i: (b, i, h)),
        out_shape=jax.ShapeDtypeStruct((B, Tq, NA_HEADS * LANE), BF16),
        compiler_params=_cp("parallel", "parallel", "arbitrary"), name="na_attn",
    )(proj, proj, proj, bias)


def _shortconv_kernel(*refs, T, with_ctx):
    u_refs, w_refs, o_refs = refs[0:3], refs[3:6], refs[6:]
    S = u_refs[0].shape[0]
    row = lax.broadcasted_iota(jnp.int32, (S, 1), 0)
    first = (row == 0) | (row == T)
    last = (row == T - 1) | (row == S - 1)
    for n in range(3):
        u = u_refs[n][...].astype(F32)
        w = w_refs[n][...]
        prev = jnp.where(first, 0.0, pltpu.roll(u, 1, 0))
        nxt = jnp.where(last, 0.0, pltpu.roll(u, S - 1, 0))
        y = (prev * w[0:1] + u * w[1:2] + nxt * w[2:3]).astype(BF16)
        o_refs[n][...] = y[:T]
        if with_ctx:
            o_refs[3 + n][...] = y[T:]


def _shortconv(proj, w, T, with_ctx):
    B, S, _ = proj.shape
    Lc = S - T
    nct = HY_W // LANE
    in_specs = [pl.BlockSpec((None, S, LANE), functools.partial(lambda b, c, n: (b, 0, n * nct + c), n=n))
                for n in range(3)]
    in_specs += [pl.BlockSpec((3, LANE), functools.partial(lambda b, c, n: (0, n * nct + c), n=n))
                 for n in range(3)]
    out_specs = [pl.BlockSpec((T, LANE), lambda b, c: (0, b * nct + c))] * 3
    out_shape = [jax.ShapeDtypeStruct((T, B * HY_W), BF16)] * 3
    if with_ctx:
        out_specs += [pl.BlockSpec((Lc, LANE), lambda b, c: (0, b * nct + c))] * 3
        out_shape += [jax.ShapeDtypeStruct((Lc, B * HY_W), BF16)] * 3
    return pl.pallas_call(
        functools.partial(_shortconv_kernel, T=T, with_ctx=with_ctx),
        grid=(B, nct), in_specs=in_specs, out_specs=out_specs, out_shape=out_shape,
        compiler_params=_cp("parallel", "arbitrary"), name="hy_shortconv",
    )(proj, proj, proj, w, w, w)


def _filter_hidden_kernel(z_ref, f1_ref, b1_ref, fr_ref, f2_ref, b2_ref, o_ref):
    fr = fr_ref[...]
    hid = jnp.sin(fr * (_dot_hi(z_ref[...], f1_ref[...]) + b1_ref[...]))
    o_ref[...] = jnp.sin(fr * (_dot_hi(hid, f2_ref[...]) + b2_ref[...]))


def _filter_kernel(h_ref, hb_ref, t_ref, tb_ref, f3f_ref, f3b_ref, dl_ref, *o_refs, circular):
    fw = _dot_hi(h_ref[...], f3f_ref[...]) * jnp.exp(-t_ref[...] * dl_ref[...])
    bw = _dot_hi(hb_ref[...], f3b_ref[...]) * jnp.exp(-tb_ref[...] * dl_ref[...])
    row = lax.broadcasted_iota(jnp.int32, bw.shape, 0)
    bw = jnp.where(row == 0, 0.0, bw)
    inv = 1.0 / (jnp.sum(jnp.abs(fw), axis=0, keepdims=True) + jnp.sum(jnp.abs(bw), axis=0, keepdims=True))
    if circular:
        o_refs[0][0] = (fw * inv).astype(BF16)
        o_refs[0][1] = (bw * inv).astype(BF16)
    else:
        o_refs[0][...] = ((fw + bw) * inv).astype(BF16)
        o_refs[1][...] = ((fw - bw) * inv).astype(BF16)


def _hyena_filters(L, f1, b1, freq, f2, b2, f3, circular):
    t = jnp.linspace(0.0, 1.0, L, dtype=F32)[:, None]
    w = (2.0 * math.pi / L) * jnp.arange(L, dtype=F32)[:, None]
    bands = jnp.linspace(1e-4, HY_BANDS - 1, HY_BANDS, dtype=F32)[None, :]
    z = jnp.concatenate([t, jnp.cos(bands * w), -jnp.sin(bands * w)], axis=-1)
    z = jnp.pad(z, ((0, 0), (0, LANE - HY_EMB)))
    f1p = jnp.pad(f1.astype(F32), ((0, LANE - HY_EMB), (0, 0)))
    ffn = f1.shape[1]
    small = lambda shape: pl.BlockSpec(shape, lambda j: (0, 0))
    hid = pl.pallas_call(
        _filter_hidden_kernel,
        grid=(1,),
        in_specs=[small((L, LANE)), small((LANE, ffn)), small((1, ffn)), small((1, ffn)), small((ffn, ffn)),
                  small((1, ffn))],
        out_specs=small((L, ffn)), out_shape=jax.ShapeDtypeStruct((L, ffn), F32),
        compiler_params=_cp("arbitrary"), name="hy_filter_hidden",
    )(z, f1p, b1.astype(F32).reshape(1, ffn), freq.astype(F32).reshape(1, ffn), f2.astype(F32),
      b2.astype(F32).reshape(1, ffn))
    back = (lambda a: jnp.roll(jnp.flip(a, axis=0), 1, axis=0)) if circular else (lambda a: a)
    deltas = jnp.abs(jnp.linspace(math.log(HY_TARGET) / HY_SLOW_DECAY, math.log(HY_TARGET) / HY_FAST_DECAY,
                                  HY_W, dtype=F32))
    NC = HY_ORDER * HY_W
    dl = jnp.tile(deltas, HY_ORDER).reshape(1, NC)
    tn = 256
    if circular:
        out_specs = [pl.BlockSpec((2, L, tn), lambda j: (0, 0, j))]
        out_shape = [jax.ShapeDtypeStruct((2, L, NC), BF16)]
    else:
        out_specs = [pl.BlockSpec((L, tn), lambda j: (0, j))] * 2
        out_shape = [jax.ShapeDtypeStruct((L, NC), BF16)] * 2
    out = pl.pallas_call(
        functools.partial(_filter_kernel, circular=circular),
        grid=(NC // tn,),
        in_specs=[small((L, ffn)), small((L, ffn)), small((L, 1)), small((L, 1)),
                  pl.BlockSpec((ffn, tn), lambda j: (0, j)),
                  pl.BlockSpec((ffn, tn), lambda j: (0, NC // tn + j)),
                  pl.BlockSpec((1, tn), lambda j: (0, j))],
        out_specs=out_specs, out_shape=out_shape,
        compiler_params=_cp("arbitrary"), name="hy_filter",
    )(hid, back(hid), t, back(t), f3.astype(F32), f3.astype(F32), dl)
    return out[0].reshape(2 * L, NC) if circular else out


def _dft_tables(L):
    N = 2 * L
    r = np.arange(N)
    k = (r // (2 * DFT_HALF)) * DFT_HALF + r % DFT_HALF
    is_im = (r // DFT_HALF) % 2 == 1
    nyq = is_im & (k == 0)
    kj = jnp.asarray(k, jnp.int32)[:, None]
    im, nyq = is_im[:, None], nyq[:, None]
    n_lo = 64

    def cos_sin(step, count):
        m = (kj * (step * jnp.arange(count, dtype=jnp.int32))[None, :]) % N
        ang = m.astype(F32) * (2.0 * math.pi / N)
        return jnp.cos(ang), jnp.sin(ang)

    c1, s1 = cos_sin(n_lo, L // n_lo)
    c0, s0 = cos_sin(1, n_lo)
    p1 = jnp.where(nyq, 1.0, jnp.where(im, -s1, c1))
    q1 = jnp.where(nyq, 0.0, jnp.where(im, -c1, -s1))
    p0 = jnp.where(nyq, (1 - 2 * (jnp.arange(n_lo) % 2)).astype(F32)[None, :], c0)
    a = (p1[:, :, None] * p0[:, None, :] + q1[:, :, None] * s0[:, None, :]).reshape(N, L).astype(BF16)
    return a, a.T


def _kf_kernel(a_ref, hs_ref, hd_ref, o_ref, *, n_fft):
    i = pl.program_id(0)
    H = DFT_HALF
    re = _dot(a_ref[:H, :], hs_ref[...])
    im = _dot(a_ref[H:, :], hd_ref[...])
    o_ref[:H, :] = re * (2.0 / n_fft)
    o_ref[H:, :] = im * (2.0 / n_fft)

    @pl.when(i == 0)
    def _():
        ny = _dot(a_ref[H:H + 16, :], hs_ref[...])
        o_ref[0:1, :] = re[0:1] * (1.0 / n_fft)
        o_ref[H:H + 1, :] = ny[0:1] * (1.0 / n_fft)


def _filter_spectrum(a, hs, hd):
    N, L = a.shape
    NC = hs.shape[1]
    tm, tn = 2 * DFT_HALF, 512
    return pl.pallas_call(
        functools.partial(_kf_kernel, n_fft=N),
        grid=(N // tm, NC // tn),
        in_specs=[pl.BlockSpec((tm, L), lambda i, j: (i, 0)),
                  pl.BlockSpec((L, tn), lambda i, j: (0, j)),
                  pl.BlockSpec((L, tn), lambda i, j: (0, j))],
        out_specs=pl.BlockSpec((tm, tn), lambda i, j: (i, j)),
        out_shape=jax.ShapeDtypeStruct((N, NC), F32),
        compiler_params=_cp("parallel", "arbitrary"), name="hy_filter_spectrum",
    )(a, hs, hd)


def _fwd_kernel(a_ref, z_ref, kf_ref, p_ref):
    i = pl.program_id(0)
    H = DFT_HALF
    acc = _dot(a_ref[...], z_ref[...])
    zr, zi = acc[:H], acc[H:]
    kr, ki = kf_ref[:H, :], kf_ref[H:, :]
    row = lax.broadcasted_iota(jnp.int32, zr.shape, 0)
    real_pair = (row == 0) & (i == 0)
    p_ref[:H, :] = jnp.where(real_pair, zr * kr, zr * kr - zi * ki).astype(BF16)
    p_ref[H:, :] = jnp.where(real_pair, zi * ki, zr * ki + zi * kr).astype(BF16)


def _dft_multiply(a, z, kf, order):
    N, L = a.shape
    NB = z.shape[1] // HY_W
    tm = 2 * DFT_HALF
    return pl.pallas_call(
        _fwd_kernel,
        grid=(N // tm, NB),
        in_specs=[pl.BlockSpec((tm, L), lambda i, j: (i, 0)),
                  pl.BlockSpec((L, HY_W), lambda i, j: (0, j)),
                  pl.BlockSpec((tm, HY_W), lambda i, j: (i, order))],
        out_specs=pl.BlockSpec((tm, HY_W), lambda i, j: (i, j)),
        out_shape=jax.ShapeDtypeStruct((N, NB * HY_W), BF16),
        compiler_params=_cp("parallel", "arbitrary"), name="hy_dft_multiply",
    )(a, z, kf)


def _inv_kernel(at_ref, p_ref, x_ref, z_ref, b_ref, o_ref):
    y = _dot(at_ref[...], p_ref[...])
    z = z_ref[...].astype(F32)
    o_ref[...] = (x_ref[...].astype(F32) * (y + b_ref[...] * z)).astype(o_ref.dtype)


def _idft_gate(at, p, xo, z, bias):
    L, N = at.shape
    NB = z.shape[1] // HY_W
    tm = min(L, 512)
    return pl.pallas_call(
        _inv_kernel,
        grid=(L // tm, NB),
        in_specs=[pl.BlockSpec((tm, N), lambda i, j: (i, 0)),
                  pl.BlockSpec((N, HY_W), lambda i, j: (0, j)),
                  pl.BlockSpec((tm, HY_W), lambda i, j: (i, j)),
                  pl.BlockSpec((tm, HY_W), lambda i, j: (i, j)),
                  pl.BlockSpec((1, HY_W), lambda i, j: (0, 0))],
        out_specs=pl.BlockSpec((tm, HY_W), lambda i, j: (i, j)),
        out_shape=jax.ShapeDtypeStruct((L, NB * HY_W), BF16),
        compiler_params=_cp("parallel", "arbitrary"), name="hy_idft_gate",
    )(at, p, xo, z, bias.astype(F32).reshape(1, HY_W))


def _hyena(vxx, tables, filt, hy_bias):
    a, at = tables
    kf = _filter_spectrum(a, *filt)
    z = vxx[0]
    for o in range(HY_ORDER):
        p = _dft_multiply(a, z, kf, o)
        z = _idft_gate(at, p, vxx[1 + o], z, hy_bias[o])
    return z


FFT_N2 = LANE
FFT_K1B = 8
FFT_LANES = 8192


def _fft_tables(L):
    N = 2 * L
    N1 = N // FFT_N2
    KH = -(-(N1 // 2 + 1) // FFT_K1B) * FFT_K1B
    k1 = jnp.arange(KH, dtype=jnp.int32)
    n1 = jnp.arange(N1, dtype=jnp.int32)
    ang1 = ((k1[:, None] * n1[None, :]) % N1).astype(F32) * (2.0 * math.pi / N1)
    c1, s1 = jnp.cos(ang1), jnp.sin(ang1)
    f1 = jnp.concatenate([c1, -s1], axis=0).astype(BF16)
    mult = jnp.where((k1 == 0) | (k1 == N1 // 2), 1.0, jnp.where(k1 < N1 // 2, 2.0, 0.0))[:, None]
    g2 = jnp.concatenate([c1 * mult, -s1 * mult], axis=0)[:, :N1 // 2].T.astype(BF16)
    k2 = jnp.arange(FFT_N2, dtype=jnp.int32)
    kk = k1[:, None, None] + N1 * k2[None, :, None]
    th = ((kk * k2[None, None, :]) % N).astype(F32) * (2.0 * math.pi / N)
    mr, mi = jnp.cos(th), -jnp.sin(th)
    wf = jnp.concatenate([jnp.concatenate([mr, -mi], axis=2),
                          jnp.concatenate([mi, mr], axis=2)], axis=1).astype(BF16)
    return f1, wf, jnp.swapaxes(wf, 1, 2), g2


def _fft_s1_kernel(f_ref, x_ref, o_ref):
    o_ref[...] = _dot(f_ref[...], x_ref[...]).astype(o_ref.dtype)


def _fft_stage1(f, x2d):
    R, K = f.shape
    M2 = x2d.shape[1]
    tn = math.gcd(M2, FFT_LANES)
    return pl.pallas_call(
        _fft_s1_kernel,
        grid=(M2 // tn,),
        in_specs=[pl.BlockSpec((R, K), lambda j: (0, 0)),
                  pl.BlockSpec((K, tn), lambda j: (0, j))],
        out_specs=pl.BlockSpec((R, tn), lambda j: (0, j)),
        out_shape=jax.ShapeDtypeStruct((R, M2), BF16),
        compiler_params=_cp("arbitrary"), name="hy_fft_stage1",
    )(f, x2d)


def _fft_s2_kernel(*refs, with_filter, scale):
    if with_filter:
        w_ref, a_ref, kf_ref, o_ref = refs
    else:
        w_ref, a_ref, o_ref = refs
    H = FFT_N2
    for kk in range(FFT_K1B):
        a = jnp.concatenate([a_ref[0, kk], a_ref[1, kk]], axis=0)
        x = _dot(w_ref[kk], a)
        xr, xi = x[:H], x[H:]
        if with_filter:
            kr, ki = kf_ref[0, kk], kf_ref[1, kk]
            xr, xi = xr * kr - xi * ki, xr * ki + xi * kr
        elif scale != 1.0:
            xr, xi = xr * scale, xi * scale
        o_ref[0, kk] = xr.astype(o_ref.dtype)
        o_ref[1, kk] = xi.astype(o_ref.dtype)


def _fft_stage2(w, a4, kf4=None, order=0, out_dtype=BF16, scale=1.0):
    _, KH, N2, cols = a4.shape
    tn = HY_W
    blk = lambda col: pl.BlockSpec((2, FFT_K1B, N2, tn), col)
    in_specs = [pl.BlockSpec((FFT_K1B, 2 * N2, 2 * N2), lambda i, j: (i, 0, 0)),
                blk(lambda i, j: (0, i, 0, j))]
    args = [w, a4]
    if kf4 is not None:
        in_specs.append(blk(lambda i, j: (0, i, 0, order)))
        args.append(kf4)
    return pl.pallas_call(
        functools.partial(_fft_s2_kernel, with_filter=kf4 is not None, scale=scale),
        grid=(KH // FFT_K1B, cols // tn),
        in_specs=in_specs,
        out_specs=blk(lambda i, j: (0, i, 0, j)),
        out_shape=jax.ShapeDtypeStruct(a4.shape, out_dtype),
        compiler_params=_cp("parallel", "arbitrary"), name="hy_fft_stage2",
    )(*args)


def _ifft_s2_kernel(g_ref, q_ref, x_ref, z_ref, b_ref, o_ref):
    y = _dot(g_ref[...], q_ref[...])
    z = z_ref[...].astype(F32)
    o_ref[...] = (x_ref[...].astype(F32) * (y + b_ref[...] * z)).astype(o_ref.dtype)


def _ifft_stage2_gate(g2, q4, xo, z, bias):
    _, KH, N2, cols = q4.shape
    H1 = g2.shape[0]
    L = z.shape[0]
    M2 = N2 * cols
    tn = math.gcd(M2, FFT_LANES)
    bias_t = jnp.tile(bias.astype(F32), tn // HY_W).reshape(1, tn)
    row = pl.BlockSpec((H1, tn), lambda j: (0, j))
    out = pl.pallas_call(
        _ifft_s2_kernel,
        grid=(M2 // tn,),
        in_specs=[pl.BlockSpec((H1, 2 * KH), lambda j: (0, 0)),
                  pl.BlockSpec((2 * KH, tn), lambda j: (0, j)),
                  row, row,
                  pl.BlockSpec((1, tn), lambda j: (0, 0))],
        out_specs=row,
        out_shape=jax.ShapeDtypeStruct((H1, M2), BF16),
        compiler_params=_cp("arbitrary"), name="hy_ifft_stage2_gate",
    )(g2, q4.reshape(2 * KH, M2), xo.reshape(H1, M2), z.reshape(H1, M2), bias_t)
    return out.reshape(L, cols)


def _hyena_fft(vxx, tables, kern, hy_bias):
    f1, wf, wi, g2 = tables
    L, cols = vxx[0].shape
    KH, N1 = f1.shape[0] // 2, f1.shape[1]
    N = 2 * L
    NC = kern.shape[1]
    ka = _fft_stage1(f1, kern.reshape(N1, FFT_N2 * NC)).reshape(2, KH, FFT_N2, NC)
    kf = _fft_stage2(wf, ka, out_dtype=F32, scale=1.0 / N)
    z = vxx[0]
    for o in range(HY_ORDER):
        a = _fft_stage1(f1[:, :N1 // 2], z.reshape(N1 // 2, FFT_N2 * cols)).reshape(2, KH, FFT_N2, cols)
        p = _fft_stage2(wf, a, kf4=kf, order=o)
        q = _fft_stage2(wi, p)
        z = _ifft_stage2_gate(g2, q, vxx[1 + o], z, hy_bias[o])
    return z


def _merge_kernel(ya_ref, yb_ref, yc_ref, yd_ref, g0_ref, g1_ref, g2_ref, g3_ref, wb_ref, wo_ref, x_ref, gate_ref,
                  o_ref):
    ys = (ya_ref, yb_ref, yc_ref, yd_ref)
    gs = (g0_ref, g1_ref, g2_ref, g3_ref)
    acc = _dot(ys[0][...], wb_ref[0]) * gs[0][...].astype(F32)
    for n in range(1, N_BRANCH):
        acc += _dot(ys[n][...], wb_ref[n]) * gs[n][...].astype(F32)
    o_ref[...] = x_ref[...] + gate_ref[...] * _dot(acc.astype(BF16), wo_ref[...])


def _merge(ys, gates, wb, wo, l, x, modt, q_gate, tm, row_tile0, mod_row):
    B, R, D = x.shape
    row = lambda b, i: (b, row_tile0 + i, 0)
    in_specs = [pl.BlockSpec((tm, BRANCH_W), lambda b, i: (i, b))]
    in_specs += [pl.BlockSpec((None, tm, BRANCH_W), row)] * (N_BRANCH - 1)
    in_specs += [pl.BlockSpec((None, tm, D), functools.partial(lambda b, i, n: (b, row_tile0 + i, n), n=n))
                 for n in range(N_BRANCH)]
    in_specs += [pl.BlockSpec((None, N_BRANCH, BRANCH_W, D), lambda b, i: (l, 0, 0, 0),
                              pipeline_mode=pl.Buffered(1)),
                 pl.BlockSpec((None, D, D), lambda b, i: (l, 0, 0), pipeline_mode=pl.Buffered(1)),
                 pl.BlockSpec((None, tm, D), lambda b, i: (b, i, 0)),
                 pl.BlockSpec((None, None, 1, D), lambda b, i: (mod_row(b), q_gate, 0, 0))]
    return pl.pallas_call(
        _merge_kernel,
        grid=(B, R // tm), in_specs=in_specs,
        out_specs=pl.BlockSpec((None, tm, D), lambda b, i: (b, i, 0)),
        out_shape=jax.ShapeDtypeStruct((B, R, D), F32),
        compiler_params=_cp("parallel", "arbitrary"), name="merge",
    )(*ys, gates, gates, gates, gates, wb, wo, x, modt)


def _expert_up_kernel(x_ref, w1_ref, w3_ref, o_ref):
    x = x_ref[...]
    a = _dot(x, w1_ref[...].astype(BF16))
    b = _dot(x, w3_ref[...].astype(BF16))
    o_ref[...] = (a * _sigmoid(a) * b).astype(o_ref.dtype)


def _expert_down_kernel(h_ref, g_ref, w2_ref, o_ref):
    o_ref[...] = (_dot(h_ref[...], w2_ref[...].astype(BF16)) * g_ref[...]).astype(o_ref.dtype)


def _experts(xg, gate, w1, w3, w2, l):
    E, M, D = xg.shape
    F = w1.shape[3]
    tf = 256
    hmid = pl.pallas_call(
        _expert_up_kernel,
        grid=(E, F // tf),
        in_specs=[pl.BlockSpec((None, M, D), lambda e, f: (e, 0, 0)),
                  pl.BlockSpec((None, None, D, tf), lambda e, f: (l, e, 0, f)),
                  pl.BlockSpec((None, None, D, tf), lambda e, f: (l, e, 0, f))],
        out_specs=pl.BlockSpec((None, M, tf), lambda e, f: (e, 0, f)),
        out_shape=jax.ShapeDtypeStruct((E, M, F), BF16),
        compiler_params=_cp("parallel", "arbitrary"), name="expert_up",
    )(xg, w1, w3)
    return pl.pallas_call(
        _expert_down_kernel,
        grid=(E, D // tf),
        in_specs=[pl.BlockSpec((None, M, F), lambda e, n: (e, 0, 0)),
                  pl.BlockSpec((None, M, 1), lambda e, n: (e, 0, 0)),
                  pl.BlockSpec((None, None, F, tf), lambda e, n: (l, e, 0, n))],
        out_specs=pl.BlockSpec((None, M, tf), lambda e, n: (e, 0, n)),
        out_shape=jax.ShapeDtypeStruct((E, M, D), BF16),
        compiler_params=_cp("parallel", "arbitrary"), name="expert_down",
    )(hmid, gate, w2)


def _combine_kernel(tok_ref, y_ref, x_ref, g_ref, o_ref, *, row0):
    E, Ct, tn = y_ref.shape
    rowid = row0 + pl.program_id(2) * ROW_TILE + lax.broadcasted_iota(jnp.int32, (ROW_TILE, 1), 0)
    onehot = jnp.where(tok_ref[...] == rowid, 1.0, 0.0).astype(BF16)
    o_ref[...] = x_ref[...] + g_ref[...] * _dot(onehot, y_ref[...].reshape(E * Ct, tn))


def _combine(tok, y, x, modt, q_gate, row0, mod_row):
    B, R, D = x.shape
    E, _, Ct, _ = y.shape
    tn = math.gcd(D, PROJ_TILE_N)
    return pl.pallas_call(
        functools.partial(_combine_kernel, row0=row0),
        grid=(B, D // tn, R // ROW_TILE),
        in_specs=[pl.BlockSpec((None, 1, E * Ct), lambda b, n, t: (b, 0, 0)),
                  pl.BlockSpec((E, None, Ct, tn), lambda b, n, t: (0, b, 0, n)),
                  pl.BlockSpec((None, ROW_TILE, tn), lambda b, n, t: (b, t, n)),
                  pl.BlockSpec((None, None, 1, tn), lambda b, n, t: (mod_row(b), q_gate, 0, n))],
        out_specs=pl.BlockSpec((None, ROW_TILE, tn), lambda b, n, t: (b, t, n)),
        out_shape=jax.ShapeDtypeStruct((B, R, D), F32),
        compiler_params=_cp("parallel", "parallel", "arbitrary"), name="moe_combine",
    )(tok, y, x, modt)


def _moe(h2, aff, T, Lc, w1, w3, w2, l):
    B, S, D = h2.shape
    E = w1.shape[1]

    def route(a):
        n = a.shape[1]
        return lax.top_k(a.transpose(0, 2, 1), CAPACITY_FACTOR * n // E)

    gate, tok = route(aff[:, :T, :E])
    if Lc:
        gate_c, tok_c = route(aff[:, T:, :E])
        gate, tok = jnp.concatenate([gate, gate_c], axis=-1), jnp.concatenate([tok, T + tok_c], axis=-1)
    Ct = tok.shape[-1]
    rows = (tok + (jnp.arange(B, dtype=jnp.int32) * S)[:, None, None]).transpose(1, 0, 2).reshape(E, B * Ct)
    xg = h2.reshape(B * S, D)[rows]
    y = _experts(xg, gate.transpose(1, 0, 2).reshape(E, B * Ct, 1), w1, w3, w2, l)
    return tok.reshape(B, 1, E * Ct), y.reshape(E, B, Ct, D)


def kernel(x, c, ctx, c_ctx, w_mod, b_mod, norm1, norm2, w_in, hy_short, hy_f1, hy_b1, hy_freq, hy_f2, hy_b2, hy_f3, hy_bias, qn_b, kn_b, qn_c, kn_c, rpb_c, qn_d, kn_d, lam_q1, lam_k1, lam_q2, lam_k2, subln_d, w_branch, w_out, w_router, w_e1, w_e3, w_e2):
    B, T, D = x.shape
    Lc = ctx.shape[1]
    S = T + Lc
    depth = w_mod.shape[0]
    assert T % ROW_TILE == 0 and Lc == ROW_TILE and T // GRID_W >= NA_WIN_ROWS

    cc = jnp.concatenate([c, c_ctx[None, :], jnp.zeros((8 - B - 1, D), F32)], axis=0)
    rope_h = _rope_tables(B, T, S, HEAD_DIM)
    rope_d = _rope_tables(B, T, S, DF_QK_DIM)
    use_fft = (2 * T // FFT_N2) % 32 == 0
    dft_lat = _fft_tables(T) if use_fft else _dft_tables(T)
    dft_ctx = _dft_tables(Lc)
    wb, wo = w_branch.astype(BF16), w_out.astype(BF16)

    for l in range(depth):
        last = l == depth - 1
        n_ctx = 0 if last else 1
        Tq = T + n_ctx * ROW_TILE
        lam_init = 0.8 - 0.6 * math.exp(-0.3 * l)
        lam = (jnp.exp(jnp.sum(lam_q1[l].astype(F32) * lam_k1[l].astype(F32)))
               - jnp.exp(jnp.sum(lam_q2[l].astype(F32) * lam_k2[l].astype(F32))) + lam_init)
        modt = _modulation(cc, w_mod, b_mod, l).reshape(8, 6, 1, D)

        h = _prenorm(x, ctx, norm1[l], modt, 1, 0, 1)
        h2d = h.reshape(B * S, D)
        gains = _proj_gains(qn_b[l], kn_b[l], qn_c[l], kn_c[l], qn_d[l], kn_d[l])
        proj = _in_proj(h2d, w_in, l, gains, rope_h, rope_d).reshape(B, S, C_GATES)
        gates = _matmul(h2d, w_in, l, C_GATES, N_BRANCH * D, BF16, sigmoid=True,
                        name="in_proj_gates").reshape(B, S, N_BRANCH * D)

        y_b = _gqa(proj, T, Tq, ATT_SCALE)
        y_c = _na(proj, _na_bias(rpb_c[l], T), T, Tq)
        y_d = _diff_attn(lam, proj, subln_d[l], T, Tq, 1.0 - lam_init)

        conv = _shortconv(proj, hy_short[l].astype(F32), T, with_ctx=not last)
        filt = _hyena_filters(T, hy_f1[l], hy_b1[l], hy_freq[l], hy_f2[l], hy_b2[l], hy_f3[l], circular=use_fft)
        y_a = (_hyena_fft if use_fft else _hyena)(conv[:3], dft_lat, filt, hy_bias[l])
        x_mid = _merge((y_a, y_b, y_c, y_d), gates, wb, wo, l, x, modt, 2, ROW_TILE, 0, lambda b: b)
        if not last:
            filt_c = _hyena_filters(Lc, hy_f1[l], hy_b1[l], hy_freq[l], hy_f2[l], hy_b2[l], hy_f3[l],
                                    circular=False)
            y_a_ctx = _hyena(conv[3:], dft_ctx, filt_c, hy_bias[l])
            ctx = _merge((y_a_ctx, y_b, y_c, y_d), gates, wb, wo, l, ctx, modt, 2, ROW_TILE, T // ROW_TILE,
                         lambda b: B)
        x = x_mid

        h2, aff = _prenorm(x, ctx, norm2[l], modt, 4, 3, n_ctx, w_router=w_router[l].astype(F32))
        tok, y = _moe(h2, aff, T, n_ctx * Lc, w_e1, w_e3, w_e2, l)
        x = _combine(tok, y, x, modt, 5, 0, lambda b: b)
        if not last:
            ctx = _combine(tok, y, ctx, modt, 5, T, lambda b: B)
    return x
```

```python
import functools
import math

import numpy as np
import jax
import jax.numpy as jnp
from jax import lax
from jax.experimental import pallas as pl
from jax.experimental.pallas import tpu as pltpu

F32 = jnp.float32
BF16 = jnp.bfloat16

GRID_W = 64
HEAD_DIM = 128
BRANCH_W = 512
N_BRANCH = 4
HY_W = 512
HY_ORDER = 2
HY_BANDS = 16
HY_EMB = 1 + 2 * HY_BANDS
HY_FAST_DECAY = 0.3
HY_SLOW_DECAY = 1.5
HY_TARGET = 1e-2
GQA_HEADS = 4
GQA_KV = 2
NA_HEADS = 4
NA_WIN_H = 8
NA_WIN_W = 16
DF_HEADS = 4
DF_QK_DIM = 64
CAPACITY_FACTOR = 2
ROPE_THETA = 10000.0
NORM_EPS = 1e-6
ATT_SCALE = HEAD_DIM ** -0.5
DF_SCALE = DF_QK_DIM ** -0.5
NEG_INF = -1e30
LOG2E = math.log2(math.e)

LANE = 128
ROW_TILE = 256
PROJ_TILE_N = 512
NA_GROUP_ROWS = 4
NA_WIN_ROWS = NA_GROUP_ROWS + NA_WIN_H
DFT_HALF = 256
VMEM_LIMIT =56 * 1024 * 1024

C_QB = 3 * HY_W
C_KB = C_QB + GQA_HEADS * HEAD_DIM
C_VB = C_KB + GQA_KV * HEAD_DIM
C_QC = C_VB + GQA_KV * HEAD_DIM
C_KC = C_QC + NA_HEADS * HEAD_DIM
C_VC = C_KC + NA_HEADS * HEAD_DIM
C_QD = C_VC + NA_HEADS * HEAD_DIM
C_KD = C_QD + DF_HEADS * 2 * DF_QK_DIM
C_VD = C_KD + DF_HEADS * 2 * DF_QK_DIM
C_GATES = C_VD + DF_HEADS * HEAD_DIM


def _cp(*sem):
    return pltpu.CompilerParams(dimension_semantics=sem, vmem_limit_bytes=VMEM_LIMIT)


def _dot(a, b):
    return jnp.dot(a, b, preferred_element_type=F32)


def _dot_hi(a, b):
    return jnp.dot(a, b, preferred_element_type=F32, precision=lax.Precision.HIGHEST)


def _sigmoid(x):
    return 0.5 * jnp.tanh(0.5 * x) + 0.5


def _mod_kernel(c_ref, w_ref, b_ref, o_ref):
    c = c_ref[...]
    a = (c * jax.nn.sigmoid(c)).astype(BF16)
    o_ref[...] = _dot(a, w_ref[...].astype(BF16)) + b_ref[...]


def _modulation(cc, w, b, l):
    depth, D, N = w.shape
    tn = math.gcd(N, 1024)
    return pl.pallas_call(
        _mod_kernel,
        grid=(N // tn,),
        in_specs=[pl.BlockSpec((8, D), lambda j: (0, 0)),
                  pl.BlockSpec((None, D, tn), lambda j: (l, 0, j)),
                  pl.BlockSpec((None, 1, tn), lambda j: (l, 0, j))],
        out_specs=pl.BlockSpec((8, tn), lambda j: (0, j)),
        out_shape=jax.ShapeDtypeStruct((8, N), F32),
        compiler_params=_cp("arbitrary"),
        name="modulation",
    )(cc, w, b.reshape(depth, 1, N))


def _prenorm_body(x, g_ref, sc_ref, sh_ref):
    y = x * lax.rsqrt(jnp.mean(x * x, axis=-1, keepdims=True) + NORM_EPS)
    return y * g_ref[...] * (1.0 + sc_ref[...]) + sh_ref[...]


def _prenorm_kernel(x_ref, c_ref, g_ref, sc_ref, sh_ref, o_ref, *, n_lat):
    i = pl.program_id(1)

    @pl.when(i < n_lat)
    def _():
        o_ref[...] = _prenorm_body(x_ref[...], g_ref, sc_ref, sh_ref).astype(BF16)

    @pl.when(i >= n_lat)
    def _():
        o_ref[...] = _prenorm_body(c_ref[...], g_ref, sc_ref, sh_ref).astype(BF16)


def _prenorm_router_kernel(x_ref, c_ref, g_ref, sc_ref, sh_ref, wr_ref, o_ref, a_ref, *, n_lat, n_exp):
    i = pl.program_id(1)

    def run(x):
        h = _prenorm_body(x, g_ref, sc_ref, sh_ref)
        o_ref[...] = h.astype(BF16)
        logits = _dot_hi(h, wr_ref[...])
        lane = lax.broadcasted_iota(jnp.int32, logits.shape, 1)
        logits = jnp.where(lane < n_exp, logits, NEG_INF)
        e = jnp.exp(logits - jnp.max(logits, axis=-1, keepdims=True))
        a_ref[...] = e / jnp.sum(e, axis=-1, keepdims=True)

    @pl.when(i < n_lat)
    def _():
        run(x_ref[...])

    @pl.when(i >= n_lat)
    def _():
        run(c_ref[...])


def _prenorm(x, ctx, gain, modt, q_scale, q_shift, n_ctx_tiles, w_router=None):
    B, T, D = x.shape
    n_lat = T // ROW_TILE
    nt = n_lat + n_ctx_tiles
    S = nt * ROW_TILE

    def mod_map(q):
        return lambda b, i: (jnp.where(i < n_lat, b, B), q, 0, 0)

    in_specs = [
        pl.BlockSpec((None, ROW_TILE, D), lambda b, i: (b, jnp.minimum(i, n_lat - 1), 0)),
        pl.BlockSpec((None, ROW_TILE, D), lambda b, i: (b, jnp.maximum(i - n_lat, 0), 0)),
        pl.BlockSpec((1, D), lambda b, i: (0, 0)),
        pl.BlockSpec((None, None, 1, D), mod_map(q_scale)),
        pl.BlockSpec((None, None, 1, D), mod_map(q_shift)),
    ]
    out_h = pl.BlockSpec((None, ROW_TILE, D), lambda b, i: (b, i, 0))
    shape_h = jax.ShapeDtypeStruct((B, S, D), BF16)
    args = [x, ctx, gain.reshape(1, D), modt, modt]
    if w_router is None:
        return pl.pallas_call(
            functools.partial(_prenorm_kernel, n_lat=n_lat),
            grid=(B, nt), in_specs=in_specs, out_specs=out_h, out_shape=shape_h,
            compiler_params=_cp("parallel", "arbitrary"), name="prenorm",
        )(*args)
    n_exp = w_router.shape[1]
    wr = jnp.pad(w_router, ((0, 0), (0, LANE - n_exp)))
    return pl.pallas_call(
        functools.partial(_prenorm_router_kernel, n_lat=n_lat, n_exp=n_exp),
        grid=(B, nt),
        in_specs=in_specs + [pl.BlockSpec((D, LANE), lambda b, i: (0, 0))],
        out_specs=[out_h, pl.BlockSpec((None, ROW_TILE, LANE), lambda b, i: (b, i, 0))],
        out_shape=[shape_h, jax.ShapeDtypeStruct((B, S, LANE), F32)],
        compiler_params=_cp("parallel", "arbitrary"), name="prenorm_router",
    )(*args, wr)


def _mm_kernel(a_ref, b_ref, o_ref, *, sigmoid):
    acc = _dot(a_ref[...], b_ref[...].astype(BF16))
    if sigmoid:
        acc = _sigmoid(acc)
    o_ref[...] = acc.astype(o_ref.dtype)


def _matmul(a, b, l, col0, N, out_dtype, sigmoid=False, name="matmul"):
    M, K = a.shape
    tm, tn = math.gcd(M, 1024), PROJ_TILE_N
    cb = col0 // tn
    return pl.pallas_call(
        functools.partial(_mm_kernel, sigmoid=sigmoid),
        grid=(M // tm, N // tn),
        in_specs=[pl.BlockSpec((tm, K), lambda i, j: (i, 0)),
                  pl.BlockSpec((None, K, tn), lambda i, j: (l, 0, cb + j))],
        out_specs=pl.BlockSpec((tm, tn), lambda i, j: (i, j)),
        out_shape=jax.ShapeDtypeStruct((M, N), out_dtype),
        compiler_params=_cp("parallel", "arbitrary"), name=name,
    )(a, b)


def _qk_post(x, g, seg, tables):
    lane = lax.broadcasted_iota(jnp.int32, (1, LANE), 1)
    sq = x * x
    if seg == LANE:
        ms = jnp.mean(sq, axis=-1, keepdims=True)
    else:
        lo = jnp.sum(jnp.where(lane < seg, sq, 0.0), axis=-1, keepdims=True)
        hi = jnp.sum(jnp.where(lane >= seg, sq, 0.0), axis=-1, keepdims=True)
        ms = jnp.where(lane < seg, lo, hi) * (1.0 / seg)
    y = x * lax.rsqrt(ms + NORM_EPS) * g
    if tables is not None:
        cos_ref, sin_ref = tables
        q = seg // 4
        partner = jnp.where((lane % (seg // 2)) < q, pltpu.roll(y, LANE - q, 1), pltpu.roll(y, q, 1))
        y = y * cos_ref[...] + partner * sin_ref[...]
    return y


def _proj_block_kinds():
    kinds = []
    for col in range(0, C_GATES, LANE):
        if C_QB <= col < C_VB:
            kinds.append((HEAD_DIM, True))
        elif C_QC <= col < C_VC:
            kinds.append((HEAD_DIM, False))
        elif C_QD <= col < C_VD:
            kinds.append((DF_QK_DIM, True))
        else:
            kinds.append(None)
    return kinds


def _inproj_kernel(a_ref, w_ref, g_ref, c128_ref, s128_ref, c64_ref, s64_ref, o_ref):
    j = pl.program_id(1)
    acc = _dot(a_ref[...], w_ref[...].astype(BF16))
    per_tile = PROJ_TILE_N // LANE
    kinds = _proj_block_kinds()
    tiles = [kinds[t * per_tile:(t + 1) * per_tile] for t in range(len(kinds) // per_tile)]
    plain = functools.reduce(jnp.logical_or, [j == t for t, ks in enumerate(tiles) if not any(ks)])

    @pl.when(plain)
    def _():
        o_ref[...] = acc.astype(o_ref.dtype)

    for t, ks in enumerate(tiles):
        if not any(ks):
            continue

        @pl.when(j == t)
        def _(ks=ks):
            for h, kind in enumerate(ks):
                y = acc[:, h * LANE:(h + 1) * LANE]
                if kind is not None:
                    seg, rope = kind
                    tables = None if not rope else ((c128_ref, s128_ref) if seg == HEAD_DIM else (c64_ref, s64_ref))
                    y = _qk_post(y, g_ref[h:h + 1, :], seg, tables)
                o_ref[:, h * LANE:(h + 1) * LANE] = y.astype(o_ref.dtype)


def _in_proj(h2d, w_in_bf, l, gains, rope_h, rope_d):
    M, K = h2d.shape
    tm, tn = math.gcd(M, 1024), PROJ_TILE_N
    per_tile = tn // LANE
    table = pl.BlockSpec((tm, LANE), lambda i, j: (i, 0))
    return pl.pallas_call(
        _inproj_kernel,
        grid=(M // tm, C_GATES // tn),
        in_specs=[pl.BlockSpec((tm, K), lambda i, j: (i, 0)),
                  pl.BlockSpec((None, K, tn), lambda i, j: (l, 0, j)),
                  pl.BlockSpec((None, per_tile, LANE), lambda i, j: (j, 0, 0)),
                  table, table, table, table],
        out_specs=pl.BlockSpec((tm, tn), lambda i, j: (i, j)),
        out_shape=jax.ShapeDtypeStruct((M, C_GATES), BF16),
        compiler_params=_cp("parallel", "arbitrary"), name="in_proj",
    )(h2d, w_in_bf, gains.reshape(-1, per_tile, LANE), *rope_h, *rope_d)


def _proj_gains(qn_b, kn_b, qn_c, kn_c, qn_d, kn_d):
    ones = lambda n: jnp.ones((n, LANE), F32)
    rep = lambda g, n: jnp.tile(jnp.tile(g.astype(F32), LANE // g.shape[0])[None, :], (n, 1))
    return jnp.concatenate([
        ones(C_QB // LANE), rep(qn_b, GQA_HEADS), rep(kn_b, GQA_KV), ones(GQA_KV),
        rep(qn_c, NA_HEADS), rep(kn_c, NA_HEADS), ones(NA_HEADS),
        rep(qn_d, DF_HEADS), rep(kn_d, DF_HEADS), ones(DF_HEADS)], axis=0)


def _rope_tables(B, T, S, seg):
    half = seg // 2
    nfreq = half // 2
    inv = ROPE_THETA ** (-jnp.arange(0, half, 2, dtype=F32) / half)
    pos = jnp.arange(T, dtype=jnp.int32)
    rows, cols = (pos // GRID_W).astype(F32), (pos % GRID_W).astype(F32)
    l = np.arange(LANE) % seg
    use_col = l >= half
    fidx = (l % half) % nfreq
    is_b = (l % half) >= nfreq
    ang = jnp.where(use_col[None, :], cols[:, None], rows[:, None]) * inv[fidx][None, :]
    cos, sin = jnp.cos(ang), jnp.sin(ang)
    sin = jnp.where(is_b[None, :], sin, -sin)
    cos = jnp.concatenate([cos, jnp.ones((S - T, LANE), F32)], axis=0)
    sin = jnp.concatenate([sin, jnp.zeros((S - T, LANE), F32)], axis=0)
    return jnp.tile(cos, (B, 1)), jnp.tile(sin, (B, 1))


def _attend_t(qts, ks, vts, scale):
    c = scale * LOG2E
    scores = [_dot(k, qt) for qt, k in zip(qts, ks)]
    outs = []
    for s, vt in zip(scores, vts):
        m = jnp.max(s, axis=0, keepdims=True)
        p = jnp.exp2(s * c - m * c)
        l = jnp.sum(p, axis=0, keepdims=True)
        outs.append(_dot(vt, p.astype(BF16)) / l)
    return outs


def _store_vt(vt_ref, v):
    for h in range(vt_ref.shape[0]):
        vt_ref[h] = v[:, h * LANE:(h + 1) * LANE].T


GQA_KV_PER_STEP = 2


def _gqa_kernel(q_ref, k_ref, v_ref, o_ref, vt_ref, *, T, R, scale):
    i = pl.program_id(2)

    @pl.when(i == 0)
    def _():
        _store_vt(vt_ref, v_ref[...])

    def attend(lo):
        n_heads = q_ref.shape[1] // LANE
        qts = [q_ref[:, h * LANE:(h + 1) * LANE].T for h in range(n_heads)]
        ks = [k_ref[lo:, (h // R) * LANE:(h // R + 1) * LANE] for h in range(n_heads)]
        outs = _attend_t(qts, ks, [vt_ref[h // R, :, lo:] for h in range(n_heads)], scale)
        for h in range(n_heads):
            o_ref[:, h * LANE:(h + 1) * LANE] = outs[h].T.astype(o_ref.dtype)

    @pl.when(i < T // ROW_TILE)
    def _():
        attend(0)

    @pl.when(i >= T // ROW_TILE)
    def _():
        attend(T)


def _gqa(proj, T, Tq, scale):
    B, S, _ = proj.shape
    R = GQA_HEADS // GQA_KV
    G = GQA_KV_PER_STEP
    qb, kb, vb = C_QB // (G * R * LANE), C_KB // (G * LANE), C_VB // (G * LANE)
    return pl.pallas_call(
        functools.partial(_gqa_kernel, T=T, R=R, scale=scale),
        grid=(B, GQA_KV // G, Tq // ROW_TILE),
        in_specs=[pl.BlockSpec((None, ROW_TILE, G * R * LANE), lambda b, g, i: (b, i, qb + g)),
                  pl.BlockSpec((None, S, G * LANE), lambda b, g, i: (b, 0, kb + g)),
                  pl.BlockSpec((None, S, G * LANE), lambda b, g, i: (b, 0, vb + g))],
        out_specs=pl.BlockSpec((None, ROW_TILE, G * R * LANE), lambda b, g, i: (b, i, g)),
        out_shape=jax.ShapeDtypeStruct((B, Tq, GQA_HEADS * LANE), BF16),
        scratch_shapes=[pltpu.VMEM((G, LANE, S), BF16)],
        compiler_params=_cp("parallel", "parallel", "arbitrary"), name="gqa",
    )(proj, proj, proj)


DF_HEADS_PER_STEP = 2


def _diff_kernel(lam_ref, q_ref, k_ref, v_ref, g_ref, o_ref, vt_ref, *, T, scale, out_scale):
    i = pl.program_id(2)
    lam = lam_ref[0]

    @pl.when(i == 0)
    def _():
        _store_vt(vt_ref, v_ref[...])

    def attend(lo):
        qts, ks, vts = [], [], []
        for h in range(DF_HEADS_PER_STEP):
            qt = q_ref[:, h * LANE:(h + 1) * LANE].T
            ch = lax.broadcasted_iota(jnp.int32, qt.shape, 0)
            zero = jnp.zeros_like(qt)
            qts += [jnp.where(ch < DF_QK_DIM, qt, zero), jnp.where(ch >= DF_QK_DIM, qt, zero)]
            ks += [k_ref[lo:, h * LANE:(h + 1) * LANE]] * 2
            vts += [vt_ref[h, :, lo:]] * 2
        outs = _attend_t(qts, ks, vts, scale)
        for h in range(DF_HEADS_PER_STEP):
            o = (outs[2 * h] - lam * outs[2 * h + 1]).T
            y = o * lax.rsqrt(jnp.mean(o * o, axis=-1, keepdims=True) + NORM_EPS) * g_ref[...]
            o_ref[:, h * LANE:(h + 1) * LANE] = (y * out_scale).astype(o_ref.dtype)

    @pl.when(i < T // ROW_TILE)
    def _():
        attend(0)

    @pl.when(i >= T // ROW_TILE)
    def _():
        attend(T)


def _diff_attn(lam, proj, subln, T, Tq, out_scale):
    B, S, _ = proj.shape
    W = DF_HEADS_PER_STEP * LANE
    qb, kb, vb = C_QD // W, C_KD // W, C_VD // W
    return pl.pallas_call(
        functools.partial(_diff_kernel, T=T, scale=DF_SCALE, out_scale=out_scale),
        grid=(B, DF_HEADS // DF_HEADS_PER_STEP, Tq // ROW_TILE),
        in_specs=[pl.BlockSpec(memory_space=pltpu.SMEM),
                  pl.BlockSpec((None, ROW_TILE, W), lambda b, h, i: (b, i, qb + h)),
                  pl.BlockSpec((None, S, W), lambda b, h, i: (b, 0, kb + h)),
                  pl.BlockSpec((None, S, W), lambda b, h, i: (b, 0, vb + h)),
                  pl.BlockSpec((1, LANE), lambda b, h, i: (0, 0))],
        out_specs=pl.BlockSpec((None, ROW_TILE, W), lambda b, h, i: (b, i, h)),
        out_shape=jax.ShapeDtypeStruct((B, Tq, DF_HEADS * LANE), BF16),
        scratch_shapes=[pltpu.VMEM((DF_HEADS_PER_STEP, LANE, S), BF16)],
        compiler_params=_cp("parallel", "parallel", "arbitrary"), name="diff_attn",
    )(lam.reshape(1).astype(F32), proj, proj, proj, subln.astype(F32).reshape(1, LANE))


NA_HEADS_PER_STEP = 4


def _na_kernel(q_ref, k_ref, v_ref, bias_ref, o_ref, *, T, scale):
    i = pl.program_id(2)
    n_groups = T // ROW_TILE
    grid_rows = T // GRID_W
    c = scale * LOG2E
    heads = [slice(h * LANE, (h + 1) * LANE) for h in range(NA_HEADS_PER_STEP)]
    qts = [q_ref[:, hs].T for hs in heads]
    u_cs = [_dot(k_ref[T:, hs], qt) * c for hs, qt in zip(heads, qts)]
    vcts = [v_ref[T:, hs].T for hs in heads]

    @pl.when(i < n_groups)
    def _():
        row0 = jnp.clip(i * NA_GROUP_ROWS - NA_WIN_H // 2, 0, grid_rows - NA_WIN_ROWS)
        win = pl.ds(pl.multiple_of(row0 * GRID_W, ROW_TILE), NA_WIN_ROWS * GRID_W)
        u_ns = [_dot(k_ref[win, hs], qt) * c + bias_ref[h] for h, (hs, qt) in enumerate(zip(heads, qts))]
        for h, hs in enumerate(heads):
            u_n, u_c = u_ns[h], u_cs[h]
            m = jnp.maximum(jnp.max(u_n, axis=0, keepdims=True), jnp.max(u_c, axis=0, keepdims=True))
            p_n, p_c = jnp.exp2(u_n - m), jnp.exp2(u_c - m)
            l = jnp.sum(p_n, axis=0, keepdims=True) + jnp.sum(p_c, axis=0, keepdims=True)
            ot = (_dot(v_ref[win, hs].T, p_n.astype(BF16)) + _dot(vcts[h], p_c.astype(BF16))) / l
            o_ref[:, hs] = ot.T.astype(o_ref.dtype)

    @pl.when(i >= n_groups)
    def _():
        for h, hs in enumerate(heads):
            u_c = u_cs[h]
            p = jnp.exp2(u_c - jnp.max(u_c, axis=0, keepdims=True))
            ot = _dot(vcts[h], p.astype(BF16)) / jnp.sum(p, axis=0, keepdims=True)
            o_ref[:, hs] = ot.T.astype(o_ref.dtype)


def _na_bias(rpb, T):
    rows = T // GRID_W
    n_groups = rows // NA_GROUP_ROWS
    n_roff, n_coff = 2 * NA_WIN_H - 1, 2 * NA_WIN_W - 1
    c = np.arange(GRID_W)[:, None]
    kc = np.arange(GRID_W)[None, :]
    cs = np.clip(c - NA_WIN_W // 2, 0, GRID_W - NA_WIN_W)
    col_valid = (kc >= cs) & (kc < cs + NA_WIN_W)
    coff = np.clip(kc - c + NA_WIN_W - 1, 0, n_coff - 1)
    col_sel = (coff[..., None] == np.arange(n_coff)).astype(np.float32)
    rpb = rpb.astype(F32) * LOG2E
    out = []
    for grp in (0, 1, n_groups - 1):
        r = grp * NA_GROUP_ROWS + np.arange(NA_GROUP_ROWS)[:, None]
        rs = np.clip(r - NA_WIN_H // 2, 0, rows - NA_WIN_H)
        row0 = np.clip(grp * NA_GROUP_ROWS - NA_WIN_H // 2, 0, rows - NA_WIN_ROWS)
        key_row = row0 + np.arange(NA_WIN_ROWS)[None, :]
        row_valid = (key_row >= rs) & (key_row < rs + NA_WIN_H)
        roff = np.clip(key_row - r + NA_WIN_H - 1, 0, n_roff - 1)
        row_sel = (roff[..., None] == np.arange(n_roff)).astype(np.float32)
        bias = jnp.einsum('ika,hab,cqb->hkqic', row_sel, rpb, col_sel, precision=lax.Precision.HIGHEST)
        valid = row_valid.T[:, None, :, None] & col_valid.T[None, :, None, :]
        out.append(jnp.where(valid[None], bias, NEG_INF).reshape(rpb.shape[0], -1, ROW_TILE))
    return jnp.stack(out)


def _na(proj, bias, T, Tq):
    B, S, _ = proj.shape
    W = NA_HEADS_PER_STEP * LANE
    qb, kb, vb = C_QC // W, C_KC // W, C_VC // W
    n_groups = T // ROW_TILE
    nk = NA_WIN_ROWS * GRID_W

    def bias_map(b, h, i):
        return (jnp.where(i == 0, 0, jnp.where(i >= n_groups - 1, 2, 1)), h, 0, 0)

    return pl.pallas_call(
        functools.partial(_na_kernel, T=T, scale=ATT_SCALE),
        grid=(B, NA_HEADS // NA_HEADS_PER_STEP, Tq // ROW_TILE),
        in_specs=[pl.BlockSpec((None, ROW_TILE, W), lambda b, h, i: (b, i, qb + h)),
                  pl.BlockSpec((None, S, W), lambda b, h, i: (b, 0, kb + h)),
                  pl.BlockSpec((None, S, W), lambda b, h, i: (b, 0, vb + h)),
                  pl.BlockSpec((None, NA_HEADS_PER_STEP, nk, ROW_TILE), bias_map)],
        out_specs=pl.BlockSpec((None, ROW_TILE, W), lambda b, h, i: (b, i, h)),
        out_shape=jax.ShapeDtypeStruct((B, Tq, NA_HEADS * LANE), BF16),
        compiler_params=_cp("parallel", "parallel", "arbitrary"), name="na_attn",
    )(proj, proj, proj, bias)


def _shortconv_kernel(*refs, T, with_ctx):
    u_refs, w_refs, o_refs = refs[0:3], refs[3:6], refs[6:]
    S = u_refs[0].shape[0]
    row = lax.broadcasted_iota(jnp.int32, (S, 1), 0)
    first = (row == 0) | (row == T)
    last = (row == T - 1) | (row == S - 1)
    for n in range(3):
        u = u_refs[n][...].astype(F32)
        w = w_refs[n][...]
        prev = jnp.where(first, 0.0, pltpu.roll(u, 1, 0))
        nxt = jnp.where(last, 0.0, pltpu.roll(u, S - 1, 0))
        y = (prev * w[0:1] + u * w[1:2] + nxt * w[2:3]).astype(BF16)
        o_refs[n][...] = y[:T]
        if with_ctx:
            o_refs[3 + n][...] = y[T:]


def _shortconv(proj, w, T, with_ctx):
    B, S, _ = proj.shape
    Lc = S - T
    nct = HY_W // LANE
    in_specs = [pl.BlockSpec((None, S, LANE), functools.partial(lambda b, c, n: (b, 0, n * nct + c), n=n))
                for n in range(3)]
    in_specs += [pl.BlockSpec((3, LANE), functools.partial(lambda b, c, n: (0, n * nct + c), n=n))
                 for n in range(3)]
    out_specs = [pl.BlockSpec((T, LANE), lambda b, c: (0, b * nct + c))] * 3
    out_shape = [jax.ShapeDtypeStruct((T, B * HY_W), BF16)] * 3
    if with_ctx:
        out_specs += [pl.BlockSpec((Lc, LANE), lambda b, c: (0, b * nct + c))] * 3
        out_shape += [jax.ShapeDtypeStruct((Lc, B * HY_W), BF16)] * 3
    return pl.pallas_call(
        functools.partial(_shortconv_kernel, T=T, with_ctx=with_ctx),
        grid=(B, nct), in_specs=in_specs, out_specs=out_specs, out_shape=out_shape,
        compiler_params=_cp("parallel", "arbitrary"), name="hy_shortconv",
    )(proj, proj, proj, w, w, w)


def _filter_hidden_kernel(z_ref, f1_ref, b1_ref, fr_ref, f2_ref, b2_ref, o_ref):
    fr = fr_ref[...]
    hid = jnp.sin(fr * (_dot_hi(z_ref[...], f1_ref[...]) + b1_ref[...]))
    o_ref[...] = jnp.sin(fr * (_dot_hi(hid, f2_ref[...]) + b2_ref[...]))


def _filter_kernel(h_ref, hb_ref, t_ref, tb_ref, f3f_ref, f3b_ref, dl_ref, *o_refs, circular):
    fw = _dot_hi(h_ref[...], f3f_ref[...]) * jnp.exp(-t_ref[...] * dl_ref[...])
    bw = _dot_hi(hb_ref[...], f3b_ref[...]) * jnp.exp(-tb_ref[...] * dl_ref[...])
    row = lax.broadcasted_iota(jnp.int32, bw.shape, 0)
    bw = jnp.where(row == 0, 0.0, bw)
    inv = 1.0 / (jnp.sum(jnp.abs(fw), axis=0, keepdims=True) + jnp.sum(jnp.abs(bw), axis=0, keepdims=True))
    if circular:
        o_refs[0][0] = (fw * inv).astype(BF16)
        o_refs[0][1] = (bw * inv).astype(BF16)
    else:
        o_refs[0][...] = ((fw + bw) * inv).astype(BF16)
        o_refs[1][...] = ((fw - bw) * inv).astype(BF16)


def _hyena_filters(L, f1, b1, freq, f2, b2, f3, circular):
    t = jnp.linspace(0.0, 1.0, L, dtype=F32)[:, None]
    w = (2.0 * math.pi / L) * jnp.arange(L, dtype=F32)[:, None]
    bands = jnp.linspace(1e-4, HY_BANDS - 1, HY_BANDS, dtype=F32)[None, :]
    z = jnp.concatenate([t, jnp.cos(bands * w), -jnp.sin(bands * w)], axis=-1)
    z = jnp.pad(z, ((0, 0), (0, LANE - HY_EMB)))
    f1p = jnp.pad(f1.astype(F32), ((0, LANE - HY_EMB), (0, 0)))
    ffn = f1.shape[1]
    small = lambda shape: pl.BlockSpec(shape, lambda j: (0, 0))
    hid = pl.pallas_call(
        _filter_hidden_kernel,
        grid=(1,),
        in_specs=[small((L, LANE)), small((LANE, ffn)), small((1, ffn)), small((1, ffn)), small((ffn, ffn)),
                  small((1, ffn))],
        out_specs=small((L, ffn)), out_shape=jax.ShapeDtypeStruct((L, ffn), F32),
        compiler_params=_cp("arbitrary"), name="hy_filter_hidden",
    )(z, f1p, b1.astype(F32).reshape(1, ffn), freq.astype(F32).reshape(1, ffn), f2.astype(F32),
      b2.astype(F32).reshape(1, ffn))
    back = (lambda a: jnp.roll(jnp.flip(a, axis=0), 1, axis=0)) if circular else (lambda a: a)
    deltas = jnp.abs(jnp.linspace(math.log(HY_TARGET) / HY_SLOW_DECAY, math.log(HY_TARGET) / HY_FAST_DECAY,
                                  HY_W, dtype=F32))
    NC = HY_ORDER * HY_W
    dl = jnp.tile(deltas, HY_ORDER).reshape(1, NC)
    tn = 256
    if circular:
        out_specs = [pl.BlockSpec((2, L, tn), lambda j: (0, 0, j))]
        out_shape = [jax.ShapeDtypeStruct((2, L, NC), BF16)]
    else:
        out_specs = [pl.BlockSpec((L, tn), lambda j: (0, j))] * 2
        out_shape = [jax.ShapeDtypeStruct((L, NC), BF16)] * 2
    out = pl.pallas_call(
        functools.partial(_filter_kernel, circular=circular),
        grid=(NC // tn,),
        in_specs=[small((L, ffn)), small((L, ffn)), small((L, 1)), small((L, 1)),
                  pl.BlockSpec((ffn, tn), lambda j: (0, j)),
                  pl.BlockSpec((ffn, tn), lambda j: (0, NC // tn + j)),
                  pl.BlockSpec((1, tn), lambda j: (0, j))],
        out_specs=out_specs, out_shape=out_shape,
        compiler_params=_cp("arbitrary"), name="hy_filter",
    )(hid, back(hid), t, back(t), f3.astype(F32), f3.astype(F32), dl)
    return out[0].reshape(2 * L, NC) if circular else out


def _dft_tables(L):
    N = 2 * L
    r = np.arange(N)
    k = (r // (2 * DFT_HALF)) * DFT_HALF + r % DFT_HALF
    is_im = (r // DFT_HALF) % 2 == 1
    nyq = is_im & (k == 0)
    kj = jnp.asarray(k, jnp.int32)[:, None]
    im, nyq = is_im[:, None], nyq[:, None]
    n_lo = 64

    def cos_sin(step, count):
        m = (kj * (step * jnp.arange(count, dtype=jnp.int32))[None, :]) % N
        ang = m.astype(F32) * (2.0 * math.pi / N)
        return jnp.cos(ang), jnp.sin(ang)

    c1, s1 = cos_sin(n_lo, L // n_lo)
    c0, s0 = cos_sin(1, n_lo)
    p1 = jnp.where(nyq, 1.0, jnp.where(im, -s1, c1))
    q1 = jnp.where(nyq, 0.0, jnp.where(im, -c1, -s1))
    p0 = jnp.where(nyq, (1 - 2 * (jnp.arange(n_lo) % 2)).astype(F32)[None, :], c0)
    a = (p1[:, :, None] * p0[:, None, :] + q1[:, :, None] * s0[:, None, :]).reshape(N, L).astype(BF16)
    return a, a.T


def _kf_kernel(a_ref, hs_ref, hd_ref, o_ref, *, n_fft):
    i = pl.program_id(0)
    H = DFT_HALF
    re = _dot(a_ref[:H, :], hs_ref[...])
    im = _dot(a_ref[H:, :], hd_ref[...])
    o_ref[:H, :] = re * (2.0 / n_fft)
    o_ref[H:, :] = im * (2.0 / n_fft)

    @pl.when(i == 0)
    def _():
        ny = _dot(a_ref[H:H + 16, :], hs_ref[...])
        o_ref[0:1, :] = re[0:1] * (1.0 / n_fft)
        o_ref[H:H + 1, :] = ny[0:1] * (1.0 / n_fft)


def _filter_spectrum(a, hs, hd):
    N, L = a.shape
    NC = hs.shape[1]
    tm, tn = 2 * DFT_HALF, 512
    return pl.pallas_call(
        functools.partial(_kf_kernel, n_fft=N),
        grid=(N // tm, NC // tn),
        in_specs=[pl.BlockSpec((tm, L), lambda i, j: (i, 0)),
                  pl.BlockSpec((L, tn), lambda i, j: (0, j)),
                  pl.BlockSpec((L, tn), lambda i, j: (0, j))],
        out_specs=pl.BlockSpec((tm, tn), lambda i, j: (i, j)),
        out_shape=jax.ShapeDtypeStruct((N, NC), F32),
        compiler_params=_cp("parallel", "arbitrary"), name="hy_filter_spectrum",
    )(a, hs, hd)


def _fwd_kernel(a_ref, z_ref, kf_ref, p_ref):
    i = pl.program_id(0)
    H = DFT_HALF
    acc = _dot(a_ref[...], z_ref[...])
    zr, zi = acc[:H], acc[H:]
    kr, ki = kf_ref[:H, :], kf_ref[H:, :]
    row = lax.broadcasted_iota(jnp.int32, zr.shape, 0)
    real_pair = (row == 0) & (i == 0)
    p_ref[:H, :] = jnp.where(real_pair, zr * kr, zr * kr - zi * ki).astype(BF16)
    p_ref[H:, :] = jnp.where(real_pair, zi * ki, zr * ki + zi * kr).astype(BF16)


def _dft_multiply(a, z, kf, order):
    N, L = a.shape
    NB = z.shape[1] // HY_W
    tm = 2 * DFT_HALF
    return pl.pallas_call(
        _fwd_kernel,
        grid=(N // tm, NB),
        in_specs=[pl.BlockSpec((tm, L), lambda i, j: (i, 0)),
                  pl.BlockSpec((L, HY_W), lambda i, j: (0, j)),
                  pl.BlockSpec((tm, HY_W), lambda i, j: (i, order))],
        out_specs=pl.BlockSpec((tm, HY_W), lambda i, j: (i, j)),
        out_shape=jax.ShapeDtypeStruct((N, NB * HY_W), BF16),
        compiler_params=_cp("parallel", "arbitrary"), name="hy_dft_multiply",
    )(a, z, kf)


def _inv_kernel(at_ref, p_ref, x_ref, z_ref, b_ref, o_ref):
    y = _dot(at_ref[...], p_ref[...])
    z = z_ref[...].astype(F32)
    o_ref[...] = (x_ref[...].astype(F32) * (y + b_ref[...] * z)).astype(o_ref.dtype)


def _idft_gate(at, p, xo, z, bias):
    L, N = at.shape
    NB = z.shape[1] // HY_W
    tm = min(L, 512)
    return pl.pallas_call(
        _inv_kernel,
        grid=(L // tm, NB),
        in_specs=[pl.BlockSpec((tm, N), lambda i, j: (i, 0)),
                  pl.BlockSpec((N, HY_W), lambda i, j: (0, j)),
                  pl.BlockSpec((tm, HY_W), lambda i, j: (i, j)),
                  pl.BlockSpec((tm, HY_W), lambda i, j: (i, j)),
                  pl.BlockSpec((1, HY_W), lambda i, j: (0, 0))],
        out_specs=pl.BlockSpec((tm, HY_W), lambda i, j: (i, j)),
        out_shape=jax.ShapeDtypeStruct((L, NB * HY_W), BF16),
        compiler_params=_cp("parallel", "arbitrary"), name="hy_idft_gate",
    )(at, p, xo, z, bias.astype(F32).reshape(1, HY_W))


def _hyena(vxx, tables, filt, hy_bias):
    a, at = tables
    kf = _filter_spectrum(a, *filt)
    z = vxx[0]
    for o in range(HY_ORDER):
        p = _dft_multiply(a, z, kf, o)
        z = _idft_gate(at, p, vxx[1 + o], z, hy_bias[o])
    return z


FFT_N2 = LANE
FFT_K1B = 8
FFT_LANES = 8192


def _fft_tables(L):
    N = 2 * L
    N1 = N // FFT_N2
    KH = -(-(N1 // 2 + 1) // FFT_K1B) * FFT_K1B
    k1 = jnp.arange(KH, dtype=jnp.int32)
    n1 = jnp.arange(N1, dtype=jnp.int32)
    ang1 = ((k1[:, None] * n1[None, :]) % N1).astype(F32) * (2.0 * math.pi / N1)
    c1, s1 = jnp.cos(ang1), jnp.sin(ang1)
    f1 = jnp.concatenate([c1, -s1], axis=0).astype(BF16)
    mult = jnp.where((k1 == 0) | (k1 == N1 // 2), 1.0, jnp.where(k1 < N1 // 2, 2.0, 0.0))[:, None]
    g2 = jnp.concatenate([c1 * mult, -s1 * mult], axis=0)[:, :N1 // 2].T.astype(BF16)
    k2 = jnp.arange(FFT_N2, dtype=jnp.int32)
    kk = k1[:, None, None] + N1 * k2[None, :, None]
    th = ((kk * k2[None, None, :]) % N).astype(F32) * (2.0 * math.pi / N)
    mr, mi = jnp.cos(th), -jnp.sin(th)
    wf = jnp.concatenate([jnp.concatenate([mr, -mi], axis=2),
                          jnp.concatenate([mi, mr], axis=2)], axis=1).astype(BF16)
    return f1, wf, jnp.swapaxes(wf, 1, 2), g2


def _fft_s1_kernel(f_ref, x_ref, o_ref):
    o_ref[...] = _dot(f_ref[...], x_ref[...]).astype(o_ref.dtype)


def _fft_stage1(f, x2d):
    R, K = f.shape
    M2 = x2d.shape[1]
    tn = math.gcd(M2, FFT_LANES)
    return pl.pallas_call(
        _fft_s1_kernel,
        grid=(M2 // tn,),
        in_specs=[pl.BlockSpec((R, K), lambda j: (0, 0)),
                  pl.BlockSpec((K, tn), lambda j: (0, j))],
        out_specs=pl.BlockSpec((R, tn), lambda j: (0, j)),
        out_shape=jax.ShapeDtypeStruct((R, M2), BF16),
        compiler_params=_cp("arbitrary"), name="hy_fft_stage1",
    )(f, x2d)


def _fft_s2_kernel(*refs, with_filter, scale):
    if with_filter:
        w_ref, a_ref, kf_ref, o_ref = refs
    else:
        w_ref, a_ref, o_ref = refs
    H = FFT_N2
    for kk in range(FFT_K1B):
        a = jnp.concatenate([a_ref[0, kk], a_ref[1, kk]], axis=0)
        x = _dot(w_ref[kk], a)
        xr, xi = x[:H], x[H:]
        if with_filter:
            kr, ki = kf_ref[0, kk], kf_ref[1, kk]
            xr, xi = xr * kr - xi * ki, xr * ki + xi * kr
        elif scale != 1.0:
            xr, xi = xr * scale, xi * scale
        o_ref[0, kk] = xr.astype(o_ref.dtype)
        o_ref[1, kk] = xi.astype(o_ref.dtype)


def _fft_stage2(w, a4, kf4=None, order=0, out_dtype=BF16, scale=1.0):
    _, KH, N2, cols = a4.shape
    tn = HY_W
    blk = lambda col: pl.BlockSpec((2, FFT_K1B, N2, tn), col)
    in_specs = [pl.BlockSpec((FFT_K1B, 2 * N2, 2 * N2), lambda i, j: (i, 0, 0)),
                blk(lambda i, j: (0, i, 0, j))]
    args = [w, a4]
    if kf4 is not None:
        in_specs.append(blk(lambda i, j: (0, i, 0, order)))
        args.append(kf4)
    return pl.pallas_call(
        functools.partial(_fft_s2_kernel, with_filter=kf4 is not None, scale=scale),
        grid=(KH // FFT_K1B, cols // tn),
        in_specs=in_specs,
        out_specs=blk(lambda i, j: (0, i, 0, j)),
        out_shape=jax.ShapeDtypeStruct(a4.shape, out_dtype),
        compiler_params=_cp("parallel", "arbitrary"), name="hy_fft_stage2",
    )(*args)


def _ifft_s2_kernel(g_ref, q_ref, x_ref, z_ref, b_ref, o_ref):
    y = _dot(g_ref[...], q_ref[...])
    z = z_ref[...].astype(F32)
    o_ref[...] = (x_ref[...].astype(F32) * (y + b_ref[...] * z)).astype(o_ref.dtype)


def _ifft_stage2_gate(g2, q4, xo, z, bias):
    _, KH, N2, cols = q4.shape
    H1 = g2.shape[0]
    L = z.shape[0]
    M2 = N2 * cols
    tn = math.gcd(M2, FFT_LANES)
    bias_t = jnp.tile(bias.astype(F32), tn // HY_W).reshape(1, tn)
    row = pl.BlockSpec((H1, tn), lambda j: (0, j))
    out = pl.pallas_call(
        _ifft_s2_kernel,
        grid=(M2 // tn,),
        in_specs=[pl.BlockSpec((H1, 2 * KH), lambda j: (0, 0)),
                  pl.BlockSpec((2 * KH, tn), lambda j: (0, j)),
                  row, row,
                  pl.BlockSpec((1, tn), lambda j: (0, 0))],
        out_specs=row,
        out_shape=jax.ShapeDtypeStruct((H1, M2), BF16),
        compiler_params=_cp("arbitrary"), name="hy_ifft_stage2_gate",
    )(g2, q4.reshape(2 * KH, M2), xo.reshape(H1, M2), z.reshape(H1, M2), bias_t)
    return out.reshape(L, cols)


def _hyena_fft(vxx, tables, kern, hy_bias):
    f1, wf, wi, g2 = tables
    L, cols = vxx[0].shape
    KH, N1 = f1.shape[0] // 2, f1.shape[1]
    N = 2 * L
    NC = kern.shape[1]
    ka = _fft_stage1(f1, kern.reshape(N1, FFT_N2 * NC)).reshape(2, KH, FFT_N2, NC)
    kf = _fft_stage2(wf, ka, out_dtype=F32, scale=1.0 / N)
    z = vxx[0]
    for o in range(HY_ORDER):
        a = _fft_stage1(f1[:, :N1 // 2], z.reshape(N1 // 2, FFT_N2 * cols)).reshape(2, KH, FFT_N2, cols)
        p = _fft_stage2(wf, a, kf4=kf, order=o)
        q = _fft_stage2(wi, p)
        z = _ifft_stage2_gate(g2, q, vxx[1 + o], z, hy_bias[o])
    return z


def _merge_kernel(ya_ref, yb_ref, yc_ref, yd_ref, g0_ref, g1_ref, g2_ref, g3_ref, wb_ref, wo_ref, x_ref, gate_ref,
                  o_ref):
    ys = (ya_ref, yb_ref, yc_ref, yd_ref)
    gs = (g0_ref, g1_ref, g2_ref, g3_ref)
    acc = _dot(ys[0][...], wb_ref[0]) * gs[0][...].astype(F32)
    for n in range(1, N_BRANCH):
        acc += _dot(ys[n][...], wb_ref[n]) * gs[n][...].astype(F32)
    o_ref[...] = x_ref[...] + gate_ref[...] * _dot(acc.astype(BF16), wo_ref[...])


def _merge(ys, gates, wb, wo, l, x, modt, q_gate, tm, row_tile0, mod_row):
    B, R, D = x.shape
    row = lambda b, i: (b, row_tile0 + i, 0)
    in_specs = [pl.BlockSpec((tm, BRANCH_W), lambda b, i: (i, b))]
    in_specs += [pl.BlockSpec((None, tm, BRANCH_W), row)] * (N_BRANCH - 1)
    in_specs += [pl.BlockSpec((None, tm, D), functools.partial(lambda b, i, n: (b, row_tile0 + i, n), n=n))
                 for n in range(N_BRANCH)]
    in_specs += [pl.BlockSpec((None, N_BRANCH, BRANCH_W, D), lambda b, i: (l, 0, 0, 0),
                              pipeline_mode=pl.Buffered(1)),
                 pl.BlockSpec((None, D, D), lambda b, i: (l, 0, 0), pipeline_mode=pl.Buffered(1)),
                 pl.BlockSpec((None, tm, D), lambda b, i: (b, i, 0)),
                 pl.BlockSpec((None, None, 1, D), lambda b, i: (mod_row(b), q_gate, 0, 0))]
    return pl.pallas_call(
        _merge_kernel,
        grid=(B, R // tm), in_specs=in_specs,
        out_specs=pl.BlockSpec((None, tm, D), lambda b, i: (b, i, 0)),
        out_shape=jax.ShapeDtypeStruct((B, R, D), F32),
        compiler_params=_cp("parallel", "arbitrary"), name="merge",
    )(*ys, gates, gates, gates, gates, wb, wo, x, modt)


def _expert_up_kernel(x_ref, w1_ref, w3_ref, o_ref):
    x = x_ref[...]
    a = _dot(x, w1_ref[...].astype(BF16))
    b = _dot(x, w3_ref[...].astype(BF16))
    o_ref[...] = (a * _sigmoid(a) * b).astype(o_ref.dtype)


def _expert_down_kernel(h_ref, g_ref, w2_ref, o_ref):
    o_ref[...] = (_dot(h_ref[...], w2_ref[...].astype(BF16)) * g_ref[...]).astype(o_ref.dtype)


def _experts(xg, gate, w1, w3, w2, l):
    E, M, D = xg.shape
    F = w1.shape[3]
    tf = math.gcd(F, 256)
    td = math.gcd(D, 512)
    hmid = pl.pallas_call(
        _expert_up_kernel,
        grid=(E, F // tf),
        in_specs=[pl.BlockSpec((None, M, D), lambda e, f: (e, 0, 0)),
                  pl.BlockSpec((None, None, D, tf), lambda e, f: (l, e, 0, f)),
                  pl.BlockSpec((None, None, D, tf), lambda e, f: (l, e, 0, f))],
        out_specs=pl.BlockSpec((None, M, tf), lambda e, f: (e, 0, f)),
        out_shape=jax.ShapeDtypeStruct((E, M, F), BF16),
        compiler_params=_cp("parallel", "arbitrary"), name="expert_up",
    )(xg, w1, w3)
    return pl.pallas_call(
        _expert_down_kernel,
        grid=(E, D // td),
        in_specs=[pl.BlockSpec((None, M, F), lambda e, n: (e, 0, 0)),
                  pl.BlockSpec((None, M, 1), lambda e, n: (e, 0, 0)),
                  pl.BlockSpec((None, None, F, td), lambda e, n: (l, e, 0, n))],
        out_specs=pl.BlockSpec((None, M, td), lambda e, n: (e, 0, n)),
        out_shape=jax.ShapeDtypeStruct((E, M, D), BF16),
        compiler_params=_cp("parallel", "arbitrary"), name="expert_down",
    )(hmid, gate, w2)


def _combine_kernel(tok_ref, y_ref, x_ref, g_ref, o_ref, *, row0):
    E, Ct, tn = y_ref.shape
    rowid = row0 + pl.program_id(2) * ROW_TILE + lax.broadcasted_iota(jnp.int32, (ROW_TILE, 1), 0)
    onehot = jnp.where(tok_ref[...] == rowid, 1.0, 0.0).astype(BF16)
    o_ref[...] = x_ref[...] + g_ref[...] * _dot(onehot, y_ref[...].reshape(E * Ct, tn))


def _combine(tok, y, x, modt, q_gate, row0, mod_row):
    B, R, D = x.shape
    E, _, Ct, _ = y.shape
    tn = math.gcd(D, PROJ_TILE_N)
    return pl.pallas_call(
        functools.partial(_combine_kernel, row0=row0),
        grid=(B, D // tn, R // ROW_TILE),
        in_specs=[pl.BlockSpec((None, 1, E * Ct), lambda b, n, t: (b, 0, 0)),
                  pl.BlockSpec((E, None, Ct, tn), lambda b, n, t: (0, b, 0, n)),
                  pl.BlockSpec((None, ROW_TILE, tn), lambda b, n, t: (b, t, n)),
                  pl.BlockSpec((None, None, 1, tn), lambda b, n, t: (mod_row(b), q_gate, 0, n))],
        out_specs=pl.BlockSpec((None, ROW_TILE, tn), lambda b, n, t: (b, t, n)),
        out_shape=jax.ShapeDtypeStruct((B, R, D), F32),
        compiler_params=_cp("parallel", "parallel", "arbitrary"), name="moe_combine",
    )(tok, y, x, modt)


def _moe(h2, aff, T, Lc, w1, w3, w2, l):
    B, S, D = h2.shape
    E = w1.shape[1]

    def route(a):
        n = a.shape[1]
        return lax.top_k(a.transpose(0, 2, 1), CAPACITY_FACTOR * n // E)

    gate, tok = route(aff[:, :T, :E])
    if Lc:
        gate_c, tok_c = route(aff[:, T:, :E])
        gate, tok = jnp.concatenate([gate, gate_c], axis=-1), jnp.concatenate([tok, T + tok_c], axis=-1)
    Ct = tok.shape[-1]
    rows = (tok + (jnp.arange(B, dtype=jnp.int32) * S)[:, None, None]).transpose(1, 0, 2).reshape(E, B * Ct)
    xg = h2.reshape(B * S, D)[rows]
    y = _experts(xg, gate.transpose(1, 0, 2).reshape(E, B * Ct, 1), w1, w3, w2, l)
    return tok.reshape(B, 1, E * Ct), y.reshape(E, B, Ct, D)


def kernel(x, c, ctx, c_ctx, w_mod, b_mod, norm1, norm2, w_in, hy_short, hy_f1, hy_b1, hy_freq, hy_f2, hy_b2, hy_f3, hy_bias, qn_b, kn_b, qn_c, kn_c, rpb_c, qn_d, kn_d, lam_q1, lam_k1, lam_q2, lam_k2, subln_d, w_branch, w_out, w_router, w_e1, w_e3, w_e2):
    B, T, D = x.shape
    Lc = ctx.shape[1]
    S = T + Lc
    depth = w_mod.shape[0]
    assert T % ROW_TILE == 0 and Lc == ROW_TILE and T // GRID_W >= NA_WIN_ROWS

    cc = jnp.concatenate([c, c_ctx[None, :], jnp.zeros((8 - B - 1, D), F32)], axis=0)
    rope_h = _rope_tables(B, T, S, HEAD_DIM)
    rope_d = _rope_tables(B, T, S, DF_QK_DIM)
    use_fft = (2 * T // FFT_N2) % 32 == 0
    dft_lat = _fft_tables(T) if use_fft else _dft_tables(T)
    dft_ctx = _dft_tables(Lc)
    wb, wo = w_branch.astype(BF16), w_out.astype(BF16)

    for l in range(depth):
        last = l == depth - 1
        n_ctx = 0 if last else 1
        Tq = T + n_ctx * ROW_TILE
        lam_init = 0.8 - 0.6 * math.exp(-0.3 * l)
        lam = (jnp.exp(jnp.sum(lam_q1[l].astype(F32) * lam_k1[l].astype(F32)))
               - jnp.exp(jnp.sum(lam_q2[l].astype(F32) * lam_k2[l].astype(F32))) + lam_init)
        modt = _modulation(cc, w_mod, b_mod, l).reshape(8, 6, 1, D)

        h = _prenorm(x, ctx, norm1[l], modt, 1, 0, 1)
        h2d = h.reshape(B * S, D)
        gains = _proj_gains(qn_b[l], kn_b[l], qn_c[l], kn_c[l], qn_d[l], kn_d[l])
        proj = _in_proj(h2d, w_in, l, gains, rope_h, rope_d).reshape(B, S, C_GATES)
        gates = _matmul(h2d, w_in, l, C_GATES, N_BRANCH * D, BF16, sigmoid=True,
                        name="in_proj_gates").reshape(B, S, N_BRANCH * D)

        y_b = _gqa(proj, T, Tq, ATT_SCALE)
        y_c = _na(proj, _na_bias(rpb_c[l], T), T, Tq)
        y_d = _diff_attn(lam, proj, subln_d[l], T, Tq, 1.0 - lam_init)

        conv = _shortconv(proj, hy_short[l].astype(F32), T, with_ctx=not last)
        filt = _hyena_filters(T, hy_f1[l], hy_b1[l], hy_freq[l], hy_f2[l], hy_b2[l], hy_f3[l], circular=use_fft)
        y_a = (_hyena_fft if use_fft else _hyena)(conv[:3], dft_lat, filt, hy_bias[l])
        x_mid = _merge((y_a, y_b, y_c, y_d), gates, wb, wo, l, x, modt, 2, ROW_TILE, 0, lambda b: b)
        if not last:
            filt_c = _hyena_filters(Lc, hy_f1[l], hy_b1[l], hy_freq[l], hy_f2[l], hy_b2[l], hy_f3[l],
                                    circular=False)
            y_a_ctx = _hyena(conv[3:], dft_ctx, filt_c, hy_bias[l])
            ctx = _merge((y_a_ctx, y_b, y_c, y_d), gates, wb, wo, l, ctx, modt, 2, ROW_TILE, T // ROW_TILE,
                         lambda b: B)
        x = x_mid

        h2, aff = _prenorm(x, ctx, norm2[l], modt, 4, 3, n_ctx, w_router=w_router[l].astype(F32))
        tok, y = _moe(h2, aff, T, n_ctx * Lc, w_e1, w_e3, w_e2, l)
        x = _combine(tok, y, x, modt, 5, 0, lambda b: b)
        if not last:
            ctx = _combine(tok, y, ctx, modt, 5, T, lambda b: B)
    return x
```

```python
import functools
import math

import numpy as np
import jax
import jax.numpy as jnp
from jax import lax
from jax.experimental import pallas as pl
from jax.experimental.pallas import tpu as pltpu

F32 = jnp.float32
BF16 = jnp.bfloat16

GRID_W = 64
HEAD_DIM = 128
BRANCH_W = 512
N_BRANCH = 4
HY_W = 512
HY_ORDER = 2
HY_BANDS = 16
HY_EMB = 1 + 2 * HY_BANDS
HY_FAST_DECAY = 0.3
HY_SLOW_DECAY = 1.5
HY_TARGET = 1e-2
GQA_HEADS = 4
GQA_KV = 2
NA_HEADS = 4
NA_WIN_H = 8
NA_WIN_W = 16
DF_HEADS = 4
DF_QK_DIM = 64
CAPACITY_FACTOR = 2
ROPE_THETA = 10000.0
NORM_EPS = 1e-6
ATT_SCALE = HEAD_DIM ** -0.5
DF_SCALE = DF_QK_DIM ** -0.5
NEG_INF = -1e30
LOG2E = math.log2(math.e)

LANE = 128
ROW_TILE = 256
PROJ_TILE_N = 512
NA_GROUP_ROWS = 4
NA_WIN_ROWS = NA_GROUP_ROWS + NA_WIN_H
DFT_HALF = 256
VMEM_LIMIT =56 * 1024 * 1024

C_QB = 3 * HY_W
C_KB = C_QB + GQA_HEADS * HEAD_DIM
C_VB = C_KB + GQA_KV * HEAD_DIM
C_QC = C_VB + GQA_KV * HEAD_DIM
C_KC = C_QC + NA_HEADS * HEAD_DIM
C_VC = C_KC + NA_HEADS * HEAD_DIM
C_QD = C_VC + NA_HEADS * HEAD_DIM
C_KD = C_QD + DF_HEADS * 2 * DF_QK_DIM
C_VD = C_KD + DF_HEADS * 2 * DF_QK_DIM
C_GATES = C_VD + DF_HEADS * HEAD_DIM


def _cp(*sem):
    return pltpu.CompilerParams(dimension_semantics=sem, vmem_limit_bytes=VMEM_LIMIT)


def _dot(a, b):
    return jnp.dot(a, b, preferred_element_type=F32)


def _dot_hi(a, b):
    return jnp.dot(a, b, preferred_element_type=F32, precision=lax.Precision.HIGHEST)


def _sigmoid(x):
    return 0.5 * jnp.tanh(0.5 * x) + 0.5


def _mod_kernel(c_ref, w_ref, b_ref, o_ref):
    c = c_ref[...]
    a = (c * jax.nn.sigmoid(c)).astype(BF16)
    o_ref[...] = _dot(a, w_ref[...].astype(BF16)) + b_ref[...]


def _modulation(cc, w, b, l):
    depth, D, N = w.shape
    tn = math.gcd(N, 1024)
    return pl.pallas_call(
        _mod_kernel,
        grid=(N // tn,),
        in_specs=[pl.BlockSpec((8, D), lambda j: (0, 0)),
                  pl.BlockSpec((None, D, tn), lambda j: (l, 0, j)),
                  pl.BlockSpec((None, 1, tn), lambda j: (l, 0, j))],
        out_specs=pl.BlockSpec((8, tn), lambda j: (0, j)),
        out_shape=jax.ShapeDtypeStruct((8, N), F32),
        compiler_params=_cp("arbitrary"),
        name="modulation",
    )(cc, w, b.reshape(depth, 1, N))


def _prenorm_body(x, g_ref, sc_ref, sh_ref):
    y = x * lax.rsqrt(jnp.mean(x * x, axis=-1, keepdims=True) + NORM_EPS)
    return y * g_ref[...] * (1.0 + sc_ref[...]) + sh_ref[...]


def _prenorm_kernel(x_ref, c_ref, g_ref, sc_ref, sh_ref, o_ref, *, n_lat):
    i = pl.program_id(1)

    @pl.when(i < n_lat)
    def _():
        o_ref[...] = _prenorm_body(x_ref[...], g_ref, sc_ref, sh_ref).astype(BF16)

    @pl.when(i >= n_lat)
    def _():
        o_ref[...] = _prenorm_body(c_ref[...], g_ref, sc_ref, sh_ref).astype(BF16)


def _prenorm_router_kernel(x_ref, c_ref, g_ref, sc_ref, sh_ref, wr_ref, o_ref, a_ref, *, n_lat, n_exp):
    i = pl.program_id(1)

    def run(x):
        h = _prenorm_body(x, g_ref, sc_ref, sh_ref)
        o_ref[...] = h.astype(BF16)
        logits = _dot_hi(h, wr_ref[...])
        lane = lax.broadcasted_iota(jnp.int32, logits.shape, 1)
        logits = jnp.where(lane < n_exp, logits, NEG_INF)
        e = jnp.exp(logits - jnp.max(logits, axis=-1, keepdims=True))
        a_ref[...] = e / jnp.sum(e, axis=-1, keepdims=True)

    @pl.when(i < n_lat)
    def _():
        run(x_ref[...])

    @pl.when(i >= n_lat)
    def _():
        run(c_ref[...])


def _prenorm(x, ctx, gain, modt, q_scale, q_shift, n_ctx_tiles, w_router=None):
    B, T, D = x.shape
    n_lat = T // ROW_TILE
    nt = n_lat + n_ctx_tiles
    S = nt * ROW_TILE

    def mod_map(q):
        return lambda b, i: (jnp.where(i < n_lat, b, B), q, 0, 0)

    in_specs = [
        pl.BlockSpec((None, ROW_TILE, D), lambda b, i: (b, jnp.minimum(i, n_lat - 1), 0)),
        pl.BlockSpec((None, ROW_TILE, D), lambda b, i: (b, jnp.maximum(i - n_lat, 0), 0)),
        pl.BlockSpec((1, D), lambda b, i: (0, 0)),
        pl.BlockSpec((None, None, 1, D), mod_map(q_scale)),
        pl.BlockSpec((None, None, 1, D), mod_map(q_shift)),
    ]
    out_h = pl.BlockSpec((None, ROW_TILE, D), lambda b, i: (b, i, 0))
    shape_h = jax.ShapeDtypeStruct((B, S, D), BF16)
    args = [x, ctx, gain.reshape(1, D), modt, modt]
    if w_router is None:
        return pl.pallas_call(
            functools.partial(_prenorm_kernel, n_lat=n_lat),
            grid=(B, nt), in_specs=in_specs, out_specs=out_h, out_shape=shape_h,
            compiler_params=_cp("parallel", "arbitrary"), name="prenorm",
        )(*args)
    n_exp = w_router.shape[1]
    wr = jnp.pad(w_router, ((0, 0), (0, LANE - n_exp)))
    return pl.pallas_call(
        functools.partial(_prenorm_router_kernel, n_lat=n_lat, n_exp=n_exp),
        grid=(B, nt),
        in_specs=in_specs + [pl.BlockSpec((D, LANE), lambda b, i: (0, 0))],
        out_specs=[out_h, pl.BlockSpec((None, ROW_TILE, LANE), lambda b, i: (b, i, 0))],
        out_shape=[shape_h, jax.ShapeDtypeStruct((B, S, LANE), F32)],
        compiler_params=_cp("parallel", "arbitrary"), name="prenorm_router",
    )(*args, wr)


def _mm_kernel(a_ref, b_ref, o_ref, *, sigmoid):
    acc = _dot(a_ref[...], b_ref[...].astype(BF16))
    if sigmoid:
        acc = _sigmoid(acc)
    o_ref[...] = acc.astype(o_ref.dtype)


def _matmul(a, b, l, col0, N, out_dtype, sigmoid=False, name="matmul"):
    M, K = a.shape
    tm, tn = math.gcd(M, 1024), PROJ_TILE_N
    cb = col0 // tn
    return pl.pallas_call(
        functools.partial(_mm_kernel, sigmoid=sigmoid),
        grid=(M // tm, N // tn),
        in_specs=[pl.BlockSpec((tm, K), lambda i, j: (i, 0)),
                  pl.BlockSpec((None, K, tn), lambda i, j: (l, 0, cb + j))],
        out_specs=pl.BlockSpec((tm, tn), lambda i, j: (i, j)),
        out_shape=jax.ShapeDtypeStruct((M, N), out_dtype),
        compiler_params=_cp("parallel", "arbitrary"), name=name,
    )(a, b)


def _qk_post(x, g, seg, tables):
    lane = lax.broadcasted_iota(jnp.int32, (1, LANE), 1)
    sq = x * x
    if seg == LANE:
        ms = jnp.mean(sq, axis=-1, keepdims=True)
    else:
        lo = jnp.sum(jnp.where(lane < seg, sq, 0.0), axis=-1, keepdims=True)
        hi = jnp.sum(jnp.where(lane >= seg, sq, 0.0), axis=-1, keepdims=True)
        ms = jnp.where(lane < seg, lo, hi) * (1.0 / seg)
    y = x * lax.rsqrt(ms + NORM_EPS) * g
    if tables is not None:
        cos_ref, sin_ref = tables
        q = seg // 4
        partner = jnp.where((lane % (seg // 2)) < q, pltpu.roll(y, LANE - q, 1), pltpu.roll(y, q, 1))
        y = y * cos_ref[...] + partner * sin_ref[...]
    return y


def _proj_block_kinds():
    kinds = []
    for col in range(0, C_GATES, LANE):
        if C_QB <= col < C_VB:
            kinds.append((HEAD_DIM, True))
        elif C_QC <= col < C_VC:
            kinds.append((HEAD_DIM, False))
        elif C_QD <= col < C_VD:
            kinds.append((DF_QK_DIM, True))
        else:
            kinds.append(None)
    return kinds


def _inproj_kernel(a_ref, w_ref, g_ref, c128_ref, s128_ref, c64_ref, s64_ref, o_ref):
    j = pl.program_id(1)
    acc = _dot(a_ref[...], w_ref[...].astype(BF16))
    per_tile = PROJ_TILE_N // LANE
    kinds = _proj_block_kinds()
    tiles = [kinds[t * per_tile:(t + 1) * per_tile] for t in range(len(kinds) // per_tile)]
    plain = functools.reduce(jnp.logical_or, [j == t for t, ks in enumerate(tiles) if not any(ks)])

    @pl.when(plain)
    def _():
        o_ref[...] = acc.astype(o_ref.dtype)

    for t, ks in enumerate(tiles):
        if not any(ks):
            continue

        @pl.when(j == t)
        def _(ks=ks):
            for h, kind in enumerate(ks):
                y = acc[:, h * LANE:(h + 1) * LANE]
                if kind is not None:
                    seg, rope = kind
                    tables = None if not rope else ((c128_ref, s128_ref) if seg == HEAD_DIM else (c64_ref, s64_ref))
                    y = _qk_post(y, g_ref[h:h + 1, :], seg, tables)
                o_ref[:, h * LANE:(h + 1) * LANE] = y.astype(o_ref.dtype)


def _in_proj(h2d, w_in_bf, l, gains, rope_h, rope_d):
    M, K = h2d.shape
    tm, tn = math.gcd(M, 1024), PROJ_TILE_N
    per_tile = tn // LANE
    table = pl.BlockSpec((tm, LANE), lambda i, j: (i, 0))
    return pl.pallas_call(
        _inproj_kernel,
        grid=(M // tm, C_GATES // tn),
        in_specs=[pl.BlockSpec((tm, K), lambda i, j: (i, 0)),
                  pl.BlockSpec((None, K, tn), lambda i, j: (l, 0, j)),
                  pl.BlockSpec((None, per_tile, LANE), lambda i, j: (j, 0, 0)),
                  table, table, table, table],
        out_specs=pl.BlockSpec((tm, tn), lambda i, j: (i, j)),
        out_shape=jax.ShapeDtypeStruct((M, C_GATES), BF16),
        compiler_params=_cp("parallel", "arbitrary"), name="in_proj",
    )(h2d, w_in_bf, gains.reshape(-1, per_tile, LANE), *rope_h, *rope_d)


def _proj_gains(qn_b, kn_b, qn_c, kn_c, qn_d, kn_d):
    ones = lambda n: jnp.ones((n, LANE), F32)
    rep = lambda g, n: jnp.tile(jnp.tile(g.astype(F32), LANE // g.shape[0])[None, :], (n, 1))
    return jnp.concatenate([
        ones(C_QB // LANE), rep(qn_b, GQA_HEADS), rep(kn_b, GQA_KV) * (ATT_SCALE * LOG2E), ones(GQA_KV),
        rep(qn_c, NA_HEADS), rep(kn_c, NA_HEADS) * (ATT_SCALE * LOG2E), ones(NA_HEADS),
        rep(qn_d, DF_HEADS), rep(kn_d, DF_HEADS) * (DF_SCALE * LOG2E), ones(DF_HEADS)], axis=0)


def _rope_tables(B, T, S, seg):
    half = seg // 2
    nfreq = half // 2
    inv = ROPE_THETA ** (-jnp.arange(0, half, 2, dtype=F32) / half)
    pos = jnp.arange(T, dtype=jnp.int32)
    rows, cols = (pos // GRID_W).astype(F32), (pos % GRID_W).astype(F32)
    l = np.arange(LANE) % seg
    use_col = l >= half
    fidx = (l % half) % nfreq
    is_b = (l % half) >= nfreq
    ang = jnp.where(use_col[None, :], cols[:, None], rows[:, None]) * inv[fidx][None, :]
    cos, sin = jnp.cos(ang), jnp.sin(ang)
    sin = jnp.where(is_b[None, :], sin, -sin)
    cos = jnp.concatenate([cos, jnp.ones((S - T, LANE), F32)], axis=0)
    sin = jnp.concatenate([sin, jnp.zeros((S - T, LANE), F32)], axis=0)
    return jnp.tile(cos, (B, 1)), jnp.tile(sin, (B, 1))


def _attend_t(qts, ks, vts):
    scores = [_dot(k, qt) for qt, k in zip(qts, ks)]
    outs = []
    for s, vt in zip(scores, vts):
        m = jnp.max(s, axis=0, keepdims=True)
        p = jnp.exp2(s - m)
        l = jnp.sum(p, axis=0, keepdims=True)
        outs.append(_dot(vt, p.astype(BF16)) / l)
    return outs


def _store_vt(vt_ref, v):
    for h in range(vt_ref.shape[0]):
        vt_ref[h] = v[:, h * LANE:(h + 1) * LANE].T


GQA_KV_PER_STEP = 2


def _gqa_kernel(q_ref, k_ref, v_ref, o_ref, vt_ref, *, T, R):
    i = pl.program_id(2)

    @pl.when(i == 0)
    def _():
        _store_vt(vt_ref, v_ref[...])

    def attend(lo):
        n_heads = q_ref.shape[1] // LANE
        qts = [q_ref[:, h * LANE:(h + 1) * LANE].T for h in range(n_heads)]
        ks = [k_ref[lo:, (h // R) * LANE:(h // R + 1) * LANE] for h in range(n_heads)]
        outs = _attend_t(qts, ks, [vt_ref[h // R, :, lo:] for h in range(n_heads)])
        for h in range(n_heads):
            o_ref[:, h * LANE:(h + 1) * LANE] = outs[h].T.astype(o_ref.dtype)

    @pl.when(i < T // ROW_TILE)
    def _():
        attend(0)

    @pl.when(i >= T // ROW_TILE)
    def _():
        attend(T)


def _gqa(proj, T, Tq):
    B, S, _ = proj.shape
    R = GQA_HEADS // GQA_KV
    G = GQA_KV_PER_STEP
    qb, kb, vb = C_QB // (G * R * LANE), C_KB // (G * LANE), C_VB // (G * LANE)
    return pl.pallas_call(
        functools.partial(_gqa_kernel, T=T, R=R),
        grid=(B, GQA_KV // G, Tq // ROW_TILE),
        in_specs=[pl.BlockSpec((None, ROW_TILE, G * R * LANE), lambda b, g, i: (b, i, qb + g)),
                  pl.BlockSpec((None, S, G * LANE), lambda b, g, i: (b, 0, kb + g)),
                  pl.BlockSpec((None, S, G * LANE), lambda b, g, i: (b, 0, vb + g))],
        out_specs=pl.BlockSpec((None, ROW_TILE, G * R * LANE), lambda b, g, i: (b, i, g)),
        out_shape=jax.ShapeDtypeStruct((B, Tq, GQA_HEADS * LANE), BF16),
        scratch_shapes=[pltpu.VMEM((G, LANE, S), BF16)],
        compiler_params=_cp("parallel", "parallel", "arbitrary"), name="gqa",
    )(proj, proj, proj)


DF_HEADS_PER_STEP = 2


def _diff_kernel(lam_ref, q_ref, k_ref, v_ref, g_ref, o_ref, vt_ref, *, T, out_scale):
    i = pl.program_id(2)
    lam = lam_ref[0]

    @pl.when(i == 0)
    def _():
        _store_vt(vt_ref, v_ref[...])

    def attend(lo):
        qts, ks, vts = [], [], []
        for h in range(DF_HEADS_PER_STEP):
            qt = q_ref[:, h * LANE:(h + 1) * LANE].T
            ch = lax.broadcasted_iota(jnp.int32, qt.shape, 0)
            zero = jnp.zeros_like(qt)
            qts += [jnp.where(ch < DF_QK_DIM, qt, zero), jnp.where(ch >= DF_QK_DIM, qt, zero)]
            ks += [k_ref[lo:, h * LANE:(h + 1) * LANE]] * 2
            vts += [vt_ref[h, :, lo:]] * 2
        outs = _attend_t(qts, ks, vts)
        for h in range(DF_HEADS_PER_STEP):
            o = (outs[2 * h] - lam * outs[2 * h + 1]).T
            y = o * lax.rsqrt(jnp.mean(o * o, axis=-1, keepdims=True) + NORM_EPS) * g_ref[...]
            o_ref[:, h * LANE:(h + 1) * LANE] = (y * out_scale).astype(o_ref.dtype)

    @pl.when(i < T // ROW_TILE)
    def _():
        attend(0)

    @pl.when(i >= T // ROW_TILE)
    def _():
        attend(T)


def _diff_attn(lam, proj, subln, T, Tq, out_scale):
    B, S, _ = proj.shape
    W = DF_HEADS_PER_STEP * LANE
    qb, kb, vb = C_QD // W, C_KD // W, C_VD // W
    return pl.pallas_call(
        functools.partial(_diff_kernel, T=T, out_scale=out_scale),
        grid=(B, DF_HEADS // DF_HEADS_PER_STEP, Tq // ROW_TILE),
        in_specs=[pl.BlockSpec(memory_space=pltpu.SMEM),
                  pl.BlockSpec((None, ROW_TILE, W), lambda b, h, i: (b, i, qb + h)),
                  pl.BlockSpec((None, S, W), lambda b, h, i: (b, 0, kb + h)),
                  pl.BlockSpec((None, S, W), lambda b, h, i: (b, 0, vb + h)),
                  pl.BlockSpec((1, LANE), lambda b, h, i: (0, 0))],
        out_specs=pl.BlockSpec((None, ROW_TILE, W), lambda b, h, i: (b, i, h)),
        out_shape=jax.ShapeDtypeStruct((B, Tq, DF_HEADS * LANE), BF16),
        scratch_shapes=[pltpu.VMEM((DF_HEADS_PER_STEP, LANE, S), BF16)],
        compiler_params=_cp("parallel", "parallel", "arbitrary"), name="diff_attn",
    )(lam.reshape(1).astype(F32), proj, proj, proj, subln.astype(F32).reshape(1, LANE))


NA_HEADS_PER_STEP = 4


def _na_kernel(q_ref, k_ref, v_ref, bias_ref, o_ref, *, T):
    i = pl.program_id(2)
    n_groups = T // ROW_TILE
    grid_rows = T // GRID_W
    heads = [slice(h * LANE, (h + 1) * LANE) for h in range(NA_HEADS_PER_STEP)]
    qts = [q_ref[:, hs].T for hs in heads]
    u_cs = [_dot(k_ref[T:, hs], qt) for hs, qt in zip(heads, qts)]
    vcts = [v_ref[T:, hs].T for hs in heads]

    @pl.when(i < n_groups)
    def _():
        row0 = jnp.clip(i * NA_GROUP_ROWS - NA_WIN_H // 2, 0, grid_rows - NA_WIN_ROWS)
        win = pl.ds(pl.multiple_of(row0 * GRID_W, ROW_TILE), NA_WIN_ROWS * GRID_W)
        u_ns = [_dot(k_ref[win, hs], qt) + bias_ref[h] for h, (hs, qt) in enumerate(zip(heads, qts))]
        for h, hs in enumerate(heads):
            u_n, u_c = u_ns[h], u_cs[h]
            m = jnp.maximum(jnp.max(u_n, axis=0, keepdims=True), jnp.max(u_c, axis=0, keepdims=True))
            p_n, p_c = jnp.exp2(u_n - m), jnp.exp2(u_c - m)
            l = jnp.sum(p_n, axis=0, keepdims=True) + jnp.sum(p_c, axis=0, keepdims=True)
            ot = (_dot(v_ref[win, hs].T, p_n.astype(BF16)) + _dot(vcts[h], p_c.astype(BF16))) / l
            o_ref[:, hs] = ot.T.astype(o_ref.dtype)

    @pl.when(i >= n_groups)
    def _():
        for h, hs in enumerate(heads):
            u_c = u_cs[h]
            p = jnp.exp2(u_c - jnp.max(u_c, axis=0, keepdims=True))
            ot = _dot(vcts[h], p.astype(BF16)) / jnp.sum(p, axis=0, keepdims=True)
            o_ref[:, hs] = ot.T.astype(o_ref.dtype)


def _na_bias(rpb, T):
    rows = T // GRID_W
    n_groups = rows // NA_GROUP_ROWS
    n_roff, n_coff = 2 * NA_WIN_H - 1, 2 * NA_WIN_W - 1
    c = np.arange(GRID_W)[:, None]
    kc = np.arange(GRID_W)[None, :]
    cs = np.clip(c - NA_WIN_W // 2, 0, GRID_W - NA_WIN_W)
    col_valid = (kc >= cs) & (kc < cs + NA_WIN_W)
    coff = np.clip(kc - c + NA_WIN_W - 1, 0, n_coff - 1)
    col_sel = (coff[..., None] == np.arange(n_coff)).astype(np.float32)
    rpb = rpb.astype(F32) * LOG2E
    out = []
    for grp in (0, 1, n_groups - 1):
        r = grp * NA_GROUP_ROWS + np.arange(NA_GROUP_ROWS)[:, None]
        rs = np.clip(r - NA_WIN_H // 2, 0, rows - NA_WIN_H)
        row0 = np.clip(grp * NA_GROUP_ROWS - NA_WIN_H // 2, 0, rows - NA_WIN_ROWS)
        key_row = row0 + np.arange(NA_WIN_ROWS)[None, :]
        row_valid = (key_row >= rs) & (key_row < rs + NA_WIN_H)
        roff = np.clip(key_row - r + NA_WIN_H - 1, 0, n_roff - 1)
        row_sel = (roff[..., None] == np.arange(n_roff)).astype(np.float32)
        bias = jnp.einsum('ika,hab,cqb->hkqic', row_sel, rpb, col_sel, precision=lax.Precision.HIGHEST)
        valid = row_valid.T[:, None, :, None] & col_valid.T[None, :, None, :]
        out.append(jnp.where(valid[None], bias, NEG_INF).reshape(rpb.shape[0], -1, ROW_TILE))
    return jnp.stack(out)


def _na(proj, bias, T, Tq):
    B, S, _ = proj.shape
    W = NA_HEADS_PER_STEP * LANE
    qb, kb, vb = C_QC // W, C_KC // W, C_VC // W
    n_groups = T // ROW_TILE
    nk = NA_WIN_ROWS * GRID_W

    def bias_map(b, h, i):
        return (jnp.where(i == 0, 0, jnp.where(i >= n_groups - 1, 2, 1)), h, 0, 0)

    return pl.pallas_call(
        functools.partial(_na_kernel, T=T),
        grid=(B, NA_HEADS // NA_HEADS_PER_STEP, Tq // ROW_TILE),
        in_specs=[pl.BlockSpec((None, ROW_TILE, W), lambda b, h, i: (b, i, qb + h)),
                  pl.BlockSpec((None, S, W), lambda b, h, i: (b, 0, kb + h)),
                  pl.BlockSpec((None, S, W), lambda b, h, i: (b, 0, vb + h)),
                  pl.BlockSpec((None, NA_HEADS_PER_STEP, nk, ROW_TILE), bias_map)],
        out_specs=pl.BlockSpec((None, ROW_TILE, W), lambda b, h, i: (b, i, h)),
        out_shape=jax.ShapeDtypeStruct((B, Tq, NA_HEADS * LANE), BF16),
        compiler_params=_cp("parallel", "parallel", "arbitrary"), name="na_attn",
    )(proj, proj, proj, bias)


def _shortconv_kernel(*refs, T, with_ctx):
    u_refs, w_refs, o_refs = refs[0:3], refs[3:6], refs[6:]
    S = u_refs[0].shape[0]
    row = lax.broadcasted_iota(jnp.int32, (S, 1), 0)
    first = (row == 0) | (row == T)
    last = (row == T - 1) | (row == S - 1)
    for n in range(3):
        u = u_refs[n][...].astype(F32)
        w = w_refs[n][...]
        prev = jnp.where(first, 0.0, pltpu.roll(u, 1, 0))
        nxt = jnp.where(last, 0.0, pltpu.roll(u, S - 1, 0))
        y = (prev * w[0:1] + u * w[1:2] + nxt * w[2:3]).astype(BF16)
        o_refs[n][...] = y[:T]
        if with_ctx:
            o_refs[3 + n][...] = y[T:]


def _shortconv(proj, w, T, with_ctx):
    B, S, _ = proj.shape
    Lc = S - T
    nct = HY_W // LANE
    in_specs = [pl.BlockSpec((None, S, LANE), functools.partial(lambda b, c, n: (b, 0, n * nct + c), n=n))
                for n in range(3)]
    in_specs += [pl.BlockSpec((3, LANE), functools.partial(lambda b, c, n: (0, n * nct + c), n=n))
                 for n in range(3)]
    out_specs = [pl.BlockSpec((T, LANE), lambda b, c: (0, b * nct + c))] * 3
    out_shape = [jax.ShapeDtypeStruct((T, B * HY_W), BF16)] * 3
    if with_ctx:
        out_specs += [pl.BlockSpec((Lc, LANE), lambda b, c: (0, b * nct + c))] * 3
        out_shape += [jax.ShapeDtypeStruct((Lc, B * HY_W), BF16)] * 3
    return pl.pallas_call(
        functools.partial(_shortconv_kernel, T=T, with_ctx=with_ctx),
        grid=(B, nct), in_specs=in_specs, out_specs=out_specs, out_shape=out_shape,
        compiler_params=_cp("parallel", "arbitrary"), name="hy_shortconv",
    )(proj, proj, proj, w, w, w)


def _filter_hidden_kernel(z_ref, f1_ref, b1_ref, fr_ref, f2_ref, b2_ref, o_ref):
    fr = fr_ref[...]
    hid = jnp.sin(fr * (_dot_hi(z_ref[...], f1_ref[...]) + b1_ref[...]))
    o_ref[...] = jnp.sin(fr * (_dot_hi(hid, f2_ref[...]) + b2_ref[...]))


def _filter_kernel(h_ref, hb_ref, t_ref, tb_ref, f3f_ref, f3b_ref, dl_ref, *o_refs, circular):
    fw = _dot_hi(h_ref[...], f3f_ref[...]) * jnp.exp(-t_ref[...] * dl_ref[...])
    bw = _dot_hi(hb_ref[...], f3b_ref[...]) * jnp.exp(-tb_ref[...] * dl_ref[...])
    row = lax.broadcasted_iota(jnp.int32, bw.shape, 0)
    bw = jnp.where(row == 0, 0.0, bw)
    inv = 1.0 / (jnp.sum(jnp.abs(fw), axis=0, keepdims=True) + jnp.sum(jnp.abs(bw), axis=0, keepdims=True))
    if circular:
        o_refs[0][0] = (fw * inv).astype(BF16)
        o_refs[0][1] = (bw * inv).astype(BF16)
    else:
        o_refs[0][...] = ((fw + bw) * inv).astype(BF16)
        o_refs[1][...] = ((fw - bw) * inv).astype(BF16)


def _hyena_filters(L, f1, b1, freq, f2, b2, f3, circular):
    t = jnp.linspace(0.0, 1.0, L, dtype=F32)[:, None]
    w = (2.0 * math.pi / L) * jnp.arange(L, dtype=F32)[:, None]
    bands = jnp.linspace(1e-4, HY_BANDS - 1, HY_BANDS, dtype=F32)[None, :]
    z = jnp.concatenate([t, jnp.cos(bands * w), -jnp.sin(bands * w)], axis=-1)
    z = jnp.pad(z, ((0, 0), (0, LANE - HY_EMB)))
    f1p = jnp.pad(f1.astype(F32), ((0, LANE - HY_EMB), (0, 0)))
    ffn = f1.shape[1]
    small = lambda shape: pl.BlockSpec(shape, lambda j: (0, 0))
    hid = pl.pallas_call(
        _filter_hidden_kernel,
        grid=(1,),
        in_specs=[small((L, LANE)), small((LANE, ffn)), small((1, ffn)), small((1, ffn)), small((ffn, ffn)),
                  small((1, ffn))],
        out_specs=small((L, ffn)), out_shape=jax.ShapeDtypeStruct((L, ffn), F32),
        compiler_params=_cp("arbitrary"), name="hy_filter_hidden",
    )(z, f1p, b1.astype(F32).reshape(1, ffn), freq.astype(F32).reshape(1, ffn), f2.astype(F32),
      b2.astype(F32).reshape(1, ffn))
    back = (lambda a: jnp.roll(jnp.flip(a, axis=0), 1, axis=0)) if circular else (lambda a: a)
    deltas = jnp.abs(jnp.linspace(math.log(HY_TARGET) / HY_SLOW_DECAY, math.log(HY_TARGET) / HY_FAST_DECAY,
                                  HY_W, dtype=F32))
    NC = HY_ORDER * HY_W
    dl = jnp.tile(deltas, HY_ORDER).reshape(1, NC)
    tn = 256
    if circular:
        out_specs = [pl.BlockSpec((2, L, tn), lambda j: (0, 0, j))]
        out_shape = [jax.ShapeDtypeStruct((2, L, NC), BF16)]
    else:
        out_specs = [pl.BlockSpec((L, tn), lambda j: (0, j))] * 2
        out_shape = [jax.ShapeDtypeStruct((L, NC), BF16)] * 2
    out = pl.pallas_call(
        functools.partial(_filter_kernel, circular=circular),
        grid=(NC // tn,),
        in_specs=[small((L, ffn)), small((L, ffn)), small((L, 1)), small((L, 1)),
                  pl.BlockSpec((ffn, tn), lambda j: (0, j)),
                  pl.BlockSpec((ffn, tn), lambda j: (0, NC // tn + j)),
                  pl.BlockSpec((1, tn), lambda j: (0, j))],
        out_specs=out_specs, out_shape=out_shape,
        compiler_params=_cp("arbitrary"), name="hy_filter",
    )(hid, back(hid), t, back(t), f3.astype(F32), f3.astype(F32), dl)
    return out[0].reshape(2 * L, NC) if circular else out


def _dft_tables(L):
    N = 2 * L
    r = np.arange(N)
    k = (r // (2 * DFT_HALF)) * DFT_HALF + r % DFT_HALF
    is_im = (r // DFT_HALF) % 2 == 1
    nyq = is_im & (k == 0)
    kj = jnp.asarray(k, jnp.int32)[:, None]
    im, nyq = is_im[:, None], nyq[:, None]
    n_lo = 64

    def cos_sin(step, count):
        m = (kj * (step * jnp.arange(count, dtype=jnp.int32))[None, :]) % N
        ang = m.astype(F32) * (2.0 * math.pi / N)
        return jnp.cos(ang), jnp.sin(ang)

    c1, s1 = cos_sin(n_lo, L // n_lo)
    c0, s0 = cos_sin(1, n_lo)
    p1 = jnp.where(nyq, 1.0, jnp.where(im, -s1, c1))
    q1 = jnp.where(nyq, 0.0, jnp.where(im, -c1, -s1))
    p0 = jnp.where(nyq, (1 - 2 * (jnp.arange(n_lo) % 2)).astype(F32)[None, :], c0)
    a = (p1[:, :, None] * p0[:, None, :] + q1[:, :, None] * s0[:, None, :]).reshape(N, L).astype(BF16)
    return a, a.T


def _kf_kernel(a_ref, hs_ref, hd_ref, o_ref, *, n_fft):
    i = pl.program_id(0)
    H = DFT_HALF
    re = _dot(a_ref[:H, :], hs_ref[...])
    im = _dot(a_ref[H:, :], hd_ref[...])
    o_ref[:H, :] = re * (2.0 / n_fft)
    o_ref[H:, :] = im * (2.0 / n_fft)

    @pl.when(i == 0)
    def _():
        ny = _dot(a_ref[H:H + 16, :], hs_ref[...])
        o_ref[0:1, :] = re[0:1] * (1.0 / n_fft)
        o_ref[H:H + 1, :] = ny[0:1] * (1.0 / n_fft)


def _filter_spectrum(a, hs, hd):
    N, L = a.shape
    NC = hs.shape[1]
    tm, tn = 2 * DFT_HALF, 512
    return pl.pallas_call(
        functools.partial(_kf_kernel, n_fft=N),
        grid=(N // tm, NC // tn),
        in_specs=[pl.BlockSpec((tm, L), lambda i, j: (i, 0)),
                  pl.BlockSpec((L, tn), lambda i, j: (0, j)),
                  pl.BlockSpec((L, tn), lambda i, j: (0, j))],
        out_specs=pl.BlockSpec((tm, tn), lambda i, j: (i, j)),
        out_shape=jax.ShapeDtypeStruct((N, NC), F32),
        compiler_params=_cp("parallel", "arbitrary"), name="hy_filter_spectrum",
    )(a, hs, hd)


def _fwd_kernel(a_ref, z_ref, kf_ref, p_ref):
    i = pl.program_id(0)
    H = DFT_HALF
    acc = _dot(a_ref[...], z_ref[...])
    zr, zi = acc[:H], acc[H:]
    kr, ki = kf_ref[:H, :], kf_ref[H:, :]
    row = lax.broadcasted_iota(jnp.int32, zr.shape, 0)
    real_pair = (row == 0) & (i == 0)
    p_ref[:H, :] = jnp.where(real_pair, zr * kr, zr * kr - zi * ki).astype(BF16)
    p_ref[H:, :] = jnp.where(real_pair, zi * ki, zr * ki + zi * kr).astype(BF16)


def _dft_multiply(a, z, kf, order):
    N, L = a.shape
    NB = z.shape[1] // HY_W
    tm = 2 * DFT_HALF
    return pl.pallas_call(
        _fwd_kernel,
        grid=(N // tm, NB),
        in_specs=[pl.BlockSpec((tm, L), lambda i, j: (i, 0)),
                  pl.BlockSpec((L, HY_W), lambda i, j: (0, j)),
                  pl.BlockSpec((tm, HY_W), lambda i, j: (i, order))],
        out_specs=pl.BlockSpec((tm, HY_W), lambda i, j: (i, j)),
        out_shape=jax.ShapeDtypeStruct((N, NB * HY_W), BF16),
        compiler_params=_cp("parallel", "arbitrary"), name="hy_dft_multiply",
    )(a, z, kf)


def _inv_kernel(at_ref, p_ref, x_ref, z_ref, b_ref, o_ref):
    y = _dot(at_ref[...], p_ref[...])
    z = z_ref[...].astype(F32)
    o_ref[...] = (x_ref[...].astype(F32) * (y + b_ref[...] * z)).astype(o_ref.dtype)


def _idft_gate(at, p, xo, z, bias):
    L, N = at.shape
    NB = z.shape[1] // HY_W
    tm = min(L, 512)
    return pl.pallas_call(
        _inv_kernel,
        grid=(L // tm, NB),
        in_specs=[pl.BlockSpec((tm, N), lambda i, j: (i, 0)),
                  pl.BlockSpec((N, HY_W), lambda i, j: (0, j)),
                  pl.BlockSpec((tm, HY_W), lambda i, j: (i, j)),
                  pl.BlockSpec((tm, HY_W), lambda i, j: (i, j)),
                  pl.BlockSpec((1, HY_W), lambda i, j: (0, 0))],
        out_specs=pl.BlockSpec((tm, HY_W), lambda i, j: (i, j)),
        out_shape=jax.ShapeDtypeStruct((L, NB * HY_W), BF16),
        compiler_params=_cp("parallel", "arbitrary"), name="hy_idft_gate",
    )(at, p, xo, z, bias.astype(F32).reshape(1, HY_W))


def _hyena(vxx, tables, filt, hy_bias):
    a, at = tables
    kf = _filter_spectrum(a, *filt)
    z = vxx[0]
    for o in range(HY_ORDER):
        p = _dft_multiply(a, z, kf, o)
        z = _idft_gate(at, p, vxx[1 + o], z, hy_bias[o])
    return z


FFT_N2 = LANE
FFT_K1B = 8
FFT_LANES = 8192


def _fft_tables(L):
    N = 2 * L
    N1 = N // FFT_N2
    KH = -(-(N1 // 2 + 1) // FFT_K1B) * FFT_K1B
    k1 = jnp.arange(KH, dtype=jnp.int32)
    n1 = jnp.arange(N1, dtype=jnp.int32)
    ang1 = ((k1[:, None] * n1[None, :]) % N1).astype(F32) * (2.0 * math.pi / N1)
    c1, s1 = jnp.cos(ang1), jnp.sin(ang1)
    f1 = jnp.concatenate([c1, -s1], axis=0).astype(BF16)
    mult = jnp.where((k1 == 0) | (k1 == N1 // 2), 1.0, jnp.where(k1 < N1 // 2, 2.0, 0.0))[:, None]
    g2 = jnp.concatenate([c1 * mult, -s1 * mult], axis=0)[:, :N1 // 2].T.astype(BF16)
    k2 = jnp.arange(FFT_N2, dtype=jnp.int32)
    kk = k1[:, None, None] + N1 * k2[None, :, None]
    th = ((kk * k2[None, None, :]) % N).astype(F32) * (2.0 * math.pi / N)
    mr, mi = jnp.cos(th), -jnp.sin(th)
    wf = jnp.concatenate([jnp.concatenate([mr, -mi], axis=2),
                          jnp.concatenate([mi, mr], axis=2)], axis=1).astype(BF16)
    return f1, wf, jnp.swapaxes(wf, 1, 2), g2


def _fft_s1_kernel(f_ref, x_ref, o_ref):
    o_ref[...] = _dot(f_ref[...], x_ref[...]).astype(o_ref.dtype)


def _fft_stage1(f, x2d):
    R, K = f.shape
    M2 = x2d.shape[1]
    tn = math.gcd(M2, FFT_LANES)
    return pl.pallas_call(
        _fft_s1_kernel,
        grid=(M2 // tn,),
        in_specs=[pl.BlockSpec((R, K), lambda j: (0, 0)),
                  pl.BlockSpec((K, tn), lambda j: (0, j))],
        out_specs=pl.BlockSpec((R, tn), lambda j: (0, j)),
        out_shape=jax.ShapeDtypeStruct((R, M2), BF16),
        compiler_params=_cp("arbitrary"), name="hy_fft_stage1",
    )(f, x2d)


def _fft_s2_kernel(*refs, with_filter, scale):
    if with_filter:
        w_ref, a_ref, kf_ref, o_ref = refs
    else:
        w_ref, a_ref, o_ref = refs
    H = FFT_N2
    for kk in range(FFT_K1B):
        a = jnp.concatenate([a_ref[0, kk], a_ref[1, kk]], axis=0)
        x = _dot(w_ref[kk], a)
        xr, xi = x[:H], x[H:]
        if with_filter:
            kr, ki = kf_ref[0, kk], kf_ref[1, kk]
            xr, xi = xr * kr - xi * ki, xr * ki + xi * kr
        elif scale != 1.0:
            xr, xi = xr * scale, xi * scale
        o_ref[0, kk] = xr.astype(o_ref.dtype)
        o_ref[1, kk] = xi.astype(o_ref.dtype)


def _fft_stage2(w, a4, kf4=None, order=0, out_dtype=BF16, scale=1.0):
    _, KH, N2, cols = a4.shape
    tn = HY_W
    blk = lambda col: pl.BlockSpec((2, FFT_K1B, N2, tn), col)
    in_specs = [pl.BlockSpec((FFT_K1B, 2 * N2, 2 * N2), lambda i, j: (i, 0, 0)),
                blk(lambda i, j: (0, i, 0, j))]
    args = [w, a4]
    if kf4 is not None:
        in_specs.append(blk(lambda i, j: (0, i, 0, order)))
        args.append(kf4)
    return pl.pallas_call(
        functools.partial(_fft_s2_kernel, with_filter=kf4 is not None, scale=scale),
        grid=(KH // FFT_K1B, cols // tn),
        in_specs=in_specs,
        out_specs=blk(lambda i, j: (0, i, 0, j)),
        out_shape=jax.ShapeDtypeStruct(a4.shape, out_dtype),
        compiler_params=_cp("parallel", "arbitrary"), name="hy_fft_stage2",
    )(*args)


def _ifft_s2_kernel(g_ref, q_ref, x_ref, z_ref, b_ref, o_ref):
    y = _dot(g_ref[...], q_ref[...])
    z = z_ref[...].astype(F32)
    o_ref[...] = (x_ref[...].astype(F32) * (y + b_ref[...] * z)).astype(o_ref.dtype)


def _ifft_stage2_gate(g2, q4, xo, z, bias):
    _, KH, N2, cols = q4.shape
    H1 = g2.shape[0]
    L = z.shape[0]
    M2 = N2 * cols
    tn = math.gcd(M2, FFT_LANES)
    bias_t = jnp.tile(bias.astype(F32), tn // HY_W).reshape(1, tn)
    row = pl.BlockSpec((H1, tn), lambda j: (0, j))
    out = pl.pallas_call(
        _ifft_s2_kernel,
        grid=(M2 // tn,),
        in_specs=[pl.BlockSpec((H1, 2 * KH), lambda j: (0, 0)),
                  pl.BlockSpec((2 * KH, tn), lambda j: (0, j)),
                  row, row,
                  pl.BlockSpec((1, tn), lambda j: (0, 0))],
        out_specs=row,
        out_shape=jax.ShapeDtypeStruct((H1, M2), BF16),
        compiler_params=_cp("arbitrary"), name="hy_ifft_stage2_gate",
    )(g2, q4.reshape(2 * KH, M2), xo.reshape(H1, M2), z.reshape(H1, M2), bias_t)
    return out.reshape(L, cols)


def _hyena_fft(vxx, tables, kern, hy_bias):
    f1, wf, wi, g2 = tables
    L, cols = vxx[0].shape
    KH, N1 = f1.shape[0] // 2, f1.shape[1]
    N = 2 * L
    NC = kern.shape[1]
    ka = _fft_stage1(f1, kern.reshape(N1, FFT_N2 * NC)).reshape(2, KH, FFT_N2, NC)
    kf = _fft_stage2(wf, ka, out_dtype=F32, scale=1.0 / N)
    z = vxx[0]
    for o in range(HY_ORDER):
        a = _fft_stage1(f1[:, :N1 // 2], z.reshape(N1 // 2, FFT_N2 * cols)).reshape(2, KH, FFT_N2, cols)
        p = _fft_stage2(wf, a, kf4=kf, order=o)
        q = _fft_stage2(wi, p)
        z = _ifft_stage2_gate(g2, q, vxx[1 + o], z, hy_bias[o])
    return z


def _merge_kernel(ya_ref, yb_ref, yc_ref, yd_ref, g0_ref, g1_ref, g2_ref, g3_ref, wb_ref, wo_ref, x_ref, gate_ref,
                  o_ref):
    ys = (ya_ref, yb_ref, yc_ref, yd_ref)
    gs = (g0_ref, g1_ref, g2_ref, g3_ref)
    acc = _dot(ys[0][...], wb_ref[0]) * gs[0][...].astype(F32)
    for n in range(1, N_BRANCH):
        acc += _dot(ys[n][...], wb_ref[n]) * gs[n][...].astype(F32)
    o_ref[...] = x_ref[...] + gate_ref[...] * _dot(acc.astype(BF16), wo_ref[...])


def _merge(ys, gates, wb, wo, l, x, modt, q_gate, tm, row_tile0, mod_row):
    B, R, D = x.shape
    row = lambda b, i: (b, row_tile0 + i, 0)
    in_specs = [pl.BlockSpec((tm, BRANCH_W), lambda b, i: (i, b))]
    in_specs += [pl.BlockSpec((None, tm, BRANCH_W), row)] * (N_BRANCH - 1)
    in_specs += [pl.BlockSpec((None, tm, D), functools.partial(lambda b, i, n: (b, row_tile0 + i, n), n=n))
                 for n in range(N_BRANCH)]
    in_specs += [pl.BlockSpec((None, N_BRANCH, BRANCH_W, D), lambda b, i: (l, 0, 0, 0),
                              pipeline_mode=pl.Buffered(1)),
                 pl.BlockSpec((None, D, D), lambda b, i: (l, 0, 0), pipeline_mode=pl.Buffered(1)),
                 pl.BlockSpec((None, tm, D), lambda b, i: (b, i, 0)),
                 pl.BlockSpec((None, None, 1, D), lambda b, i: (mod_row(b), q_gate, 0, 0))]
    return pl.pallas_call(
        _merge_kernel,
        grid=(B, R // tm), in_specs=in_specs,
        out_specs=pl.BlockSpec((None, tm, D), lambda b, i: (b, i, 0)),
        out_shape=jax.ShapeDtypeStruct((B, R, D), F32),
        compiler_params=_cp("parallel", "arbitrary"), name="merge",
    )(*ys, gates, gates, gates, gates, wb, wo, x, modt)


def _expert_up_kernel(x_ref, w1_ref, w3_ref, o_ref):
    x = x_ref[...]
    a = _dot(x, w1_ref[...].astype(BF16))
    b = _dot(x, w3_ref[...].astype(BF16))
    o_ref[...] = (a * _sigmoid(a) * b).astype(o_ref.dtype)


def _expert_down_kernel(h_ref, g_ref, w2_ref, o_ref):
    o_ref[...] = (_dot(h_ref[...], w2_ref[...].astype(BF16)) * g_ref[...]).astype(o_ref.dtype)


def _experts(xg, gate, w1, w3, w2, l):
    E, M, D = xg.shape
    F = w1.shape[3]
    tf = math.gcd(F, 256)
    td = math.gcd(D, 512)
    hmid = pl.pallas_call(
        _expert_up_kernel,
        grid=(E, F // tf),
        in_specs=[pl.BlockSpec((None, M, D), lambda e, f: (e, 0, 0)),
                  pl.BlockSpec((None, None, D, tf), lambda e, f: (l, e, 0, f)),
                  pl.BlockSpec((None, None, D, tf), lambda e, f: (l, e, 0, f))],
        out_specs=pl.BlockSpec((None, M, tf), lambda e, f: (e, 0, f)),
        out_shape=jax.ShapeDtypeStruct((E, M, F), BF16),
        compiler_params=_cp("parallel", "arbitrary"), name="expert_up",
    )(xg, w1, w3)
    return pl.pallas_call(
        _expert_down_kernel,
        grid=(E, D // td),
        in_specs=[pl.BlockSpec((None, M, F), lambda e, n: (e, 0, 0)),
                  pl.BlockSpec((None, M, 1), lambda e, n: (e, 0, 0)),
                  pl.BlockSpec((None, None, F, td), lambda e, n: (l, e, 0, n))],
        out_specs=pl.BlockSpec((None, M, td), lambda e, n: (e, 0, n)),
        out_shape=jax.ShapeDtypeStruct((E, M, D), BF16),
        compiler_params=_cp("parallel", "arbitrary"), name="expert_down",
    )(hmid, gate, w2)


def _combine_kernel(tok_ref, y_ref, x_ref, g_ref, o_ref, *, row0):
    E, Ct, tn = y_ref.shape
    rowid = row0 + pl.program_id(2) * ROW_TILE + lax.broadcasted_iota(jnp.int32, (ROW_TILE, 1), 0)
    onehot = jnp.where(tok_ref[...] == rowid, 1.0, 0.0).astype(BF16)
    o_ref[...] = x_ref[...] + g_ref[...] * _dot(onehot, y_ref[...].reshape(E * Ct, tn))


def _combine(tok, y, x, modt, q_gate, row0, mod_row):
    B, R, D = x.shape
    E, _, Ct, _ = y.shape
    tn = math.gcd(D, PROJ_TILE_N)
    return pl.pallas_call(
        functools.partial(_combine_kernel, row0=row0),
        grid=(B, D // tn, R // ROW_TILE),
        in_specs=[pl.BlockSpec((None, 1, E * Ct), lambda b, n, t: (b, 0, 0)),
                  pl.BlockSpec((E, None, Ct, tn), lambda b, n, t: (0, b, 0, n)),
                  pl.BlockSpec((None, ROW_TILE, tn), lambda b, n, t: (b, t, n)),
                  pl.BlockSpec((None, None, 1, tn), lambda b, n, t: (mod_row(b), q_gate, 0, n))],
        out_specs=pl.BlockSpec((None, ROW_TILE, tn), lambda b, n, t: (b, t, n)),
        out_shape=jax.ShapeDtypeStruct((B, R, D), F32),
        compiler_params=_cp("parallel", "parallel", "arbitrary"), name="moe_combine",
    )(tok, y, x, modt)


def _moe(h2, aff, T, Lc, w1, w3, w2, l):
    B, S, D = h2.shape
    E = w1.shape[1]

    def route(a):
        n = a.shape[1]
        return lax.top_k(a.transpose(0, 2, 1), CAPACITY_FACTOR * n // E)

    gate, tok = route(aff[:, :T, :E])
    if Lc:
        gate_c, tok_c = route(aff[:, T:, :E])
        gate, tok = jnp.concatenate([gate, gate_c], axis=-1), jnp.concatenate([tok, T + tok_c], axis=-1)
    Ct = tok.shape[-1]
    rows = (tok + (jnp.arange(B, dtype=jnp.int32) * S)[:, None, None]).transpose(1, 0, 2).reshape(E, B * Ct)
    xg = h2.reshape(B * S, D)[rows]
    y = _experts(xg, gate.transpose(1, 0, 2).reshape(E, B * Ct, 1), w1, w3, w2, l)
    return tok.reshape(B, 1, E * Ct), y.reshape(E, B, Ct, D)


def kernel(x, c, ctx, c_ctx, w_mod, b_mod, norm1, norm2, w_in, hy_short, hy_f1, hy_b1, hy_freq, hy_f2, hy_b2, hy_f3, hy_bias, qn_b, kn_b, qn_c, kn_c, rpb_c, qn_d, kn_d, lam_q1, lam_k1, lam_q2, lam_k2, subln_d, w_branch, w_out, w_router, w_e1, w_e3, w_e2):
    B, T, D = x.shape
    Lc = ctx.shape[1]
    S = T + Lc
    depth = w_mod.shape[0]
    assert T % ROW_TILE == 0 and Lc == ROW_TILE and T // GRID_W >= NA_WIN_ROWS

    cc = jnp.concatenate([c, c_ctx[None, :], jnp.zeros((8 - B - 1, D), F32)], axis=0)
    rope_h = _rope_tables(B, T, S, HEAD_DIM)
    rope_d = _rope_tables(B, T, S, DF_QK_DIM)
    use_fft = (2 * T // FFT_N2) % 32 == 0
    dft_lat = _fft_tables(T) if use_fft else _dft_tables(T)
    dft_ctx = _dft_tables(Lc)
    wb, wo = w_branch.astype(BF16), w_out.astype(BF16)

    for l in range(depth):
        last = l == depth - 1
        n_ctx = 0 if last else 1
        Tq = T + n_ctx * ROW_TILE
        lam_init = 0.8 - 0.6 * math.exp(-0.3 * l)
        lam = (jnp.exp(jnp.sum(lam_q1[l].astype(F32) * lam_k1[l].astype(F32)))
               - jnp.exp(jnp.sum(lam_q2[l].astype(F32) * lam_k2[l].astype(F32))) + lam_init)
        modt = _modulation(cc, w_mod, b_mod, l).reshape(8, 6, 1, D)

        h = _prenorm(x, ctx, norm1[l], modt, 1, 0, 1)
        h2d = h.reshape(B * S, D)
        gains = _proj_gains(qn_b[l], kn_b[l], qn_c[l], kn_c[l], qn_d[l], kn_d[l])
        proj = _in_proj(h2d, w_in, l, gains, rope_h, rope_d).reshape(B, S, C_GATES)
        gates = _matmul(h2d, w_in, l, C_GATES, N_BRANCH * D, BF16, sigmoid=True,
                        name="in_proj_gates").reshape(B, S, N_BRANCH * D)

        y_b = _gqa(proj, T, Tq)
        y_c = _na(proj, _na_bias(rpb_c[l], T), T, Tq)
        y_d = _diff_attn(lam, proj, subln_d[l], T, Tq, 1.0 - lam_init)

        conv = _shortconv(proj, hy_short[l].astype(F32), T, with_ctx=not last)
        filt = _hyena_filters(T, hy_f1[l], hy_b1[l], hy_freq[l], hy_f2[l], hy_b2[l], hy_f3[l], circular=use_fft)
        y_a = (_hyena_fft if use_fft else _hyena)(conv[:3], dft_lat, filt, hy_bias[l])
        x_mid = _merge((y_a, y_b, y_c, y_d), gates, wb, wo, l, x, modt, 2, ROW_TILE, 0, lambda b: b)
        if not last:
            filt_c = _hyena_filters(Lc, hy_f1[l], hy_b1[l], hy_freq[l], hy_f2[l], hy_b2[l], hy_f3[l],
                                    circular=False)
            y_a_ctx = _hyena(conv[3:], dft_ctx, filt_c, hy_bias[l])
            ctx = _merge((y_a_ctx, y_b, y_c, y_d), gates, wb, wo, l, ctx, modt, 2, ROW_TILE, T // ROW_TILE,
                         lambda b: B)
        x = x_mid

        h2, aff = _prenorm(x, ctx, norm2[l], modt, 4, 3, n_ctx, w_router=w_router[l].astype(F32))
        tok, y = _moe(h2, aff, T, n_ctx * Lc, w_e1, w_e3, w_e2, l)
        x = _combine(tok, y, x, modt, 5, 0, lambda b: b)
        if not last:
            ctx = _combine(tok, y, ctx, modt, 5, T, lambda b: B)
    return x
```
